```python
import jax
import jax.numpy as jnp
from jax import lax
import numpy as np

D_MODEL = 2048
BATCH = 2
SEQ = 16384
DEPTH = 2

N_META = 16
GRID_W = 64
QBLK = 128
EPS = 1e-6
NEG_INF = -1e30

MLA_HEADS = 8
MLA_NOPE = 128
MLA_ROPE = 64
MLA_V = 128
MLA_Q_LORA = 512
MLA_KV_LORA = 512
ROPE_THETA = 10000.0

NA_HEADS = 8
NA_DIM = 128
NA_KH_MAX = 8
NA_KW = 16

IN_AB = MLA_Q_LORA + MLA_KV_LORA + MLA_ROPE + 3 * NA_HEADS * NA_DIM
MIX_AB = MLA_HEADS * MLA_V + NA_HEADS * NA_DIM

SWA_HEADS = 32
SWA_KV_HEADS = 4
SWA_GROUP = SWA_HEADS // SWA_KV_HEADS
SWA_DIM = 64
SWA_WINDOW = 128
IN_C = (SWA_HEADS + 2 * SWA_KV_HEADS) * SWA_DIM
MIX_C = SWA_HEADS * SWA_DIM

N_EXPERTS = 16
EC_CAPACITY_FACTOR = 2
D_EXPERT = 2048

F32 = jnp.float32

kernel_name = "hybrid_mla_natten_swa_ec_encoder"


def rmsnorm(x, g):
    xf = x.astype(F32)
    y = xf * lax.rsqrt(jnp.mean(xf * xf, axis=-1, keepdims=True) + EPS)
    return (y * g.astype(F32)).astype(x.dtype)


def rope_tables(length, dim):
    inv_freq = 1.0 / (ROPE_THETA ** (jnp.arange(0, dim, 2, dtype=F32) / dim))
    ang = jnp.arange(length, dtype=F32)[:, None] * inv_freq[None, :]
    return jnp.cos(ang), jnp.sin(ang)


def apply_rope(x, cos, sin):
    xf = x.astype(F32)
    x1, x2 = jnp.split(xf, 2, axis=-1)
    return jnp.concatenate([x1 * cos - x2 * sin, x2 * cos + x1 * sin], axis=-1).astype(x.dtype)


def mla_attention(c_q, c_kv, k_rope, q_norm, kv_norm, w_uq, w_ukv):
    B, L, _ = c_q.shape
    q = (rmsnorm(c_q, q_norm) @ w_uq).reshape(B, L, MLA_HEADS, MLA_NOPE + MLA_ROPE)
    kv = (rmsnorm(c_kv, kv_norm) @ w_ukv).reshape(B, L, MLA_HEADS, MLA_NOPE + MLA_V)
    q_nope, q_pe = q[..., :MLA_NOPE], q[..., MLA_NOPE:]
    k_nope, v = kv[..., :MLA_NOPE], kv[..., MLA_NOPE:]
    cos, sin = rope_tables(L, MLA_ROPE)
    q_pe = apply_rope(q_pe, cos[None, :, None, :], sin[None, :, None, :])
    k_pe = apply_rope(k_rope, cos[None], sin[None])
    scale = (MLA_NOPE + MLA_ROPE) ** -0.5

    def attend(qn, qp):
        s = (jnp.einsum("bqhd,bkhd->bhqk", qn, k_nope, preferred_element_type=F32)
             + jnp.einsum("bqhr,bkr->bhqk", qp, k_pe, preferred_element_type=F32)) * scale
        p = jax.nn.softmax(s, axis=-1).astype(v.dtype)
        return jnp.einsum("bhqk,bkhd->bqhd", p, v)

    n = L - N_META
    nb = n // QBLK
    out_meta = attend(q_nope[:, :N_META], q_pe[:, :N_META])
    qn_blk = q_nope[:, N_META:].reshape(B, nb, QBLK, MLA_HEADS, MLA_NOPE).transpose(1, 0, 2, 3, 4)
    qp_blk = q_pe[:, N_META:].reshape(B, nb, QBLK, MLA_HEADS, MLA_ROPE).transpose(1, 0, 2, 3, 4)
    out_real = lax.map(lambda a: attend(a[0], a[1]), (qn_blk, qp_blk))
    out_real = out_real.transpose(1, 0, 2, 3, 4).reshape(B, n, MLA_HEADS, MLA_V)
    return jnp.concatenate([out_meta, out_real], axis=1).reshape(B, L, MLA_HEADS * MLA_V)


def neighbourhood_attention(q, k, v, rpb):
    B, L, H, d = q.shape
    n = L - N_META
    rows = n // GRID_W
    kh = min(NA_KH_MAX, rows)
    scale = d ** -0.5
    qm, km, vm = q[:, :N_META], k[:, :N_META], v[:, :N_META]
    qr, kr, vr = q[:, N_META:], k[:, N_META:], v[:, N_META:]

    s_mm = jnp.einsum("bqhd,bkhd->bhqk", qm, km, preferred_element_type=F32) * scale
    out_meta = jnp.einsum("bhqk,bkhd->bqhd", jax.nn.softmax(s_mm, axis=-1).astype(v.dtype), vm)

    col = jnp.arange(GRID_W)
    col_start = jnp.clip(col - NA_KW // 2, 0, GRID_W - NA_KW)
    key_local = jnp.arange(kh * GRID_W)
    key_row = key_local // GRID_W
    key_col = key_local % GRID_W
    col_ok = (key_col[None, :] >= col_start[:, None]) & (key_col[None, :] < col_start[:, None] + NA_KW)
    dc_idx = jnp.clip(key_col[None, :] - col[:, None], -(NA_KW - 1), NA_KW - 1) + (NA_KW - 1)

    def row_step(r):
        rs = jnp.clip(r - kh // 2, 0, rows - kh)
        q_row = lax.dynamic_slice_in_dim(qr, r * GRID_W, GRID_W, axis=1)
        k_win = lax.dynamic_slice_in_dim(kr, rs * GRID_W, kh * GRID_W, axis=1)
        v_win = lax.dynamic_slice_in_dim(vr, rs * GRID_W, kh * GRID_W, axis=1)
        dr_idx = rs + key_row - r + (NA_KH_MAX - 1)
        bias = rpb[:, dr_idx[None, :], dc_idx].astype(F32)
        s_win = jnp.einsum("bqhd,bkhd->bhqk", q_row, k_win, preferred_element_type=F32) * scale + bias
        s_win = jnp.where(col_ok, s_win, NEG_INF)
        s_met = jnp.einsum("bqhd,bkhd->bhqk", q_row, km, preferred_element_type=F32) * scale
        p = jax.nn.softmax(jnp.concatenate([s_met, s_win], axis=-1), axis=-1).astype(v.dtype)
        return (jnp.einsum("bhqk,bkhd->bqhd", p[..., :N_META], vm)
                + jnp.einsum("bhqk,bkhd->bqhd", p[..., N_META:], v_win))

    out_real = lax.map(row_step, jnp.arange(rows))
    out_real = out_real.transpose(1, 0, 2, 3, 4).reshape(B, n, H, d)
    return jnp.concatenate([out_meta, out_real], axis=1).reshape(B, L, H * d)


def sink_softmax(s, sink):
    col = jnp.broadcast_to(sink[None, :, :, None, None].astype(F32), s.shape[:-1] + (1,))
    return jax.nn.softmax(jnp.concatenate([s, col], axis=-1), axis=-1)[..., :-1]


def window_sink_attention(q, k, v, sinks):
    B, L = q.shape[0], q.shape[1]
    n = L - N_META
    nb = n // QBLK
    scale = SWA_DIM ** -0.5
    slopes = (2.0 ** (-8.0 * jnp.arange(1, SWA_HEADS + 1, dtype=F32) / SWA_HEADS)).reshape(SWA_KV_HEADS, SWA_GROUP)
    sink = sinks.reshape(SWA_KV_HEADS, SWA_GROUP)
    qm, km, vm = q[:, :N_META], k[:, :N_META], v[:, :N_META]
    qr, kr, vr = q[:, N_META:], k[:, N_META:], v[:, N_META:]

    s_mm = jnp.einsum("bqhgd,bshd->bhgqs", qm, km, preferred_element_type=F32) * scale
    s_mf = jnp.einsum("bqhgd,bshd->bhgqs", qm, kr[:, :QBLK], preferred_element_type=F32) * scale
    ok_m = (N_META + jnp.arange(QBLK)[None, :] - jnp.arange(N_META)[:, None]) <= SWA_WINDOW
    s_mf = jnp.where(ok_m, s_mf, NEG_INF)
    p_m = sink_softmax(jnp.concatenate([s_mm, s_mf], axis=-1), sink).astype(v.dtype)
    out_meta = (jnp.einsum("bhgqs,bshd->bqhgd", p_m[..., :N_META], vm)
                + jnp.einsum("bhgqs,bshd->bqhgd", p_m[..., N_META:], vr[:, :QBLK]))

    kp = jnp.pad(kr, ((0, 0), (QBLK, QBLK), (0, 0), (0, 0)))
    vp = jnp.pad(vr, ((0, 0), (QBLK, QBLK), (0, 0), (0, 0)))
    a = jnp.arange(QBLK)
    j = jnp.arange(3 * QBLK)
    dist = jnp.abs(j[None, :] - QBLK - a[:, None])
    alibi = -slopes[:, :, None, None] * dist.astype(F32)

    def blk(i):
        qb = lax.dynamic_slice_in_dim(qr, i * QBLK, QBLK, axis=1)
        kb = lax.dynamic_slice_in_dim(kp, i * QBLK, 3 * QBLK, axis=1)
        vb = lax.dynamic_slice_in_dim(vp, i * QBLK, 3 * QBLK, axis=1)
        key_real = i * QBLK - QBLK + j
        ok = (dist <= SWA_WINDOW) & ((key_real >= 0) & (key_real < n))[None, :]
        s_loc = jnp.einsum("bqhgd,bshd->bhgqs", qb, kb, preferred_element_type=F32) * scale + alibi
        s_loc = jnp.where(ok, s_loc, NEG_INF)
        s_met = jnp.einsum("bqhgd,bshd->bhgqs", qb, km, preferred_element_type=F32) * scale
        p = sink_softmax(jnp.concatenate([s_met, s_loc], axis=-1), sink).astype(v.dtype)
        return (jnp.einsum("bhgqs,bshd->bqhgd", p[..., :N_META], vm)
                + jnp.einsum("bhgqs,bshd->bqhgd", p[..., N_META:], vb))

    out_real = lax.map(blk, jnp.arange(nb))
    out_real = out_real.transpose(1, 0, 2, 3, 4, 5).reshape(B, n, MIX_C)
    return jnp.concatenate([out_meta.reshape(B, N_META, MIX_C), out_real], axis=1)


def expert_choice_ffn(x, w_router, w_gate, w_up, w_down):
    B, L, D = x.shape
    cap = (EC_CAPACITY_FACTOR * L) // N_EXPERTS
    logits = jnp.einsum("bld,de->ble", x, w_router, preferred_element_type=F32)
    aff = jax.nn.softmax(logits, axis=-1)
    gate, idx = lax.top_k(aff.transpose(0, 2, 1), cap)
    xs = jax.vmap(lambda xb, ib: xb[ib])(x, idx)
    h = jax.nn.silu(jnp.einsum("becd,edf->becf", xs, w_gate)) * jnp.einsum("becd,edf->becf", xs, w_up)
    y = jnp.einsum("becf,efd->becd", h, w_down) * gate[..., None].astype(x.dtype)
    return jax.vmap(lambda ib, yb: jnp.zeros((L, D), yb.dtype).at[ib.reshape(-1)].add(yb.reshape(-1, D)))(idx, y)


def even_mixer(u, w_in, q_norm, kv_norm, w_uq, w_ukv, rpb, w_out):
    B, L, _ = u.shape
    z = u @ w_in
    o1 = MLA_Q_LORA
    o2 = o1 + MLA_KV_LORA
    o3 = o2 + MLA_ROPE
    y_mla = mla_attention(z[..., :o1], z[..., o1:o2], z[..., o2:o3], q_norm, kv_norm, w_uq, w_ukv)
    na_qkv = z[..., o3:].reshape(B, L, 3, NA_HEADS, NA_DIM)
    y_na = neighbourhood_attention(na_qkv[:, :, 0], na_qkv[:, :, 1], na_qkv[:, :, 2], rpb)
    return jnp.concatenate([y_mla, y_na], axis=-1) @ w_out


def odd_mixer(u, w_in, sinks, w_out):
    B, L, _ = u.shape
    z = u @ w_in
    kvw = SWA_KV_HEADS * SWA_DIM
    q = z[..., :MIX_C].reshape(B, L, SWA_KV_HEADS, SWA_GROUP, SWA_DIM)
    k = z[..., MIX_C:MIX_C + kvw].reshape(B, L, SWA_KV_HEADS, SWA_DIM)
    v = z[..., MIX_C + kvw:].reshape(B, L, SWA_KV_HEADS, SWA_DIM)
    return window_sink_attention(q, k, v, sinks) @ w_out


def setup_inputs(seed: int = 0) -> dict:
    key = jax.random.key(seed)
    ks = jax.random.split(key, 20)
    n_even = (DEPTH + 1) // 2
    n_odd = DEPTH // 2
    D = D_MODEL

    def w(k, shape, fan_in):
        return jax.random.normal(k, shape, F32) * fan_in ** -0.5

    def gain(k, shape):
        return 1.0 + 0.05 * jax.random.normal(k, shape, F32)

    return {
        "x": jax.random.normal(ks[0], (BATCH, SEQ, D), F32),
        "meta_tokens": jax.random.normal(ks[1], (N_META, D), F32),
        "norm_mix": gain(ks[2], (DEPTH, D)),
        "norm_ffn": gain(ks[3], (DEPTH, D)),
        "norm_final": gain(ks[4], (D,)),
        "ab_w_in": w(ks[5], (n_even, D, IN_AB), D),
        "ab_q_norm": gain(ks[6], (n_even, MLA_Q_LORA)),
        "ab_kv_norm": gain(ks[7], (n_even, MLA_KV_LORA)),
        "ab_w_uq": w(ks[8], (n_even, MLA_Q_LORA, MLA_HEADS * (MLA_NOPE + MLA_ROPE)), MLA_Q_LORA),
        "ab_w_ukv": w(ks[9], (n_even, MLA_KV_LORA, MLA_HEADS * (MLA_NOPE + MLA_V)), MLA_KV_LORA),
        "ab_rpb": 0.1 * jax.random.normal(ks[10], (n_even, NA_HEADS, 2 * NA_KH_MAX - 1, 2 * NA_KW - 1), F32),
        "ab_w_out": w(ks[11], (n_even, MIX_AB, D), MIX_AB),
        "c_w_in": w(ks[12], (n_odd, D, IN_C), D),
        "c_sinks": 0.5 * jax.random.normal(ks[13], (n_odd, SWA_HEADS), F32),
        "c_w_out": w(ks[14], (n_odd, MIX_C, D), MIX_C),
        "ec_w_router": w(ks[15], (DEPTH, D, N_EXPERTS), D),
        "ec_w_gate": w(ks[16], (DEPTH, N_EXPERTS, D, D_EXPERT), D),
        "ec_w_up": w(ks[17], (DEPTH, N_EXPERTS, D, D_EXPERT), D),
        "ec_w_down": w(ks[18], (DEPTH, N_EXPERTS, D_EXPERT, D), D_EXPERT),
    }


def reference(x, meta_tokens, norm_mix, norm_ffn, norm_final,
              ab_w_in, ab_q_norm, ab_kv_norm, ab_w_uq, ab_w_ukv, ab_rpb, ab_w_out,
              c_w_in, c_sinks, c_w_out,
              ec_w_router, ec_w_gate, ec_w_up, ec_w_down):
    B = x.shape[0]
    meta = jnp.broadcast_to(meta_tokens[None].astype(x.dtype), (B, N_META, x.shape[-1]))
    h = jnp.concatenate([meta, x], axis=1)
    for layer in range(DEPTH):
        u = rmsnorm(h, norm_mix[layer])
        if layer % 2 == 0:
            e = layer // 2
            y = even_mixer(u, ab_w_in[e], ab_q_norm[e], ab_kv_norm[e], ab_w_uq[e], ab_w_ukv[e],
                           ab_rpb[e], ab_w_out[e])
        else:
            o = layer // 2
            y = odd_mixer(u, c_w_in[o], c_sinks[o], c_w_out[o])
        h = h + y
        h = h + expert_choice_ffn(rmsnorm(h, norm_ffn[layer]), ec_w_router[layer], ec_w_gate[layer],
                                  ec_w_up[layer], ec_w_down[layer])
    h = rmsnorm(h, norm_final)
    return h[:, N_META:]
```

```python
import functools
import math

import jax
import jax.numpy as jnp
from jax import lax
from jax.experimental import pallas as pl
from jax.experimental.pallas import tpu as pltpu

F32 = jnp.float32
BF16 = jnp.bfloat16
I32 = jnp.int32

N_META = 16
GRID_W = 64
QBLK = 128
EPS = 1e-6
NEG = -1e30

MLA_H = 8
MLA_NOPE = 128
MLA_ROPE = 64
MLA_V = 128
MLA_QL = 512
MLA_KVL = 512
MLA_HP = 256
ROPE_THETA = 10000.0

NA_H = 8
NA_D = 128
NA_KH = 8
NA_KW = 16

SWA_H = 32
SWA_KV = 4
SWA_G = SWA_H // SWA_KV
SWA_D = 64
SWA_WIN = 128

N_EXP = 16
CAP_F = 2

TAIL = 512
LANE = 128
ZCHUNK = 256
TOKB = 256


def _cp(sem, vmem_mb=48):
    return pltpu.CompilerParams(dimension_semantics=sem, vmem_limit_bytes=vmem_mb << 20)


def _dot(a, b):
    return jnp.dot(a, b, preferred_element_type=F32)


def _dot_nt(a, b):
    return lax.dot_general(a, b, (((1,), (1,)), ((), ())), preferred_element_type=F32)


def _rms(x, g):
    return x * lax.rsqrt(jnp.mean(x * x, axis=-1, keepdims=True) + EPS) * g


def _norm_body(x_ref, g_ref, o_ref):
    o_ref[0] = _rms(x_ref[0], g_ref[...]).astype(o_ref.dtype)


def _norm(h, g, tm):
    B, Lp, D = h.shape
    return pl.pallas_call(
        _norm_body,
        out_shape=jax.ShapeDtypeStruct((B, Lp, D), BF16),
        grid=(B, Lp // tm),
        in_specs=[pl.BlockSpec((1, tm, D), lambda b, i: (b, i, 0)), pl.BlockSpec((1, D), lambda b, i: (0, 0))],
        out_specs=pl.BlockSpec((1, tm, D), lambda b, i: (b, i, 0)),
        compiler_params=_cp(("parallel", "parallel")),
    )(h, g.reshape(1, D))


def _mm_body(x_ref, w_ref, o_ref):
    o_ref[0] = _dot(x_ref[0], w_ref[...]).astype(o_ref.dtype)


def _mm(x, w, tm, tn):
    B, Lp, K = x.shape
    N = w.shape[1]
    return pl.pallas_call(
        _mm_body,
        out_shape=jax.ShapeDtypeStruct((B, Lp, N), BF16),
        grid=(B, Lp // tm, N // tn),
        in_specs=[pl.BlockSpec((1, tm, K), lambda b, i, j: (b, i, 0)), pl.BlockSpec((K, tn), lambda b, i, j: (0, j))],
        out_specs=pl.BlockSpec((1, tm, tn), lambda b, i, j: (b, i, j)),
        compiler_params=_cp(("parallel", "parallel", "parallel")),
    )(x, w)


def _rope(pe, c, sn, sp):
    return pe * c + pltpu.roll(pe, 96, 1) * sn + pltpu.roll(pe, 32, 1) * sp


def _uq_body(lat_ref, g_ref, w_ref, c_ref, sn_ref, sp_ref, o_ref, *, scale):
    xn = _rms(lat_ref[0].astype(F32), g_ref[...]).astype(BF16)
    z = _dot(xn, w_ref[...])
    c, sn, sp = c_ref[...], sn_ref[...], sp_ref[...]
    for h in range(MLA_H):
        o = h * MLA_HP
        o_ref[0, :, o:o + LANE] = (z[:, o:o + LANE] * scale).astype(o_ref.dtype)
        o_ref[0, :, o + LANE:o + MLA_HP] = (_rope(z[:, o + LANE:o + MLA_HP], c, sn, sp) * scale).astype(o_ref.dtype)


def _ukv_body(lat_ref, g_ref, w_ref, kr_ref, c_ref, sn_ref, sp_ref, k_ref, v_ref):
    xn = _rms(lat_ref[0].astype(F32), g_ref[...]).astype(BF16)
    z = _dot(xn, w_ref[...])
    kr = _rope(kr_ref[0].astype(F32), c_ref[...], sn_ref[...], sp_ref[...]).astype(k_ref.dtype)
    nk = MLA_H * MLA_NOPE
    for h in range(MLA_H):
        k_ref[0, :, h * MLA_HP:h * MLA_HP + LANE] = z[:, h * LANE:(h + 1) * LANE].astype(k_ref.dtype)
        k_ref[0, :, h * MLA_HP + LANE:(h + 1) * MLA_HP] = kr
    v_ref[0] = z[:, nk:].astype(v_ref.dtype)


def _mla_up(lat, krope, qn, kvn, wq, wkv, tabs, tm):
    B, Lp, _ = lat.shape
    c, sn, sp = tabs
    tab_spec = pl.BlockSpec((tm, LANE), lambda b, i: (i, 0))
    scale = (MLA_NOPE + MLA_ROPE) ** -0.5
    q = pl.pallas_call(
        functools.partial(_uq_body, scale=scale),
        out_shape=jax.ShapeDtypeStruct((B, Lp, MLA_H * MLA_HP), BF16),
        grid=(B, Lp // tm),
        in_specs=[pl.BlockSpec((1, tm, MLA_QL), lambda b, i: (b, i, 0)),
                  pl.BlockSpec((1, MLA_QL), lambda b, i: (0, 0)),
                  pl.BlockSpec(wq.shape, lambda b, i: (0, 0)),
                  tab_spec, tab_spec, tab_spec],
        out_specs=pl.BlockSpec((1, tm, MLA_H * MLA_HP), lambda b, i: (b, i, 0)),
        compiler_params=_cp(("parallel", "parallel")),
    )(lat, qn.reshape(1, -1), wq, c, sn, sp)
    k, v = pl.pallas_call(
        _ukv_body,
        out_shape=(jax.ShapeDtypeStruct((B, Lp, MLA_H * MLA_HP), BF16),
                   jax.ShapeDtypeStruct((B, Lp, MLA_H * MLA_V), BF16)),
        grid=(B, Lp // tm),
        in_specs=[pl.BlockSpec((1, tm, MLA_KVL), lambda b, i: (b, i, 1)),
                  pl.BlockSpec((1, MLA_KVL), lambda b, i: (0, 0)),
                  pl.BlockSpec(wkv.shape, lambda b, i: (0, 0)),
                  pl.BlockSpec((1, tm, LANE), lambda b, i: (b, i, 0)),
                  tab_spec, tab_spec, tab_spec],
        out_specs=(pl.BlockSpec((1, tm, MLA_H * MLA_HP), lambda b, i: (b, i, 0)),
                   pl.BlockSpec((1, tm, MLA_H * MLA_V), lambda b, i: (b, i, 0))),
        compiler_params=_cp(("parallel", "parallel")),
    )(lat, kvn.reshape(1, -1), wkv, krope, c, sn, sp)
    return q, k, v


def _mla_body(q_ref, k_ref, v_ref, o_ref, m_sc, l_sc, acc_sc, *, n_real, tkc):
    q = q_ref[0]
    tq = q.shape[0]
    m_sc[...] = jnp.full(m_sc.shape, NEG, F32)
    l_sc[...] = jnp.zeros(l_sc.shape, F32)
    acc_sc[...] = jnp.zeros(acc_sc.shape, F32)

    def step(kc, vc):
        s = _dot_nt(q, kc)
        m_prev = m_sc[...]
        m_new = jnp.maximum(m_prev, jnp.max(s, axis=1, keepdims=True))
        p = jnp.exp(s - m_new)
        alpha = jnp.exp(m_prev - m_new)
        l_sc[...] = alpha * l_sc[...] + jnp.sum(p, axis=1, keepdims=True)
        acc_sc[...] = alpha * acc_sc[...] + _dot(p.astype(BF16), vc)
        m_sc[...] = m_new

    def body(c, carry):
        r0 = pl.multiple_of(c * tkc, tkc)
        step(k_ref[0, pl.ds(r0, tkc), :], v_ref[0, pl.ds(r0, tkc), :])
        return carry

    lax.fori_loop(0, n_real // tkc, body, 0)
    step(k_ref[0, n_real:n_real + N_META, :], v_ref[0, n_real:n_real + N_META, :])
    out = acc_sc[...] / l_sc[...]
    i = pl.program_id(2)
    row = i * tq + lax.broadcasted_iota(I32, (tq, 1), 0)
    o_ref[0] = jnp.where(row < n_real + N_META, out, 0.0).astype(o_ref.dtype)


def _mla_attention(q, k, v, n_real, tq, tkc):
    B, Lp, _ = q.shape
    return pl.pallas_call(
        functools.partial(_mla_body, n_real=n_real, tkc=tkc),
        out_shape=jax.ShapeDtypeStruct((B, Lp, MLA_H * MLA_V), BF16),
        grid=(B, MLA_H, Lp // tq),
        in_specs=[pl.BlockSpec((1, tq, MLA_HP), lambda b, h, i: (b, i, h)),
                  pl.BlockSpec((1, Lp, MLA_HP), lambda b, h, i: (b, 0, h)),
                  pl.BlockSpec((1, Lp, MLA_V), lambda b, h, i: (b, 0, h))],
        out_specs=pl.BlockSpec((1, tq, MLA_V), lambda b, h, i: (b, i, h)),
        scratch_shapes=[pltpu.VMEM((tq, 1), F32), pltpu.VMEM((tq, 1), F32), pltpu.VMEM((tq, MLA_V), F32)],
        compiler_params=_cp(("parallel", "parallel", "arbitrary")),
    )(q, k, v)


def _na_bias_table(rpb):
    col = jnp.arange(GRID_W)
    cs = jnp.clip(col - NA_KW // 2, 0, GRID_W - NA_KW)
    ok = (col[None, :] >= cs[:, None]) & (col[None, :] < cs[:, None] + NA_KW)
    dc = jnp.clip(col[None, :] - col[:, None], -(NA_KW - 1), NA_KW - 1) + (NA_KW - 1)
    d = jnp.arange(NA_KH)[:, None] + jnp.arange(NA_KH)[None, :]
    t = rpb[:, d][:, :, :, dc]
    t = jnp.where(ok[None, None, None], t, NEG)
    return t.transpose(0, 1, 3, 2, 4).reshape(rpb.shape[0], NA_KH, GRID_W, NA_KH * GRID_W).astype(F32)


def _na_body(q_ref, k_ref, v_ref, t_ref, o_ref, *, n_real, rpg):
    i = pl.program_id(2)
    rows = n_real // GRID_W
    scale = NA_D ** -0.5
    km = k_ref[0, n_real:n_real + N_META, :]
    vm = v_ref[0, n_real:n_real + N_META, :]

    @pl.when(i < rows // rpg)
    def _():
        for rr in range(rpg):
            r = i * rpg + rr
            rs = jnp.clip(r - NA_KH // 2, 0, rows - NA_KH)
            d0 = rs - r + (NA_KH - 1)
            k0 = pl.multiple_of(rs * GRID_W, GRID_W)
            qr = q_ref[0, rr * GRID_W:(rr + 1) * GRID_W, :]
            kw = k_ref[0, pl.ds(k0, NA_KH * GRID_W), :]
            vw = v_ref[0, pl.ds(k0, NA_KH * GRID_W), :]
            s = _dot_nt(qr, kw) * scale + t_ref[0, d0]
            sm = _dot_nt(qr, km) * scale
            m = jnp.maximum(jnp.max(s, axis=1, keepdims=True), jnp.max(sm, axis=1, keepdims=True))
            p = jnp.exp(s - m)
            pm = jnp.exp(sm - m)
            l = jnp.sum(p, axis=1, keepdims=True) + jnp.sum(pm, axis=1, keepdims=True)
            o = (_dot(p.astype(BF16), vw) + _dot(pm.astype(BF16), vm)) / l
            o_ref[0, rr * GRID_W:(rr + 1) * GRID_W, :] = o.astype(o_ref.dtype)

    @pl.when(i >= rows // rpg)
    def _():
        qm = q_ref[0, 0:N_META, :]
        sm = _dot_nt(qm, km) * scale
        pm = jnp.exp(sm - jnp.max(sm, axis=1, keepdims=True))
        o = _dot(pm.astype(BF16), vm) / jnp.sum(pm, axis=1, keepdims=True)
        o_ref[0] = jnp.zeros(o_ref.shape[1:], o_ref.dtype)
        o_ref[0, 0:N_META, :] = o.astype(o_ref.dtype)


def _na_attention(qkv, table, n_real):
    B, Lp, _ = qkv.shape
    rpg = TAIL // GRID_W
    tq = rpg * GRID_W
    return pl.pallas_call(
        functools.partial(_na_body, n_real=n_real, rpg=rpg),
        out_shape=jax.ShapeDtypeStruct((B, Lp, NA_H * NA_D), BF16),
        grid=(B, NA_H, Lp // tq),
        in_specs=[pl.BlockSpec((1, tq, NA_D), lambda b, h, i: (b, i, h)),
                  pl.BlockSpec((1, Lp, NA_D), lambda b, h, i: (b, 0, NA_H + h)),
                  pl.BlockSpec((1, Lp, NA_D), lambda b, h, i: (b, 0, 2 * NA_H + h)),
                  pl.BlockSpec((1,) + table.shape[1:], lambda b, h, i: (h, 0, 0, 0))],
        out_specs=pl.BlockSpec((1, tq, NA_D), lambda b, h, i: (b, i, h)),
        compiler_params=_cp(("parallel", "parallel", "arbitrary")),
    )(qkv, qkv, qkv, table)


def _swa_body(sink_ref, q_ref, kp_ref, kc_ref, kn_ref, km_ref, vp_ref, vc_ref, vn_ref, vm_ref, o_ref, *, n_real):
    i = pl.program_id(1)
    nb = n_real // QBLK
    scale = SWA_D ** -0.5
    kmeta = km_ref[0, 0:N_META, :]
    vmeta = vm_ref[0, 0:N_META, :]

    def attend(qh, kh, vh, kmh, vmh, bias, ok, sink):
        s = _dot_nt(qh, kh) * scale + bias
        s = jnp.where(ok, s, NEG)
        sm = _dot_nt(qh, kmh) * scale
        m = jnp.maximum(jnp.maximum(jnp.max(s, axis=1, keepdims=True), jnp.max(sm, axis=1, keepdims=True)), sink)
        p = jnp.exp(s - m)
        pm = jnp.exp(sm - m)
        l = jnp.sum(p, axis=1, keepdims=True) + jnp.sum(pm, axis=1, keepdims=True) + jnp.exp(sink - m)
        return (_dot(p.astype(BF16), vh) + _dot(pm.astype(BF16), vmh)) / l

    def heads(nq, kall, vall, dist, ok, use_alibi):
        outs = []
        for h in range(SWA_H):
            kv = h // SWA_G
            qh = q_ref[0, 0:nq, h * SWA_D:(h + 1) * SWA_D]
            sl = slice(kv * SWA_D, (kv + 1) * SWA_D)
            bias = (-(2.0 ** (-8.0 * (h + 1) / SWA_H))) * dist if use_alibi else 0.0
            outs.append(attend(qh, kall[:, sl], vall[:, sl], kmeta[:, sl], vmeta[:, sl], bias, ok, sink_ref[h]))
        return jnp.concatenate(outs, axis=1)

    @pl.when(i < nb)
    def _():
        a = lax.broadcasted_iota(I32, (QBLK, 3 * QBLK), 0)
        j = lax.broadcasted_iota(I32, (QBLK, 3 * QBLK), 1)
        dist = jnp.abs(j - QBLK - a)
        key_real = i * QBLK - QBLK + j
        ok = (dist <= SWA_WIN) & (key_real >= 0) & (key_real < n_real)
        kall = jnp.concatenate([kp_ref[0], kc_ref[0], kn_ref[0]], axis=0)
        vall = jnp.concatenate([vp_ref[0], vc_ref[0], vn_ref[0]], axis=0)
        o_ref[0] = heads(QBLK, kall, vall, dist.astype(F32), ok, True).astype(o_ref.dtype)

    @pl.when(i == nb)
    def _():
        a = lax.broadcasted_iota(I32, (N_META, QBLK), 0)
        j = lax.broadcasted_iota(I32, (N_META, QBLK), 1)
        ok = (N_META + j - a) <= SWA_WIN
        o = heads(N_META, kn_ref[0], vn_ref[0], None, ok, False)
        o_ref[0] = jnp.zeros(o_ref.shape[1:], o_ref.dtype)
        o_ref[0, 0:N_META, :] = o.astype(o_ref.dtype)

    @pl.when(i > nb)
    def _():
        o_ref[0] = jnp.zeros(o_ref.shape[1:], o_ref.dtype)


def _swa_attention(z, sinks, n_real):
    B, Lp, _ = z.shape
    nb = n_real // QBLK
    qw = SWA_H * SWA_D
    kw = SWA_KV * SWA_D
    kblk = qw // kw
    prev = lambda b, i: (b, jnp.clip(i - 1, 0, nb - 1), kblk)
    cur = lambda b, i: (b, jnp.minimum(i, nb - 1), kblk)
    nxt = lambda b, i: (b, jnp.where(i >= nb, 0, jnp.minimum(i + 1, nb - 1)), kblk)
    met = lambda b, i: (b, nb, kblk)
    shift = lambda f: (lambda b, i: f(b, i)[:2] + (kblk + 1,))
    kspec = lambda f: pl.BlockSpec((1, QBLK, kw), f)
    return pl.pallas_call(
        functools.partial(_swa_body, n_real=n_real),
        out_shape=jax.ShapeDtypeStruct((B, Lp, qw), BF16),
        grid=(B, Lp // QBLK),
        in_specs=[pl.BlockSpec(memory_space=pltpu.SMEM),
                  pl.BlockSpec((1, QBLK, qw), lambda b, i: (b, i, 0)),
                  kspec(prev), kspec(cur), kspec(nxt), kspec(met),
                  kspec(shift(prev)), kspec(shift(cur)), kspec(shift(nxt)), kspec(shift(met))],
        out_specs=pl.BlockSpec((1, QBLK, qw), lambda b, i: (b, i, 0)),
        compiler_params=_cp(("parallel", "arbitrary")),
    )(sinks, z, z, z, z, z, z, z, z, z)


def _outproj_body(*refs, n_y, n_valid):
    h_ref = refs[0]
    y_refs = refs[1:1 + n_y]
    wo_ref, g_ref, wr_ref, h1_ref, xe_ref, aff_ref = refs[1 + n_y:]
    D = h_ref.shape[2]
    acc = h_ref[0]
    k0 = 0
    for y_ref in y_refs:
        kk = y_ref.shape[2]
        acc = acc + _dot(y_ref[0], wo_ref[k0:k0 + kk, :])
        k0 += kk
    h1_ref[0] = acc
    xn = _rms(acc, g_ref[...])
    logits = jnp.dot(xn, wr_ref[...], preferred_element_type=F32, precision=lax.Precision.HIGHEST)
    lane = lax.broadcasted_iota(I32, logits.shape, 1)
    logits = jnp.where(lane < N_EXP, logits, NEG)
    e = jnp.exp(logits - jnp.max(logits, axis=1, keepdims=True))
    aff = e / jnp.sum(e, axis=1, keepdims=True)
    xe_ref[0, :, 0:D] = xn
    xe_ref[0, :, D:D + LANE] = aff
    aff_ref[0] = aff.T[0:N_EXP, :]


def _outproj(h, ys, wo, g, wr, tm):
    B, Lp, D = h.shape
    row = lambda b, i: (b, i, 0)
    full = lambda b, i: (0, 0)
    return pl.pallas_call(
        functools.partial(_outproj_body, n_y=len(ys), n_valid=0),
        out_shape=(jax.ShapeDtypeStruct((B, Lp, D), F32),
                   jax.ShapeDtypeStruct((B, Lp, D + LANE), F32),
                   jax.ShapeDtypeStruct((B, N_EXP, Lp), F32)),
        grid=(B, Lp // tm),
        in_specs=[pl.BlockSpec((1, tm, D), row)] + [pl.BlockSpec((1, tm, y.shape[2]), row) for y in ys]
        + [pl.BlockSpec(wo.shape, full), pl.BlockSpec((1, D), full), pl.BlockSpec(wr.shape, full)],
        out_specs=(pl.BlockSpec((1, tm, D), row), pl.BlockSpec((1, tm, D + LANE), row),
                   pl.BlockSpec((1, N_EXP, tm), lambda b, i: (b, 0, i))),
        compiler_params=_cp(("parallel", "parallel")),
    )(h, *ys, wo, g.reshape(1, D), wr)


def _topk_body(aff_ref, sel_ref, q_ref, lohi_ref, sel_sc, *, n_real, cap):
    Lp = aff_ref.shape[2]
    nch = Lp // LANE
    cm = n_real // LANE
    aff = aff_ref[0]
    tok = lax.broadcasted_iota(I32, aff.shape, 1)
    keys = jnp.where(tok < n_real + N_META, pltpu.bitcast(aff, I32), -1)

    def search(it, prefix):
        cand = prefix | lax.shift_left(jnp.int32(1), 30 - it)
        cnt = jnp.sum(jnp.where(keys >= cand, 1.0, 0.0), axis=1, keepdims=True)
        return jnp.where(cnt >= cap, cand, prefix)

    thr = lax.fori_loop(0, 31, search, jnp.zeros((N_EXP, 1), I32))
    need = cap - jnp.sum(jnp.where(keys > thr, 1.0, 0.0), axis=1, keepdims=True)

    ri = lax.broadcasted_iota(I32, (LANE, LANE), 0)
    ci = lax.broadcasted_iota(I32, (LANE, LANE), 1)
    upper = (ri <= ci).astype(BF16)
    er = lax.broadcasted_iota(I32, (N_EXP, N_EXP), 0)
    ec = lax.broadcasted_iota(I32, (N_EXP, N_EXP), 1)
    lower = (ec < er).astype(BF16)

    sel_sc[...] = jnp.zeros(sel_sc.shape, F32)

    def select(c, carry):
        c0 = pl.multiple_of(c * LANE, LANE)
        a = pltpu.bitcast(aff_ref[0, :, pl.ds(c0, LANE)], I32)
        t = c0 + lax.broadcasted_iota(I32, a.shape, 1)
        kc = jnp.where(t < n_real + N_META, a, -1)
        eq = (kc == thr).astype(F32)
        rank = _dot(eq.astype(BF16), upper) - eq + carry
        sel = jnp.where((kc > thr) | ((eq > 0) & (rank < need)), 1.0, 0.0)
        sel_sc[:, pl.ds(c0, LANE)] = sel
        return carry + jnp.sum(eq, axis=1, keepdims=True)

    carry = select(jnp.int32(cm), jnp.zeros((N_EXP, 1), F32))
    lax.fori_loop(0, cm, select, carry)

    def offsets(c, carry):
        off_c = carry
        c0 = pl.multiple_of(c * LANE, LANE)
        sel = sel_sc[:, pl.ds(c0, LANE)]
        selb = sel.astype(BF16)
        nt = jnp.sum(sel, axis=0, keepdims=True)
        off_incl = _dot(jnp.broadcast_to(nt, (8, LANE)).astype(BF16), upper)[0:1] + off_c
        r = _dot(lower, selb)
        sel_ref[0, :, pl.ds(c0, LANE)] = sel.astype(I32)
        q_ref[0, :, pl.ds(c0, LANE)] = (off_incl - nt + r).astype(I32)
        lohi_ref[0, 0:1, pl.ds(c0, LANE)] = (off_incl - nt).astype(I32)
        lohi_ref[0, 1:2, pl.ds(c0, LANE)] = off_incl.astype(I32)
        return off_c + jnp.sum(nt, axis=1, keepdims=True)

    lax.fori_loop(0, nch, offsets, jnp.zeros((1, 1), F32))


def _topk(aff_t, n_real, cap):
    B, E, Lp = aff_t.shape
    spec = pl.BlockSpec((1, E, Lp), lambda b: (b, 0, 0))
    return pl.pallas_call(
        functools.partial(_topk_body, n_real=n_real, cap=cap),
        out_shape=(jax.ShapeDtypeStruct((B, E, Lp), I32), jax.ShapeDtypeStruct((B, E, Lp), I32),
                   jax.ShapeDtypeStruct((B, 2, Lp), I32)),
        grid=(B,),
        in_specs=[spec],
        out_specs=(spec, spec, pl.BlockSpec((1, 2, Lp), lambda b: (b, 0, 0))),
        scratch_shapes=[pltpu.VMEM((E, Lp), F32)],
        compiler_params=_cp(("parallel",)),
    )(aff_t)


def _compact_body(sel_ref, idx_ref, *, n_tok, cap):
    def put(t, cnt):
        idx_ref[0, 0, 0, cnt] = t
        return cnt + sel_ref[0, 0, 0, t]

    lax.fori_loop(0, n_tok, put, jnp.int32(0), unroll=8)

    def fill(s, c):
        idx_ref[0, 0, 0, s] = 0
        return c

    lax.fori_loop(cap, idx_ref.shape[3], fill, 0)


def _compact(sel, n_tok, cap, cp):
    B, E, Lp = sel.shape
    return pl.pallas_call(
        functools.partial(_compact_body, n_tok=n_tok, cap=cap),
        out_shape=jax.ShapeDtypeStruct((B, E, 1, cp), I32),
        grid=(B, E),
        in_specs=[pl.BlockSpec((1, 1, 1, Lp), lambda b, e: (b, e, 0, 0), memory_space=pltpu.SMEM)],
        out_specs=pl.BlockSpec((1, 1, 1, cp), lambda b, e: (b, e, 0, 0), memory_space=pltpu.SMEM),
        compiler_params=_cp(("parallel", "parallel")),
    )(sel.reshape(B, E, 1, Lp))


def _gather_body(idx_ref, src_ref, dst_ref, sem, *, n_rows):
    b = pl.program_id(0)
    e = pl.program_id(1)

    def issue(s, c):
        t = idx_ref[0, 0, 0, s]
        pltpu.make_async_copy(src_ref.at[b, pl.ds(t, 1), :], dst_ref.at[b, e, pl.ds(s, 1), :], sem).start()
        return c

    lax.fori_loop(0, n_rows, issue, 0)
    pltpu.make_async_copy(dst_ref.at[b, e], dst_ref.at[b, e], sem).wait()


def _gather(idx, xe, cr):
    B, E, _, cp = idx.shape
    De = xe.shape[2]
    return pl.pallas_call(
        functools.partial(_gather_body, n_rows=cr),
        out_shape=jax.ShapeDtypeStruct((B, E, cr, De), xe.dtype),
        grid=(B, E),
        in_specs=[pl.BlockSpec((1, 1, 1, cp), lambda b, e: (b, e, 0, 0), memory_space=pltpu.SMEM),
                  pl.BlockSpec(memory_space=pl.ANY)],
        out_specs=pl.BlockSpec(memory_space=pl.ANY),
        scratch_shapes=[pltpu.SemaphoreType.DMA(())],
        compiler_params=_cp(("arbitrary", "arbitrary")),
    )(idx, xe)


def _ffn_body(x_ref, wg_ref, wu_ref, wd_ref, y_ref, xb_sc):
    e = pl.program_id(0)
    f = pl.program_id(3)
    D = xb_sc.shape[1]

    @pl.when(f == 0)
    def _():
        xb_sc[...] = x_ref[0, 0, :, 0:D].astype(BF16)
        y_ref[0, 0] = jnp.zeros(y_ref.shape[2:], F32)

    x = xb_sc[...]
    g = _dot(x, wg_ref[0])
    u = _dot(x, wu_ref[0])
    hmid = (g * jax.nn.sigmoid(g) * u).astype(BF16)
    y_ref[0, 0] += _dot(hmid, wd_ref[0])

    @pl.when(f == pl.num_programs(3) - 1)
    def _():
        aff = x_ref[0, 0, :, D:D + LANE]
        lane = lax.broadcasted_iota(I32, aff.shape, 1)
        gate = jnp.sum(jnp.where(lane == e, aff, 0.0), axis=1, keepdims=True)
        y_ref[0, 0] = y_ref[0, 0] * gate


def _ffn(xs, wg, wu, wd, tr, tf):
    B, E, cr, De = xs.shape
    D, F = wg.shape[1], wg.shape[2]
    return pl.pallas_call(
        _ffn_body,
        out_shape=jax.ShapeDtypeStruct((B, E, cr, D), F32),
        grid=(E, B, cr // tr, F // tf),
        in_specs=[pl.BlockSpec((1, 1, tr, De), lambda e, b, r, f: (b, e, r, 0)),
                  pl.BlockSpec((1, D, tf), lambda e, b, r, f: (e, 0, f)),
                  pl.BlockSpec((1, D, tf), lambda e, b, r, f: (e, 0, f)),
                  pl.BlockSpec((1, tf, D), lambda e, b, r, f: (e, f, 0))],
        out_specs=pl.BlockSpec((1, 1, tr, D), lambda e, b, r, f: (b, e, r, 0)),
        scratch_shapes=[pltpu.VMEM((tr, D), BF16)],
        compiler_params=_cp(("parallel", "parallel", "parallel", "arbitrary")),
    )(xs, wg, wu, wd)


def _scatter_body(idx_ref, q_ref, y_ref, z_ref, zero_sc, sem, zsem, *, cap, n_fill):
    b = pl.program_id(0)
    e = pl.program_id(1)
    total = N_EXP * cap

    @pl.when(e == 0)
    def _():
        zero_sc[...] = jnp.zeros(zero_sc.shape, zero_sc.dtype)
        cp = pltpu.make_async_copy(zero_sc, z_ref.at[b, total:total + n_fill, :], zsem)
        cp.start()
        cp.wait()

    def issue(s, c):
        dst = q_ref[0, 0, 0, idx_ref[0, 0, 0, s]]
        pltpu.make_async_copy(y_ref.at[b, e, pl.ds(s, 1), :], z_ref.at[b, pl.ds(dst, 1), :], sem).start()
        return c

    lax.fori_loop(0, cap, issue, 0)
    n8 = cap // 8 * 8
    pltpu.make_async_copy(y_ref.at[b, e, 0:n8, :], y_ref.at[b, e, 0:n8, :], sem).wait()
    for _ in range(cap - n8):
        pltpu.make_async_copy(y_ref.at[b, e, 0:1, :], y_ref.at[b, e, 0:1, :], sem).wait()


def _scatter(idx, q, y, cap, zr):
    B, E, _, cp = idx.shape
    Lp = q.shape[2]
    D = y.shape[3]
    n_fill = zr - E * cap
    smem = lambda n: pl.BlockSpec((1, 1, 1, n), lambda b, e: (b, e, 0, 0), memory_space=pltpu.SMEM)
    return pl.pallas_call(
        functools.partial(_scatter_body, cap=cap, n_fill=n_fill),
        out_shape=jax.ShapeDtypeStruct((B, zr, D), y.dtype),
        grid=(B, E),
        in_specs=[smem(cp), smem(Lp), pl.BlockSpec(memory_space=pl.ANY)],
        out_specs=pl.BlockSpec(memory_space=pl.ANY),
        scratch_shapes=[pltpu.VMEM((n_fill, D), y.dtype), pltpu.SemaphoreType.DMA(()), pltpu.SemaphoreType.DMA(())],
        compiler_params=_cp(("arbitrary", "arbitrary")),
    )(idx, q.reshape(B, E, 1, Lp), y)


def _combine_body(k0_ref, k1_ref, h_ref, lohi_ref, g_ref, z_ref, h2_ref, u_ref, buf, acc_sc, sem, *, nblk):
    b = pl.program_id(0)
    j = pl.program_id(1)
    k0 = k0_ref[b * nblk + j]
    k1 = k1_ref[b * nblk + j]
    lo = lohi_ref[0, 0:1, :]
    hi = lohi_ref[0, 1:2, :]
    acc_sc[...] = jnp.zeros(acc_sc.shape, F32)

    def copy(k, slot):
        r0 = pl.multiple_of(k * ZCHUNK, ZCHUNK)
        return pltpu.make_async_copy(z_ref.at[b, pl.ds(r0, ZCHUNK), :], buf.at[slot], sem.at[slot])

    @pl.when(k1 > k0)
    def _():
        copy(k0, 0).start()

    def chunk(k, c):
        slot = lax.rem(k - k0, 2)
        copy(k, slot).wait()

        @pl.when(k + 1 < k1)
        def _():
            copy(k + 1, 1 - slot).start()

        w = k * ZCHUNK + lax.broadcasted_iota(I32, (ZCHUNK, lo.shape[1]), 0)
        band_t = jnp.where((w >= lo) & (w < hi), 1.0, 0.0)
        acc_sc[...] += _dot(band_t.T.astype(BF16), buf[slot].astype(BF16))
        return c

    lax.fori_loop(k0, k1, chunk, 0)
    h2 = h_ref[0] + acc_sc[...]
    h2_ref[0] = h2
    u_ref[0] = _rms(h2, g_ref[...]).astype(u_ref.dtype)


def _combine(h1, lohi, z, g, n_rows, u_dtype):
    B, Lp, D = h1.shape
    nblk = n_rows // TOKB
    lo = lohi[:, 0, :n_rows].reshape(B, nblk, TOKB)
    hi = lohi[:, 1, :n_rows].reshape(B, nblk, TOKB)
    k0 = (lo[:, :, 0] // ZCHUNK).reshape(-1).astype(I32)
    k1 = ((hi[:, :, -1] + ZCHUNK - 1) // ZCHUNK).reshape(-1).astype(I32)
    k1 = jnp.where(hi[:, :, -1].reshape(-1) > lo[:, :, 0].reshape(-1), k1, k0)
    row = lambda b, j, *_: (b, j, 0)
    grid_spec = pltpu.PrefetchScalarGridSpec(
        num_scalar_prefetch=2,
        grid=(B, nblk),
        in_specs=[pl.BlockSpec((1, TOKB, D), row),
                  pl.BlockSpec((1, 2, TOKB), lambda b, j, *_: (b, 0, j)),
                  pl.BlockSpec((1, D), lambda b, j, *_: (0, 0)),
                  pl.BlockSpec(memory_space=pl.ANY)],
        out_specs=(pl.BlockSpec((1, TOKB, D), row), pl.BlockSpec((1, TOKB, D), row)),
        scratch_shapes=[pltpu.VMEM((2, ZCHUNK, D), z.dtype), pltpu.VMEM((TOKB, D), F32),
                        pltpu.SemaphoreType.DMA((2,))],
    )
    return pl.pallas_call(
        functools.partial(_combine_body, nblk=nblk),
        out_shape=(jax.ShapeDtypeStruct((B, n_rows, D), F32), jax.ShapeDtypeStruct((B, n_rows, D), u_dtype)),
        grid_spec=grid_spec,
        compiler_params=_cp(("arbitrary", "arbitrary")),
    )(k0, k1, h1, lohi, g.reshape(1, D), z)


def _moe(h1, xe, aff_t, wg, wu, wd, g_next, n_real, n_rows_out, u_dtype):
    B, Lp, D = h1.shape
    cap = (CAP_F * (n_real + N_META)) // N_EXP
    cr = -(-cap // 48) * 48
    cp = -(-(cap + 1) // LANE) * LANE
    zr = -(-(N_EXP * cap) // ZCHUNK) * ZCHUNK
    if zr == N_EXP * cap:
        zr += ZCHUNK
    sel, q, lohi = _topk(aff_t, n_real, cap)
    idx = _compact(sel, n_real + N_META, cap, cp)
    xs = _gather(idx, xe, cr)
    F = wg.shape[2]
    y = _ffn(xs, wg, wu, wd, cr // 3, min(F, 512))
    z = _scatter(idx, q, y, cap, zr)
    return _combine(h1, lohi, z, g_next, n_rows_out, u_dtype)


def _rope_tables(n_real, Lp):
    pos = jnp.concatenate([jnp.arange(n_real) + N_META, jnp.arange(N_META), jnp.zeros((Lp - n_real - N_META,), I32)])
    inv_freq = 1.0 / (ROPE_THETA ** (jnp.arange(0, MLA_ROPE, 2, dtype=F32) / MLA_ROPE))
    ang = pos.astype(F32)[:, None] * inv_freq[None, :]
    cos, sin = jnp.cos(ang), jnp.sin(ang)
    half = MLA_ROPE // 2
    z = lambda n: jnp.zeros((Lp, n), F32)
    c = jnp.concatenate([cos, cos, z(LANE - 2 * half)], axis=1)
    sn = jnp.concatenate([-sin, z(LANE - half)], axis=1)
    sp = jnp.concatenate([z(half), sin, z(LANE - 2 * half)], axis=1)
    return c, sn, sp


def kernel(x, meta_tokens, norm_mix, norm_ffn, norm_final, ab_w_in, ab_q_norm, ab_kv_norm, ab_w_uq, ab_w_ukv, ab_rpb, ab_w_out, c_w_in, c_sinks, c_w_out, ec_w_router, ec_w_gate, ec_w_up, ec_w_down):
    B, S, D = x.shape
    depth = norm_mix.shape[0]
    Lp = S + TAIL
    tm = 512
    meta = jnp.broadcast_to(meta_tokens[None].astype(x.dtype), (B, N_META, D))
    h = jnp.concatenate([x, meta, jnp.zeros((B, TAIL - N_META, D), x.dtype)], axis=1)
    u = _norm(h, norm_mix[0], tm)
    tabs = _rope_tables(S, Lp)

    for layer in range(depth):
        if layer % 2 == 0:
            e = layer // 2
            w_in = ab_w_in[e].astype(BF16)
            o2 = MLA_QL + MLA_KVL
            o3 = o2 + MLA_ROPE
            lat = _mm(u, w_in[:, :o2], tm, o2)
            krope = _mm(u, jnp.pad(w_in[:, o2:o3], ((0, 0), (0, LANE - MLA_ROPE))), tm, LANE)
            na_qkv = _mm(u, w_in[:, o3:], tm, 1024)
            wq = jnp.pad(ab_w_uq[e].reshape(MLA_QL, MLA_H, MLA_NOPE + MLA_ROPE),
                         ((0, 0), (0, 0), (0, MLA_HP - MLA_NOPE - MLA_ROPE))).reshape(MLA_QL, MLA_H * MLA_HP)
            wkv = ab_w_ukv[e].reshape(MLA_KVL, MLA_H, MLA_NOPE + MLA_V)
            wkv = jnp.concatenate([wkv[:, :, :MLA_NOPE].reshape(MLA_KVL, -1), wkv[:, :, MLA_NOPE:].reshape(MLA_KVL, -1)], axis=1)
            q, k, v = _mla_up(lat, krope, ab_q_norm[e], ab_kv_norm[e], wq.astype(BF16), wkv.astype(BF16), tabs, tm)
            y_mla = _mla_attention(q, k, v, S, TAIL, min(S, 512))
            y_na = _na_attention(na_qkv, _na_bias_table(ab_rpb[e]), S)
            ys = [y_mla, y_na]
            w_out = ab_w_out[e]
        else:
            o = layer // 2
            z = _mm(u, c_w_in[o].astype(BF16), tm, 512)
            ys = [_swa_attention(z, c_sinks[o], S)]
            w_out = c_w_out[o]
        wr = jnp.pad(ec_w_router[layer], ((0, 0), (0, LANE - N_EXP)))
        h1, xe, aff_t = _outproj(h, ys, w_out.astype(BF16), norm_ffn[layer], wr, 256)
        last = layer == depth - 1
        g_next = norm_final if last else norm_mix[layer + 1]
        h, u = _moe(h1, xe, aff_t, ec_w_gate[layer].astype(BF16), ec_w_up[layer].astype(BF16),
                    ec_w_down[layer].astype(BF16), g_next, S, S if last else Lp, F32 if last else BF16)
    return u
```

```python
import functools
import math

import jax
import jax.numpy as jnp
from jax import lax
from jax.experimental import pallas as pl
from jax.experimental.pallas import tpu as pltpu

F32 = jnp.float32
BF16 = jnp.bfloat16
I32 = jnp.int32

N_META = 16
GRID_W = 64
QBLK = 128
EPS = 1e-6
NEG = -1e30

MLA_H = 8
MLA_NOPE = 128
MLA_ROPE = 64
MLA_V = 128
MLA_QL = 512
MLA_KVL = 512
MLA_HP = 256
ROPE_THETA = 10000.0

NA_H = 8
NA_D = 128
NA_KH = 8
NA_KW = 16

SWA_H = 32
SWA_KV = 4
SWA_G = SWA_H // SWA_KV
SWA_D = 64
SWA_WIN = 128

N_EXP = 16
CAP_F = 2

TAIL = 512
LANE = 128
ZCHUNK = 256
TOKB = 256


def _cp(sem, vmem_mb=48):
    return pltpu.CompilerParams(dimension_semantics=sem, vmem_limit_bytes=vmem_mb << 20)


def _dot(a, b):
    return jnp.dot(a, b, preferred_element_type=F32)


def _dot_nt(a, b):
    return lax.dot_general(a, b, (((1,), (1,)), ((), ())), preferred_element_type=F32)


def _rms(x, g):
    return x * lax.rsqrt(jnp.mean(x * x, axis=-1, keepdims=True) + EPS) * g


def _norm_body(x_ref, g_ref, o_ref):
    o_ref[0] = _rms(x_ref[0], g_ref[...]).astype(o_ref.dtype)


def _norm(h, g, tm):
    B, Lp, D = h.shape
    return pl.pallas_call(
        _norm_body,
        out_shape=jax.ShapeDtypeStruct((B, Lp, D), BF16),
        grid=(B, Lp // tm),
        in_specs=[pl.BlockSpec((1, tm, D), lambda b, i: (b, i, 0)), pl.BlockSpec((1, D), lambda b, i: (0, 0))],
        out_specs=pl.BlockSpec((1, tm, D), lambda b, i: (b, i, 0)),
        compiler_params=_cp(("parallel", "parallel")),
        name="rmsnorm",
    )(h, g.reshape(1, D))


def _mm_body(x_ref, w_ref, o_ref):
    o_ref[0] = _dot(x_ref[0], w_ref[...]).astype(o_ref.dtype)


def _mm(x, w, tm, tn):
    B, Lp, K = x.shape
    N = w.shape[1]
    return pl.pallas_call(
        _mm_body,
        out_shape=jax.ShapeDtypeStruct((B, Lp, N), BF16),
        grid=(B, Lp // tm, N // tn),
        in_specs=[pl.BlockSpec((1, tm, K), lambda b, i, j: (b, i, 0)), pl.BlockSpec((K, tn), lambda b, i, j: (0, j))],
        out_specs=pl.BlockSpec((1, tm, tn), lambda b, i, j: (b, i, j)),
        compiler_params=_cp(("parallel", "parallel", "parallel")),
        name="in_proj",
    )(x, w)


def _rope(pe, c, sn, sp):
    return pe * c + pltpu.roll(pe, 96, 1) * sn + pltpu.roll(pe, 32, 1) * sp


def _uq_body(lat_ref, g_ref, wt_ref, c_ref, s_ref, o_ref, *, scale):
    xn = _rms(lat_ref[0].astype(F32), g_ref[...]).astype(BF16)
    zt = _dot_nt(wt_ref[...], xn)
    c, s = c_ref[...], s_ref[...]
    half = MLA_ROPE // 2
    dt = o_ref.dtype
    for h in range(MLA_H):
        o = h * MLA_HP
        p1, p2, p3 = o + MLA_NOPE, o + MLA_NOPE + half, o + MLA_NOPE + MLA_ROPE
        x1, x2 = zt[p1:p2], zt[p2:p3]
        o_ref[0, o:p1, :] = (zt[o:p1] * scale).astype(dt)
        o_ref[0, p1:p2, :] = ((x1 * c - x2 * s) * scale).astype(dt)
        o_ref[0, p2:p3, :] = ((x2 * c + x1 * s) * scale).astype(dt)
        o_ref[0, p3:o + MLA_HP, :] = jnp.zeros((o + MLA_HP - p3, zt.shape[1]), dt)


def _ukv_body(lat_ref, g_ref, wk_ref, wvt_ref, kr_ref, c_ref, sn_ref, sp_ref, k_ref, vt_ref):
    xn = _rms(lat_ref[0].astype(F32), g_ref[...]).astype(BF16)
    zk = _dot(xn, wk_ref[...])
    kr = _rope(kr_ref[0].astype(F32), c_ref[...], sn_ref[...], sp_ref[...]).astype(k_ref.dtype)
    for h in range(MLA_H):
        k_ref[0, :, h * MLA_HP:h * MLA_HP + LANE] = zk[:, h * LANE:(h + 1) * LANE].astype(k_ref.dtype)
        k_ref[0, :, h * MLA_HP + LANE:(h + 1) * MLA_HP] = kr
    vt_ref[0] = _dot_nt(wvt_ref[...], xn).astype(vt_ref.dtype)


def _mla_up(lat, krope, qn, kvn, wqt, wk, wvt, tabs, tabs_t, tm):
    B, Lp, _ = lat.shape
    c, sn, sp = tabs
    ct, st = tabs_t
    tab_spec = pl.BlockSpec((tm, LANE), lambda b, i: (i, 0))
    tabt_spec = pl.BlockSpec((MLA_ROPE // 2, tm), lambda b, i: (0, i))
    scale = (MLA_NOPE + MLA_ROPE) ** -0.5 * math.log2(math.e)
    qt = pl.pallas_call(
        functools.partial(_uq_body, scale=scale),
        out_shape=jax.ShapeDtypeStruct((B, MLA_H * MLA_HP, Lp), BF16),
        grid=(B, Lp // tm),
        in_specs=[pl.BlockSpec((1, tm, MLA_QL), lambda b, i: (b, i, 0)),
                  pl.BlockSpec((1, MLA_QL), lambda b, i: (0, 0)),
                  pl.BlockSpec(wqt.shape, lambda b, i: (0, 0)),
                  tabt_spec, tabt_spec],
        out_specs=pl.BlockSpec((1, MLA_H * MLA_HP, tm), lambda b, i: (b, 0, i)),
        compiler_params=_cp(("parallel", "parallel")),
        name="mla_q_up",
    )(lat, qn.reshape(1, -1), wqt, ct, st)
    k, vt = pl.pallas_call(
        _ukv_body,
        out_shape=(jax.ShapeDtypeStruct((B, Lp, MLA_H * MLA_HP), BF16),
                   jax.ShapeDtypeStruct((B, MLA_H * MLA_V, Lp), BF16)),
        grid=(B, Lp // tm),
        in_specs=[pl.BlockSpec((1, tm, MLA_KVL), lambda b, i: (b, i, 1)),
                  pl.BlockSpec((1, MLA_KVL), lambda b, i: (0, 0)),
                  pl.BlockSpec(wk.shape, lambda b, i: (0, 0)),
                  pl.BlockSpec(wvt.shape, lambda b, i: (0, 0)),
                  pl.BlockSpec((1, tm, LANE), lambda b, i: (b, i, 0)),
                  tab_spec, tab_spec, tab_spec],
        out_specs=(pl.BlockSpec((1, tm, MLA_H * MLA_HP), lambda b, i: (b, i, 0)),
                   pl.BlockSpec((1, MLA_H * MLA_V, tm), lambda b, i: (b, 0, i))),
        compiler_params=_cp(("parallel", "parallel")),
        name="mla_kv_up",
    )(lat, kvn.reshape(1, -1), wk, wvt, krope, c, sn, sp)
    return qt, k, vt


def _mla_body(qt_ref, k_ref, vt_ref, o_ref, m_sc, l_sc, acc_sc, s0_sc, s1_sc, *, n_real, tkc):
    qt = qt_ref[0]
    tq = qt.shape[1]
    nch = n_real // tkc
    m_sc[...] = jnp.full(m_sc.shape, NEG, F32)
    l_sc[...] = jnp.zeros(l_sc.shape, F32)
    acc_sc[...] = jnp.zeros(acc_sc.shape, F32)

    def scores(c):
        r0 = pl.multiple_of(jnp.minimum(c, nch - 1) * tkc, tkc)
        return _dot(k_ref[0, pl.ds(r0, tkc), :], qt)

    def consume(st, vtc):
        m_prev = m_sc[...]
        m_new = jnp.maximum(m_prev, jnp.max(st, axis=0, keepdims=True))
        p = jnp.exp2(st - m_new)
        alpha = jnp.exp2(m_prev - m_new)
        l_sc[...] = alpha * l_sc[...] + jnp.sum(p, axis=0, keepdims=True)
        acc_sc[...] = alpha * acc_sc[...] + _dot(vtc, p.astype(BF16))
        m_sc[...] = m_new

    def values(c):
        return vt_ref[0, :, pl.ds(pl.multiple_of(c * tkc, tkc), tkc)]

    s0_sc[...] = scores(0)

    def body(c2, carry):
        c = 2 * c2
        s1_sc[...] = scores(c + 1)
        consume(s0_sc[...], values(c))
        s0_sc[...] = scores(c + 2)
        consume(s1_sc[...], values(c + 1))
        return carry

    lax.fori_loop(0, nch // 2, body, 0)
    consume(_dot(k_ref[0, n_real:n_real + N_META, :], qt), vt_ref[0, :, n_real:n_real + N_META])
    out = (acc_sc[...] / l_sc[...]).T
    i = pl.program_id(2)
    row = i * tq + lax.broadcasted_iota(I32, (tq, 1), 0)
    o_ref[0] = jnp.where(row < n_real + N_META, out, 0.0).astype(o_ref.dtype)


def _mla_attention(qt, k, vt, n_real, tq, tkc):
    B, Lp, _ = k.shape
    return pl.pallas_call(
        functools.partial(_mla_body, n_real=n_real, tkc=tkc),
        out_shape=jax.ShapeDtypeStruct((B, Lp, MLA_H * MLA_V), BF16),
        grid=(B, MLA_H, Lp // tq),
        in_specs=[pl.BlockSpec((1, MLA_HP, tq), lambda b, h, i: (b, h, i)),
                  pl.BlockSpec((1, Lp, MLA_HP), lambda b, h, i: (b, 0, h)),
                  pl.BlockSpec((1, MLA_V, Lp), lambda b, h, i: (b, h, 0))],
        out_specs=pl.BlockSpec((1, tq, MLA_V), lambda b, h, i: (b, i, h)),
        scratch_shapes=[pltpu.VMEM((1, tq), F32), pltpu.VMEM((1, tq), F32), pltpu.VMEM((MLA_V, tq), F32),
                        pltpu.VMEM((tkc, tq), F32), pltpu.VMEM((tkc, tq), F32)],
        compiler_params=_cp(("parallel", "parallel", "arbitrary")),
        name="mla_attention",
    )(qt, k, vt)


def _na_bias_table(rpb):
    col = jnp.arange(GRID_W)
    cs = jnp.clip(col - NA_KW // 2, 0, GRID_W - NA_KW)
    ok = (col[None, :] >= cs[:, None]) & (col[None, :] < cs[:, None] + NA_KW)
    dc = jnp.clip(col[None, :] - col[:, None], -(NA_KW - 1), NA_KW - 1) + (NA_KW - 1)
    d = jnp.arange(NA_KH)[:, None] + jnp.arange(NA_KH)[None, :]
    t = rpb[:, d][:, :, :, dc]
    t = jnp.where(ok[None, None, None], t, NEG)
    return t.transpose(0, 1, 3, 2, 4).reshape(rpb.shape[0], NA_KH, GRID_W, NA_KH * GRID_W).astype(F32)


def _na_body(q_ref, k_ref, v_ref, t_ref, o_ref, *, n_real, rpg):
    i = pl.program_id(2)
    rows = n_real // GRID_W
    scale = NA_D ** -0.5
    km = k_ref[0, n_real:n_real + N_META, :]
    vm = v_ref[0, n_real:n_real + N_META, :]

    @pl.when(i < rows // rpg)
    def _():
        for rr in range(rpg):
            r = i * rpg + rr
            rs = jnp.clip(r - NA_KH // 2, 0, rows - NA_KH)
            d0 = rs - r + (NA_KH - 1)
            k0 = pl.multiple_of(rs * GRID_W, GRID_W)
            qr = q_ref[0, rr * GRID_W:(rr + 1) * GRID_W, :]
            kw = k_ref[0, pl.ds(k0, NA_KH * GRID_W), :]
            vw = v_ref[0, pl.ds(k0, NA_KH * GRID_W), :]
            s = _dot_nt(qr, kw) * scale + t_ref[0, d0]
            sm = _dot_nt(qr, km) * scale
            m = jnp.maximum(jnp.max(s, axis=1, keepdims=True), jnp.max(sm, axis=1, keepdims=True))
            p = jnp.exp(s - m)
            pm = jnp.exp(sm - m)
            l = jnp.sum(p, axis=1, keepdims=True) + jnp.sum(pm, axis=1, keepdims=True)
            o = (_dot(p.astype(BF16), vw) + _dot(pm.astype(BF16), vm)) / l
            o_ref[0, rr * GRID_W:(rr + 1) * GRID_W, :] = o.astype(o_ref.dtype)

    @pl.when(i >= rows // rpg)
    def _():
        qm = q_ref[0, 0:N_META, :]
        sm = _dot_nt(qm, km) * scale
        pm = jnp.exp(sm - jnp.max(sm, axis=1, keepdims=True))
        o = _dot(pm.astype(BF16), vm) / jnp.sum(pm, axis=1, keepdims=True)
        o_ref[0] = jnp.zeros(o_ref.shape[1:], o_ref.dtype)
        o_ref[0, 0:N_META, :] = o.astype(o_ref.dtype)


def _na_attention(qkv, table, n_real):
    B, Lp, _ = qkv.shape
    rpg = TAIL // GRID_W
    tq = rpg * GRID_W
    return pl.pallas_call(
        functools.partial(_na_body, n_real=n_real, rpg=rpg),
        out_shape=jax.ShapeDtypeStruct((B, Lp, NA_H * NA_D), BF16),
        grid=(B, NA_H, Lp // tq),
        in_specs=[pl.BlockSpec((1, tq, NA_D), lambda b, h, i: (b, i, h)),
                  pl.BlockSpec((1, Lp, NA_D), lambda b, h, i: (b, 0, NA_H + h)),
                  pl.BlockSpec((1, Lp, NA_D), lambda b, h, i: (b, 0, 2 * NA_H + h)),
                  pl.BlockSpec((1,) + table.shape[1:], lambda b, h, i: (h, 0, 0, 0))],
        out_specs=pl.BlockSpec((1, tq, NA_D), lambda b, h, i: (b, i, h)),
        compiler_params=_cp(("parallel", "parallel", "arbitrary")),
        name="na_attention",
    )(qkv, qkv, qkv, table)


def _swa_body(sink_ref, q_ref, kp_ref, kc_ref, kn_ref, km_ref, vp_ref, vc_ref, vn_ref, vm_ref, o_ref, *, n_real):
    i = pl.program_id(1)
    nb = n_real // QBLK
    scale = SWA_D ** -0.5
    kmeta = km_ref[0, 0:N_META, :]
    vmeta = vm_ref[0, 0:N_META, :]

    def attend(qh, kh, vh, kmh, vmh, bias, ok, sink):
        s = _dot_nt(qh, kh) * scale + bias
        s = jnp.where(ok, s, NEG)
        sm = _dot_nt(qh, kmh) * scale
        m = jnp.maximum(jnp.maximum(jnp.max(s, axis=1, keepdims=True), jnp.max(sm, axis=1, keepdims=True)), sink)
        p = jnp.exp(s - m)
        pm = jnp.exp(sm - m)
        l = jnp.sum(p, axis=1, keepdims=True) + jnp.sum(pm, axis=1, keepdims=True) + jnp.exp(sink - m)
        return (_dot(p.astype(BF16), vh) + _dot(pm.astype(BF16), vmh)) / l

    def heads(nq, kall, vall, dist, ok, use_alibi):
        outs = []
        for h in range(SWA_H):
            kv = h // SWA_G
            qh = q_ref[0, 0:nq, h * SWA_D:(h + 1) * SWA_D]
            sl = slice(kv * SWA_D, (kv + 1) * SWA_D)
            bias = (-(2.0 ** (-8.0 * (h + 1) / SWA_H))) * dist if use_alibi else 0.0
            outs.append(attend(qh, kall[:, sl], vall[:, sl], kmeta[:, sl], vmeta[:, sl], bias, ok, sink_ref[h]))
        return jnp.concatenate(outs, axis=1)

    @pl.when(i < nb)
    def _():
        a = lax.broadcasted_iota(I32, (QBLK, 3 * QBLK), 0)
        j = lax.broadcasted_iota(I32, (QBLK, 3 * QBLK), 1)
        dist = jnp.abs(j - QBLK - a)
        key_real = i * QBLK - QBLK + j
        ok = (dist <= SWA_WIN) & (key_real >= 0) & (key_real < n_real)
        kall = jnp.concatenate([kp_ref[0], kc_ref[0], kn_ref[0]], axis=0)
        vall = jnp.concatenate([vp_ref[0], vc_ref[0], vn_ref[0]], axis=0)
        o_ref[0] = heads(QBLK, kall, vall, dist.astype(F32), ok, True).astype(o_ref.dtype)

    @pl.when(i == nb)
    def _():
        a = lax.broadcasted_iota(I32, (N_META, QBLK), 0)
        j = lax.broadcasted_iota(I32, (N_META, QBLK), 1)
        ok = (N_META + j - a) <= SWA_WIN
        o = heads(N_META, kn_ref[0], vn_ref[0], None, ok, False)
        o_ref[0] = jnp.zeros(o_ref.shape[1:], o_ref.dtype)
        o_ref[0, 0:N_META, :] = o.astype(o_ref.dtype)

    @pl.when(i > nb)
    def _():
        o_ref[0] = jnp.zeros(o_ref.shape[1:], o_ref.dtype)


def _swa_attention(z, sinks, n_real):
    B, Lp, _ = z.shape
    nb = n_real // QBLK
    qw = SWA_H * SWA_D
    kw = SWA_KV * SWA_D
    kblk = qw // kw
    prev = lambda b, i: (b, jnp.clip(i - 1, 0, nb - 1), kblk)
    cur = lambda b, i: (b, jnp.minimum(i, nb - 1), kblk)
    nxt = lambda b, i: (b, jnp.where(i >= nb, 0, jnp.minimum(i + 1, nb - 1)), kblk)
    met = lambda b, i: (b, nb, kblk)
    shift = lambda f: (lambda b, i: f(b, i)[:2] + (kblk + 1,))
    kspec = lambda f: pl.BlockSpec((1, QBLK, kw), f)
    return pl.pallas_call(
        functools.partial(_swa_body, n_real=n_real),
        out_shape=jax.ShapeDtypeStruct((B, Lp, qw), BF16),
        grid=(B, Lp // QBLK),
        in_specs=[pl.BlockSpec(memory_space=pltpu.SMEM),
                  pl.BlockSpec((1, QBLK, qw), lambda b, i: (b, i, 0)),
                  kspec(prev), kspec(cur), kspec(nxt), kspec(met),
                  kspec(shift(prev)), kspec(shift(cur)), kspec(shift(nxt)), kspec(shift(met))],
        out_specs=pl.BlockSpec((1, QBLK, qw), lambda b, i: (b, i, 0)),
        compiler_params=_cp(("parallel", "arbitrary")),
        name="swa_attention",
    )(sinks, z, z, z, z, z, z, z, z, z)


def _outproj_body(*refs, n_y):
    h_ref = refs[0]
    y_refs = refs[1:1 + n_y]
    wo_ref, g_ref, wr_ref, h1_ref, xe_ref, aff_ref = refs[1 + n_y:]
    D = h_ref.shape[2]
    acc = h_ref[0]
    k0 = 0
    for y_ref in y_refs:
        kk = y_ref.shape[2]
        acc = acc + _dot(y_ref[0], wo_ref[k0:k0 + kk, :])
        k0 += kk
    h1_ref[0] = acc
    xn = _rms(acc, g_ref[...])
    logits = jnp.dot(xn, wr_ref[...], preferred_element_type=F32, precision=lax.Precision.HIGHEST)
    lane = lax.broadcasted_iota(I32, logits.shape, 1)
    logits = jnp.where(lane < N_EXP, logits, NEG)
    e = jnp.exp(logits - jnp.max(logits, axis=1, keepdims=True))
    aff = e / jnp.sum(e, axis=1, keepdims=True)
    tm = xn.shape[0]
    xe_ref[0, :, :, 0:D] = xn.reshape(tm, 1, D)
    xe_ref[0, :, :, D:D + LANE] = aff.reshape(tm, 1, LANE)
    aff_ref[0] = aff.T[0:N_EXP, :]


def _outproj(h, ys, wo, g, wr, tm):
    B, Lp, D = h.shape
    row = lambda b, i: (b, i, 0)
    full = lambda b, i: (0, 0)
    return pl.pallas_call(
        functools.partial(_outproj_body, n_y=len(ys)),
        out_shape=(jax.ShapeDtypeStruct((B, Lp, D), F32),
                   jax.ShapeDtypeStruct((B, Lp, 1, D + LANE), F32),
                   jax.ShapeDtypeStruct((B, N_EXP, Lp), F32)),
        grid=(B, Lp // tm),
        in_specs=[pl.BlockSpec((1, tm, D), row)] + [pl.BlockSpec((1, tm, y.shape[2]), row) for y in ys]
        + [pl.BlockSpec(wo.shape, full), pl.BlockSpec((1, D), full), pl.BlockSpec(wr.shape, full)],
        out_specs=(pl.BlockSpec((1, tm, D), row), pl.BlockSpec((1, tm, 1, D + LANE), lambda b, i: (b, i, 0, 0)),
                   pl.BlockSpec((1, N_EXP, tm), lambda b, i: (b, 0, i))),
        compiler_params=_cp(("parallel", "parallel")),
        name="outproj_router",
    )(h, *ys, wo, g.reshape(1, D), wr)


def _topk_body(aff_ref, sel_ref, q_ref, lohi_ref, sel_sc, *, n_real, cap):
    Lp = aff_ref.shape[2]
    nch = Lp // LANE
    cm = n_real // LANE
    aff = aff_ref[0]
    tok = lax.broadcasted_iota(I32, aff.shape, 1)
    keys = jnp.where(tok < n_real + N_META, pltpu.bitcast(aff, I32), -1)

    def search(it, prefix):
        cand = prefix | lax.shift_left(jnp.int32(1), 30 - it)
        cnt = jnp.sum(jnp.where(keys >= cand, 1.0, 0.0), axis=1, keepdims=True)
        return jnp.where(cnt >= cap, cand, prefix)

    thr = lax.fori_loop(0, 31, search, jnp.zeros((N_EXP, 1), I32))
    need = cap - jnp.sum(jnp.where(keys > thr, 1.0, 0.0), axis=1, keepdims=True)

    ri = lax.broadcasted_iota(I32, (LANE, LANE), 0)
    ci = lax.broadcasted_iota(I32, (LANE, LANE), 1)
    upper = (ri <= ci).astype(BF16)
    er = lax.broadcasted_iota(I32, (N_EXP, N_EXP), 0)
    ec = lax.broadcasted_iota(I32, (N_EXP, N_EXP), 1)
    lower = (ec < er).astype(BF16)

    sel_sc[...] = jnp.zeros(sel_sc.shape, F32)

    def select(c, carry):
        c0 = pl.multiple_of(c * LANE, LANE)
        a = pltpu.bitcast(aff_ref[0, :, pl.ds(c0, LANE)], I32)
        t = c0 + lax.broadcasted_iota(I32, a.shape, 1)
        kc = jnp.where(t < n_real + N_META, a, -1)
        eq = (kc == thr).astype(F32)
        rank = _dot(eq.astype(BF16), upper) - eq + carry
        sel = jnp.where((kc > thr) | ((eq > 0) & (rank < need)), 1.0, 0.0)
        sel_sc[:, pl.ds(c0, LANE)] = sel
        return carry + jnp.sum(eq, axis=1, keepdims=True)

    carry = select(jnp.int32(cm), jnp.zeros((N_EXP, 1), F32))
    lax.fori_loop(0, cm, select, carry)

    def offsets(c, carry):
        off_c = carry
        c0 = pl.multiple_of(c * LANE, LANE)
        sel = sel_sc[:, pl.ds(c0, LANE)]
        selb = sel.astype(BF16)
        nt = jnp.sum(sel, axis=0, keepdims=True)
        off_incl = _dot(jnp.broadcast_to(nt, (8, LANE)).astype(BF16), upper)[0:1] + off_c
        r = _dot(lower, selb)
        sel_ref[0, :, pl.ds(c0, LANE)] = sel.astype(I32)
        q_ref[0, :, pl.ds(c0, LANE)] = (off_incl - nt + r).astype(I32)
        lohi_ref[0, 0:1, pl.ds(c0, LANE)] = (off_incl - nt).astype(I32)
        lohi_ref[0, 1:2, pl.ds(c0, LANE)] = off_incl.astype(I32)
        return off_c + jnp.sum(nt, axis=1, keepdims=True)

    lax.fori_loop(0, nch, offsets, jnp.zeros((1, 1), F32))


def _topk(aff_t, n_real, cap):
    B, E, Lp = aff_t.shape
    spec = pl.BlockSpec((1, E, Lp), lambda b: (b, 0, 0))
    return pl.pallas_call(
        functools.partial(_topk_body, n_real=n_real, cap=cap),
        out_shape=(jax.ShapeDtypeStruct((B, E, Lp), I32), jax.ShapeDtypeStruct((B, E, Lp), I32),
                   jax.ShapeDtypeStruct((B, 2, Lp), I32)),
        grid=(B,),
        in_specs=[spec],
        out_specs=(spec, spec, pl.BlockSpec((1, 2, Lp), lambda b: (b, 0, 0))),
        scratch_shapes=[pltpu.VMEM((E, Lp), F32)],
        compiler_params=_cp(("parallel",)),
        name="expert_topk",
    )(aff_t)


def _compact_body(sel_ref, idx_ref, *, n_tok, cap):
    def put(t, cnt):
        idx_ref[0, 0, 0, cnt] = t
        return cnt + sel_ref[0, 0, 0, t]

    lax.fori_loop(0, n_tok, put, jnp.int32(0), unroll=8)

    def fill(s, c):
        idx_ref[0, 0, 0, s] = 0
        return c

    lax.fori_loop(cap, idx_ref.shape[3], fill, 0)


def _compact(sel, n_tok, cap, cp):
    B, E, Lp = sel.shape
    return pl.pallas_call(
        functools.partial(_compact_body, n_tok=n_tok, cap=cap),
        out_shape=jax.ShapeDtypeStruct((B, E, 1, cp), I32),
        grid=(B, E),
        in_specs=[pl.BlockSpec((1, 1, 1, Lp), lambda b, e: (b, e, 0, 0), memory_space=pltpu.SMEM)],
        out_specs=pl.BlockSpec((1, 1, 1, cp), lambda b, e: (b, e, 0, 0), memory_space=pltpu.SMEM),
        compiler_params=_cp(("parallel", "parallel")),
        name="expert_compact",
    )(sel.reshape(B, E, 1, Lp))


def _ffn_body(idx_ref, q_ref, xe_ref, wg_ref, wu_ref, wd_ref, z_ref, xbuf, x2d_sc, xb_sc, yacc, ybuf, gsem, ssem, *,
              cap, zr):
    e = pl.program_id(0)
    b = pl.program_id(1)
    r = pl.program_id(2)
    f = pl.program_id(3)
    tr, D = xb_sc.shape
    total = N_EXP * cap

    @pl.when(f == 0)
    def _():
        @pl.when((e == 0) & (r == 0))
        def _():
            for r0 in range(total, zr, tr):
                n = min(tr, zr - r0)
                ybuf[0:n] = jnp.zeros((n, 1, D), F32)
                cp = pltpu.make_async_copy(ybuf.at[0:n], z_ref.at[b, r0:r0 + n], ssem)
                cp.start()
                cp.wait()

        def issue(s, c):
            t = idx_ref[0, 0, 0, r * tr + s]
            pltpu.make_async_copy(xe_ref.at[b, t], xbuf.at[s], gsem).start()
            return c

        lax.fori_loop(0, tr, issue, 0, unroll=8)
        pltpu.make_async_copy(xe_ref.at[b, 0:tr], xbuf, gsem).wait()
        x2d_sc[...] = xbuf[...].reshape(x2d_sc.shape)
        xb_sc[...] = x2d_sc[:, 0:D].astype(BF16)

    x = xb_sc[...]
    g = _dot(x, wg_ref[0])
    u = _dot(x, wu_ref[0])
    hmid = (g * jax.nn.sigmoid(g) * u).astype(BF16)
    contrib = _dot(hmid, wd_ref[0])

    @pl.when(f == 0)
    def _():
        yacc[...] = contrib

    @pl.when(f > 0)
    def _():
        yacc[...] += contrib

    @pl.when(f == pl.num_programs(3) - 1)
    def _():
        aff = x2d_sc[:, D:D + LANE]
        lane = lax.broadcasted_iota(I32, aff.shape, 1)
        gate = jnp.sum(jnp.where(lane == e, aff, 0.0), axis=1, keepdims=True)
        ybuf[...] = (yacc[...] * gate).reshape(tr, 1, D)
        n = jnp.clip(cap - r * tr, 0, tr)

        def issue(s, c):
            dst = q_ref[0, 0, 0, idx_ref[0, 0, 0, r * tr + s]]
            pltpu.make_async_copy(ybuf.at[s], z_ref.at[b, dst], ssem).start()
            return c

        lax.fori_loop(0, n, issue, 0)

        def drain(s, c):
            pltpu.make_async_copy(ybuf.at[0], z_ref.at[b, 0], ssem).wait()
            return c

        lax.fori_loop(0, n, drain, 0)


def _ffn(idx, q, xe, wg, wu, wd, cap, cr, zr, tf):
    B, E, _, cp = idx.shape
    Lp = q.shape[2]
    De = xe.shape[3]
    D, F = wg.shape[1], wg.shape[2]
    tr = cr // 3
    smem = lambda n: pl.BlockSpec((1, 1, 1, n), lambda e, b, r, f: (b, e, 0, 0), memory_space=pltpu.SMEM)
    return pl.pallas_call(
        functools.partial(_ffn_body, cap=cap, zr=zr),
        out_shape=jax.ShapeDtypeStruct((B, zr, 1, D), F32),
        grid=(E, B, cr // tr, F // tf),
        in_specs=[smem(cp), smem(Lp), pl.BlockSpec(memory_space=pl.ANY),
                  pl.BlockSpec((1, D, tf), lambda e, b, r, f: (e, 0, f)),
                  pl.BlockSpec((1, D, tf), lambda e, b, r, f: (e, 0, f)),
                  pl.BlockSpec((1, tf, D), lambda e, b, r, f: (e, f, 0))],
        out_specs=pl.BlockSpec(memory_space=pl.ANY),
        scratch_shapes=[pltpu.VMEM((tr, 1, De), F32), pltpu.VMEM((tr, De), F32), pltpu.VMEM((tr, D), BF16),
                        pltpu.VMEM((tr, D), F32),
                        pltpu.VMEM((tr, 1, D), F32), pltpu.SemaphoreType.DMA(()), pltpu.SemaphoreType.DMA(())],
        compiler_params=_cp(("arbitrary", "arbitrary", "arbitrary", "arbitrary"), 56),
        name="expert_ffn",
    )(idx, q.reshape(B, E, 1, Lp), xe, wg, wu, wd)


def _combine_body(k0_ref, k1_ref, h_ref, lohi_ref, g_ref, z_ref, h2_ref, u_ref, buf, zb_sc, acc_sc, sem, *, nblk):
    b = pl.program_id(0)
    j = pl.program_id(1)
    k0 = k0_ref[b * nblk + j]
    k1 = k1_ref[b * nblk + j]
    lo = lohi_ref[0, 0:1, :]
    hi = lohi_ref[0, 1:2, :]
    acc_sc[...] = jnp.zeros(acc_sc.shape, F32)

    def copy(k, slot):
        r0 = pl.multiple_of(k * ZCHUNK, ZCHUNK)
        return pltpu.make_async_copy(z_ref.at[b, pl.ds(r0, ZCHUNK)], buf.at[slot], sem.at[slot])

    @pl.when(k1 > k0)
    def _():
        copy(k0, 0).start()

    def chunk(k, c):
        slot = lax.rem(k - k0, 2)
        copy(k, slot).wait()

        @pl.when(k + 1 < k1)
        def _():
            copy(k + 1, 1 - slot).start()

        w = k * ZCHUNK + lax.broadcasted_iota(I32, (ZCHUNK, lo.shape[1]), 0)
        band_t = jnp.where((w >= lo) & (w < hi), 1.0, 0.0)
        zb_sc[...] = buf[slot].reshape(zb_sc.shape)
        acc_sc[...] += _dot(band_t.T.astype(BF16), zb_sc[...].astype(BF16))
        return c

    lax.fori_loop(k0, k1, chunk, 0)
    h2 = h_ref[0] + acc_sc[...]
    h2_ref[0] = h2
    u_ref[0] = _rms(h2, g_ref[...]).astype(u_ref.dtype)


def _combine(h1, lohi, z, g, n_rows, u_dtype):
    B, Lp, D = h1.shape
    nblk = n_rows // TOKB
    lo = lohi[:, 0, :n_rows].reshape(B, nblk, TOKB)
    hi = lohi[:, 1, :n_rows].reshape(B, nblk, TOKB)
    k0 = (lo[:, :, 0] // ZCHUNK).reshape(-1).astype(I32)
    k1 = ((hi[:, :, -1] + ZCHUNK - 1) // ZCHUNK).reshape(-1).astype(I32)
    k1 = jnp.where(hi[:, :, -1].reshape(-1) > lo[:, :, 0].reshape(-1), k1, k0)
    row = lambda b, j, *_: (b, j, 0)
    grid_spec = pltpu.PrefetchScalarGridSpec(
        num_scalar_prefetch=2,
        grid=(B, nblk),
        in_specs=[pl.BlockSpec((1, TOKB, D), row),
                  pl.BlockSpec((1, 2, TOKB), lambda b, j, *_: (b, 0, j)),
                  pl.BlockSpec((1, D), lambda b, j, *_: (0, 0)),
                  pl.BlockSpec(memory_space=pl.ANY)],
        out_specs=(pl.BlockSpec((1, TOKB, D), row), pl.BlockSpec((1, TOKB, D), row)),
        scratch_shapes=[pltpu.VMEM((2, ZCHUNK, 1, D), z.dtype), pltpu.VMEM((ZCHUNK, D), F32),
                        pltpu.VMEM((TOKB, D), F32), pltpu.SemaphoreType.DMA((2,))],
    )
    return pl.pallas_call(
        functools.partial(_combine_body, nblk=nblk),
        out_shape=(jax.ShapeDtypeStruct((B, n_rows, D), F32), jax.ShapeDtypeStruct((B, n_rows, D), u_dtype)),
        grid_spec=grid_spec,
        compiler_params=_cp(("arbitrary", "arbitrary")),
        name="moe_combine",
    )(k0, k1, h1, lohi, g.reshape(1, D), z)


def _moe(h1, xe, aff_t, wg, wu, wd, g_next, n_real, n_rows_out, u_dtype):
    B, Lp, D = h1.shape
    cap = (CAP_F * (n_real + N_META)) // N_EXP
    cr = -(-cap // 48) * 48
    cp = -(-(cap + 1) // LANE) * LANE
    zr = -(-(N_EXP * cap) // ZCHUNK) * ZCHUNK
    if zr == N_EXP * cap:
        zr += ZCHUNK
    sel, q, lohi = _topk(aff_t, n_real, cap)
    idx = _compact(sel, n_real + N_META, cap, cp)
    z = _ffn(idx, q, xe, wg, wu, wd, cap, cr, zr, min(wg.shape[2], 512))
    return _combine(h1, lohi, z, g_next, n_rows_out, u_dtype)


def _rope_tables(n_real, Lp):
    pos = jnp.concatenate([jnp.arange(n_real) + N_META, jnp.arange(N_META), jnp.zeros((Lp - n_real - N_META,), I32)])
    inv_freq = 1.0 / (ROPE_THETA ** (jnp.arange(0, MLA_ROPE, 2, dtype=F32) / MLA_ROPE))
    ang = pos.astype(F32)[:, None] * inv_freq[None, :]
    cos, sin = jnp.cos(ang), jnp.sin(ang)
    half = MLA_ROPE // 2
    z = lambda n: jnp.zeros((Lp, n), F32)
    c = jnp.concatenate([cos, cos, z(LANE - 2 * half)], axis=1)
    sn = jnp.concatenate([-sin, z(LANE - half)], axis=1)
    sp = jnp.concatenate([z(half), sin, z(LANE - 2 * half)], axis=1)
    return (c, sn, sp), (cos.T, sin.T)


def kernel(x, meta_tokens, norm_mix, norm_ffn, norm_final, ab_w_in, ab_q_norm, ab_kv_norm, ab_w_uq, ab_w_ukv, ab_rpb, ab_w_out, c_w_in, c_sinks, c_w_out, ec_w_router, ec_w_gate, ec_w_up, ec_w_down):
    B, S, D = x.shape
    depth = norm_mix.shape[0]
    Lp = S + TAIL
    tm = 512
    meta = jnp.broadcast_to(meta_tokens[None].astype(x.dtype), (B, N_META, D))
    h = jnp.concatenate([x, meta, jnp.zeros((B, TAIL - N_META, D), x.dtype)], axis=1)
    u = _norm(h, norm_mix[0], tm)
    tabs, tabs_t = _rope_tables(S, Lp)

    for layer in range(depth):
        if layer % 2 == 0:
            e = layer // 2
            w_in = ab_w_in[e].astype(BF16)
            o2 = MLA_QL + MLA_KVL
            o3 = o2 + MLA_ROPE
            lat = _mm(u, w_in[:, :o2], tm, o2)
            krope = _mm(u, jnp.pad(w_in[:, o2:o3], ((0, 0), (0, LANE - MLA_ROPE))), tm, LANE)
            na_qkv = _mm(u, w_in[:, o3:], tm, 1024)
            wq = jnp.pad(ab_w_uq[e].reshape(MLA_QL, MLA_H, MLA_NOPE + MLA_ROPE),
                         ((0, 0), (0, 0), (0, MLA_HP - MLA_NOPE - MLA_ROPE))).reshape(MLA_QL, MLA_H * MLA_HP)
            wkv = ab_w_ukv[e].reshape(MLA_KVL, MLA_H, MLA_NOPE + MLA_V)
            wk = wkv[:, :, :MLA_NOPE].reshape(MLA_KVL, -1)
            wv = wkv[:, :, MLA_NOPE:].reshape(MLA_KVL, -1)
            qt, k, vt = _mla_up(lat, krope, ab_q_norm[e], ab_kv_norm[e], wq.T.astype(BF16), wk.astype(BF16),
                                wv.T.astype(BF16), tabs, tabs_t, tm)
            y_mla = _mla_attention(qt, k, vt, S, TAIL, min(S, 512))
            y_na = _na_attention(na_qkv, _na_bias_table(ab_rpb[e]), S)
            ys = [y_mla, y_na]
            w_out = ab_w_out[e]
        else:
            o = layer // 2
            z = _mm(u, c_w_in[o].astype(BF16), tm, 512)
            ys = [_swa_attention(z, c_sinks[o], S)]
            w_out = c_w_out[o]
        wr = jnp.pad(ec_w_router[layer], ((0, 0), (0, LANE - N_EXP)))
        h1, xe, aff_t = _outproj(h, ys, w_out.astype(BF16), norm_ffn[layer], wr, 256)
        last = layer == depth - 1
        g_next = norm_final if last else norm_mix[layer + 1]
        h, u = _moe(h1, xe, aff_t, ec_w_gate[layer].astype(BF16), ec_w_up[layer].astype(BF16),
                    ec_w_down[layer].astype(BF16), g_next, S, S if last else Lp, F32 if last else BF16)
    return u
```

```python
import functools
import math

import jax
import jax.numpy as jnp
from jax import lax
from jax.experimental import pallas as pl
from jax.experimental.pallas import tpu as pltpu

F32 = jnp.float32
BF16 = jnp.bfloat16
I32 = jnp.int32

N_META = 16
GRID_W = 64
QBLK = 128
EPS = 1e-6
NEG = -1e30

MLA_H = 8
MLA_NOPE = 128
MLA_ROPE = 64
MLA_V = 128
MLA_QL = 512
MLA_KVL = 512
MLA_HP = 256
ROPE_THETA = 10000.0

NA_H = 8
NA_D = 128
NA_KH = 8
NA_KW = 16

SWA_H = 32
SWA_KV = 4
SWA_G = SWA_H // SWA_KV
SWA_D = 64
SWA_WIN = 128

N_EXP = 16
CAP_F = 2

TAIL = 512
LANE = 128
ZCHUNK = 256
TOKB = 256


def _cp(sem, vmem_mb=48):
    return pltpu.CompilerParams(dimension_semantics=sem, vmem_limit_bytes=vmem_mb << 20)


def _dot(a, b):
    return jnp.dot(a, b, preferred_element_type=F32)


def _dot_nt(a, b):
    return lax.dot_general(a, b, (((1,), (1,)), ((), ())), preferred_element_type=F32)


def _rms(x, g):
    return x * lax.rsqrt(jnp.mean(x * x, axis=-1, keepdims=True) + EPS) * g


def _norm_body(x_ref, g_ref, o_ref):
    o_ref[0] = _rms(x_ref[0], g_ref[...]).astype(o_ref.dtype)


def _norm(h, g, tm):
    B, Lp, D = h.shape
    return pl.pallas_call(
        _norm_body,
        out_shape=jax.ShapeDtypeStruct((B, Lp, D), BF16),
        grid=(B, Lp // tm),
        in_specs=[pl.BlockSpec((1, tm, D), lambda b, i: (b, i, 0)), pl.BlockSpec((1, D), lambda b, i: (0, 0))],
        out_specs=pl.BlockSpec((1, tm, D), lambda b, i: (b, i, 0)),
        compiler_params=_cp(("parallel", "parallel")),
        name="rmsnorm",
    )(h, g.reshape(1, D))


def _mm_body(x_ref, w_ref, o_ref):
    o_ref[0] = _dot(x_ref[0], w_ref[...]).astype(o_ref.dtype)


def _mm(x, w, tm, tn):
    B, Lp, K = x.shape
    N = w.shape[1]
    return pl.pallas_call(
        _mm_body,
        out_shape=jax.ShapeDtypeStruct((B, Lp, N), BF16),
        grid=(B, Lp // tm, N // tn),
        in_specs=[pl.BlockSpec((1, tm, K), lambda b, i, j: (b, i, 0)), pl.BlockSpec((K, tn), lambda b, i, j: (0, j))],
        out_specs=pl.BlockSpec((1, tm, tn), lambda b, i, j: (b, i, j)),
        compiler_params=_cp(("parallel", "parallel", "parallel")),
        name="in_proj",
    )(x, w)


def _mm_t_body(x_ref, wt_ref, o_ref):
    o_ref[0] = _dot_nt(wt_ref[...], x_ref[0]).astype(o_ref.dtype)


def _mm_t(x, wt, tm, tn):
    B, Lp, K = x.shape
    N = wt.shape[0]
    return pl.pallas_call(
        _mm_t_body,
        out_shape=jax.ShapeDtypeStruct((B, N, Lp), BF16),
        grid=(B, Lp // tm, N // tn),
        in_specs=[pl.BlockSpec((1, tm, K), lambda b, i, j: (b, i, 0)), pl.BlockSpec((tn, K), lambda b, i, j: (j, 0))],
        out_specs=pl.BlockSpec((1, tn, tm), lambda b, i, j: (b, j, i)),
        compiler_params=_cp(("parallel", "parallel", "parallel")),
        name="in_proj_t",
    )(x, wt)


def _rope(pe, c, sn, sp):
    return pe * c + pltpu.roll(pe, 96, 1) * sn + pltpu.roll(pe, 32, 1) * sp


def _uq_body(lat_ref, g_ref, wt_ref, c_ref, s_ref, o_ref, *, scale):
    xn = _rms(lat_ref[0].astype(F32), g_ref[...]).astype(BF16)
    zt = _dot_nt(wt_ref[...], xn)
    c, s = c_ref[...], s_ref[...]
    half = MLA_ROPE // 2
    dt = o_ref.dtype
    for h in range(MLA_H):
        o = h * MLA_HP
        p1, p2, p3 = o + MLA_NOPE, o + MLA_NOPE + half, o + MLA_NOPE + MLA_ROPE
        x1, x2 = zt[p1:p2], zt[p2:p3]
        o_ref[0, o:p1, :] = (zt[o:p1] * scale).astype(dt)
        o_ref[0, p1:p2, :] = ((x1 * c - x2 * s) * scale).astype(dt)
        o_ref[0, p2:p3, :] = ((x2 * c + x1 * s) * scale).astype(dt)
        o_ref[0, p3:o + MLA_HP, :] = jnp.zeros((o + MLA_HP - p3, zt.shape[1]), dt)


def _ukv_body(lat_ref, g_ref, wk_ref, wvt_ref, kr_ref, c_ref, sn_ref, sp_ref, k_ref, vt_ref):
    xn = _rms(lat_ref[0].astype(F32), g_ref[...]).astype(BF16)
    zk = _dot(xn, wk_ref[...])
    kr = _rope(kr_ref[0].astype(F32), c_ref[...], sn_ref[...], sp_ref[...]).astype(k_ref.dtype)
    for h in range(MLA_H):
        k_ref[0, :, h * MLA_HP:h * MLA_HP + LANE] = zk[:, h * LANE:(h + 1) * LANE].astype(k_ref.dtype)
        k_ref[0, :, h * MLA_HP + LANE:(h + 1) * MLA_HP] = kr
    vt_ref[0] = _dot_nt(wvt_ref[...], xn).astype(vt_ref.dtype)


def _mla_up(lat, krope, qn, kvn, wqt, wk, wvt, tabs, tabs_t, tm):
    B, Lp, _ = lat.shape
    c, sn, sp = tabs
    ct, st = tabs_t
    tab_spec = pl.BlockSpec((tm, LANE), lambda b, i: (i, 0))
    tabt_spec = pl.BlockSpec((MLA_ROPE // 2, tm), lambda b, i: (0, i))
    scale = (MLA_NOPE + MLA_ROPE) ** -0.5 * math.log2(math.e)
    qt = pl.pallas_call(
        functools.partial(_uq_body, scale=scale),
        out_shape=jax.ShapeDtypeStruct((B, MLA_H * MLA_HP, Lp), BF16),
        grid=(B, Lp // tm),
        in_specs=[pl.BlockSpec((1, tm, MLA_QL), lambda b, i: (b, i, 0)),
                  pl.BlockSpec((1, MLA_QL), lambda b, i: (0, 0)),
                  pl.BlockSpec(wqt.shape, lambda b, i: (0, 0)),
                  tabt_spec, tabt_spec],
        out_specs=pl.BlockSpec((1, MLA_H * MLA_HP, tm), lambda b, i: (b, 0, i)),
        compiler_params=_cp(("parallel", "parallel")),
        name="mla_q_up",
    )(lat, qn.reshape(1, -1), wqt, ct, st)
    k, vt = pl.pallas_call(
        _ukv_body,
        out_shape=(jax.ShapeDtypeStruct((B, Lp, MLA_H * MLA_HP), BF16),
                   jax.ShapeDtypeStruct((B, MLA_H * MLA_V, Lp), BF16)),
        grid=(B, Lp // tm),
        in_specs=[pl.BlockSpec((1, tm, MLA_KVL), lambda b, i: (b, i, 1)),
                  pl.BlockSpec((1, MLA_KVL), lambda b, i: (0, 0)),
                  pl.BlockSpec(wk.shape, lambda b, i: (0, 0)),
                  pl.BlockSpec(wvt.shape, lambda b, i: (0, 0)),
                  pl.BlockSpec((1, tm, LANE), lambda b, i: (b, i, 0)),
                  tab_spec, tab_spec, tab_spec],
        out_specs=(pl.BlockSpec((1, tm, MLA_H * MLA_HP), lambda b, i: (b, i, 0)),
                   pl.BlockSpec((1, MLA_H * MLA_V, tm), lambda b, i: (b, 0, i))),
        compiler_params=_cp(("parallel", "parallel")),
        name="mla_kv_up",
    )(lat, kvn.reshape(1, -1), wk, wvt, krope, c, sn, sp)
    return qt, k, vt


def _mla_body(qt_ref, k_ref, vt_ref, o_ref, m_sc, acc_sc, s0_sc, s1_sc, *, n_real, tkc):
    qt = qt_ref[0]
    tq = qt.shape[1]
    nch = n_real // tkc
    m_sc[...] = jnp.full(m_sc.shape, NEG, F32)
    acc_sc[...] = jnp.zeros(acc_sc.shape, F32)

    def scores(c):
        r0 = pl.multiple_of(jnp.minimum(c, nch - 1) * tkc, tkc)
        return _dot(k_ref[0, pl.ds(r0, tkc), :], qt)

    def consume(st, vtc):
        m_prev = m_sc[...]
        m_new = jnp.maximum(m_prev, jnp.max(st, axis=0, keepdims=True))
        p = jnp.exp2((st - m_new).astype(BF16))
        alpha = jnp.exp2(m_prev - m_new)
        vte = jnp.concatenate([vtc, jnp.ones((16, vtc.shape[1]), BF16)], axis=0)
        acc_sc[...] = alpha * acc_sc[...] + _dot(vte, p)
        m_sc[...] = m_new

    def values(c):
        return vt_ref[0, :, pl.ds(pl.multiple_of(c * tkc, tkc), tkc)]

    s0_sc[...] = scores(0)

    pairs = 2 if nch % 4 == 0 else 1

    def body(cb, carry):
        for u in range(pairs):
            c = 2 * (pairs * cb + u)
            s1_sc[...] = scores(c + 1)
            consume(s0_sc[...], values(c))
            s0_sc[...] = scores(c + 2)
            consume(s1_sc[...], values(c + 1))
        return carry

    lax.fori_loop(0, nch // (2 * pairs), body, 0)
    consume(_dot(k_ref[0, n_real:n_real + N_META, :], qt), vt_ref[0, :, n_real:n_real + N_META])
    out = (acc_sc[0:MLA_V, :] / acc_sc[MLA_V:MLA_V + 1, :]).T
    i = pl.program_id(2)
    row = i * tq + lax.broadcasted_iota(I32, (tq, 1), 0)
    o_ref[0] = jnp.where(row < n_real + N_META, out, 0.0).astype(o_ref.dtype)


def _mla_attention(qt, k, vt, n_real, tq, tkc):
    B, Lp, _ = k.shape
    return pl.pallas_call(
        functools.partial(_mla_body, n_real=n_real, tkc=tkc),
        out_shape=jax.ShapeDtypeStruct((B, Lp, MLA_H * MLA_V), BF16),
        grid=(B, MLA_H, Lp // tq),
        in_specs=[pl.BlockSpec((1, MLA_HP, tq), lambda b, h, i: (b, h, i)),
                  pl.BlockSpec((1, Lp, MLA_HP), lambda b, h, i: (b, 0, h)),
                  pl.BlockSpec((1, MLA_V, Lp), lambda b, h, i: (b, h, 0))],
        out_specs=pl.BlockSpec((1, tq, MLA_V), lambda b, h, i: (b, i, h)),
        scratch_shapes=[pltpu.VMEM((1, tq), F32), pltpu.VMEM((MLA_V + 16, tq), F32),
                        pltpu.VMEM((tkc, tq), F32), pltpu.VMEM((tkc, tq), F32)],
        compiler_params=_cp(("parallel", "parallel", "arbitrary")),
        name="mla_attention",
    )(qt, k, vt)


def _na_bias_table(rpb):
    col = jnp.arange(GRID_W)
    cs = jnp.clip(col - NA_KW // 2, 0, GRID_W - NA_KW)
    ok = (col[None, :] >= cs[:, None]) & (col[None, :] < cs[:, None] + NA_KW)
    dc = jnp.clip(col[None, :] - col[:, None], -(NA_KW - 1), NA_KW - 1) + (NA_KW - 1)
    d = jnp.arange(NA_KH)[:, None] + jnp.arange(NA_KH)[None, :]
    t = rpb[:, d][:, :, :, dc]
    t = jnp.where(ok[None, None, None], t, NEG)
    return t.transpose(0, 1, 3, 2, 4).reshape(rpb.shape[0], NA_KH, GRID_W, NA_KH * GRID_W).astype(F32)


def _na_body(q_ref, k_ref, v_ref, t_ref, o_ref, *, n_real, rpg):
    i = pl.program_id(2)
    rows = n_real // GRID_W
    scale = NA_D ** -0.5
    km = k_ref[0, n_real:n_real + N_META, :]
    vm = v_ref[0, n_real:n_real + N_META, :]

    @pl.when(i < rows // rpg)
    def _():
        for rr in range(rpg):
            r = i * rpg + rr
            rs = jnp.clip(r - NA_KH // 2, 0, rows - NA_KH)
            d0 = rs - r + (NA_KH - 1)
            k0 = pl.multiple_of(rs * GRID_W, GRID_W)
            qr = q_ref[0, rr * GRID_W:(rr + 1) * GRID_W, :]
            kw = k_ref[0, pl.ds(k0, NA_KH * GRID_W), :]
            vw = v_ref[0, pl.ds(k0, NA_KH * GRID_W), :]
            s = _dot_nt(qr, kw) * scale + t_ref[0, d0]
            sm = _dot_nt(qr, km) * scale
            m = jnp.maximum(jnp.max(s, axis=1, keepdims=True), jnp.max(sm, axis=1, keepdims=True))
            p = jnp.exp(s - m)
            pm = jnp.exp(sm - m)
            l = jnp.sum(p, axis=1, keepdims=True) + jnp.sum(pm, axis=1, keepdims=True)
            o = (_dot(p.astype(BF16), vw) + _dot(pm.astype(BF16), vm)) / l
            o_ref[0, rr * GRID_W:(rr + 1) * GRID_W, :] = o.astype(o_ref.dtype)

    @pl.when(i >= rows // rpg)
    def _():
        qm = q_ref[0, 0:N_META, :]
        sm = _dot_nt(qm, km) * scale
        pm = jnp.exp(sm - jnp.max(sm, axis=1, keepdims=True))
        o = _dot(pm.astype(BF16), vm) / jnp.sum(pm, axis=1, keepdims=True)
        o_ref[0] = jnp.zeros(o_ref.shape[1:], o_ref.dtype)
        o_ref[0, 0:N_META, :] = o.astype(o_ref.dtype)


def _na_attention(qkv, table, n_real):
    B, Lp, _ = qkv.shape
    rpg = TAIL // GRID_W
    tq = rpg * GRID_W
    return pl.pallas_call(
        functools.partial(_na_body, n_real=n_real, rpg=rpg),
        out_shape=jax.ShapeDtypeStruct((B, Lp, NA_H * NA_D), BF16),
        grid=(B, NA_H, Lp // tq),
        in_specs=[pl.BlockSpec((1, tq, NA_D), lambda b, h, i: (b, i, h)),
                  pl.BlockSpec((1, Lp, NA_D), lambda b, h, i: (b, 0, NA_H + h)),
                  pl.BlockSpec((1, Lp, NA_D), lambda b, h, i: (b, 0, 2 * NA_H + h)),
                  pl.BlockSpec((1,) + table.shape[1:], lambda b, h, i: (h, 0, 0, 0))],
        out_specs=pl.BlockSpec((1, tq, NA_D), lambda b, h, i: (b, i, h)),
        compiler_params=_cp(("parallel", "parallel", "arbitrary")),
        name="na_attention",
    )(qkv, qkv, qkv, table)


def _swa_bias_tables():
    a = jnp.arange(QBLK)[None, :]
    j = jnp.arange(3 * QBLK)[:, None]
    dist = jnp.abs(j - QBLK - a)
    slopes = 2.0 ** (-8.0 * jnp.arange(1, SWA_H + 1, dtype=F32) / SWA_H)
    real = jnp.where(dist <= SWA_WIN, -slopes[:, None, None] * dist.astype(F32) * math.log2(math.e), NEG)
    meta = jnp.where(N_META + jnp.arange(QBLK)[:, None] - a <= SWA_WIN, 0.0, NEG)
    return real.astype(F32), meta.astype(F32)


def _swa_body(sink_ref, qt_ref, kp_ref, kc_ref, kn_ref, km_ref, vp_ref, vc_ref, vn_ref, vm_ref, tr_ref, tm_ref, o_ref,
              *, n_real):
    i = pl.program_id(1)
    nb = n_real // QBLK
    log2e = math.log2(math.e)
    gw = SWA_G * SWA_D
    kmeta = km_ref[0, 0:N_META, :]
    vmeta_t = vm_ref[0, :, 0:N_META]

    def attend(kall, vall_t, bias_of):
        for kv in range(SWA_KV):
            tile, half = kv // 2, kv % 2
            q64 = jnp.concatenate([qt_ref[0, kv * gw + g * SWA_D:kv * gw + (g + 1) * SWA_D, :] for g in range(SWA_G)],
                                  axis=1)
            z64 = jnp.zeros_like(q64)
            qpad = jnp.concatenate([q64, z64] if half == 0 else [z64, q64], axis=0)
            st = _dot(kall[:, tile * LANE:(tile + 1) * LANE], qpad)
            sm = _dot(kmeta[:, tile * LANE:(tile + 1) * LANE], qpad)
            ps, ls = [], []
            for g in range(SWA_G):
                h = kv * SWA_G + g
                cols = slice(g * QBLK, (g + 1) * QBLK)
                s = st[:, cols] + bias_of(h)
                smg = sm[:, cols]
                sink = sink_ref[h] * log2e
                m = jnp.maximum(jnp.maximum(jnp.max(s, axis=0, keepdims=True), jnp.max(smg, axis=0, keepdims=True)),
                                sink)
                p = jnp.exp2(s - m)
                pm = jnp.exp2(smg - m)
                ls.append(jnp.sum(p, axis=0, keepdims=True) + jnp.sum(pm, axis=0, keepdims=True) + jnp.exp2(sink - m))
                ps.append((p.astype(BF16), pm.astype(BF16)))
            p_all = jnp.concatenate([p for p, _ in ps], axis=1)
            pm_all = jnp.concatenate([pm for _, pm in ps], axis=1)
            rows = slice(kv * SWA_D, (kv + 1) * SWA_D)
            ot = (_dot(vall_t[rows, :], p_all) + _dot(vmeta_t[rows, :], pm_all)) / jnp.concatenate(ls, axis=1)
            for g in range(SWA_G):
                o_ref[0, kv * gw + g * SWA_D:kv * gw + (g + 1) * SWA_D, :] = ot[:, g * QBLK:(g + 1) * QBLK].astype(
                    o_ref.dtype)

    @pl.when(i < nb)
    def _():
        kall = jnp.concatenate([kp_ref[0], kc_ref[0], kn_ref[0]], axis=0)
        vall_t = jnp.concatenate([vp_ref[0], vc_ref[0], vn_ref[0]], axis=1)
        row = lax.broadcasted_iota(I32, (3 * QBLK, QBLK), 0)
        pen = jnp.where(((i == 0) & (row < QBLK)) | ((i == nb - 1) & (row >= 2 * QBLK)), NEG, 0.0)
        attend(kall, vall_t, lambda h: tr_ref[h] + pen)

    @pl.when(i == nb)
    def _():
        attend(kn_ref[0], vn_ref[0], lambda h: tm_ref[...])
        lane = lax.broadcasted_iota(I32, o_ref.shape[1:], 1)
        o_ref[0] = jnp.where(lane < N_META, o_ref[0], jnp.zeros(o_ref.shape[1:], o_ref.dtype))

    @pl.when(i > nb)
    def _():
        o_ref[0] = jnp.zeros(o_ref.shape[1:], o_ref.dtype)


def _swa_attention(qvt, k, sinks, n_real):
    B, Lp, kw = k.shape
    nb = n_real // QBLK
    qw = SWA_H * SWA_D
    vrow = qw // kw
    prev = lambda i: jnp.clip(i - 1, 0, nb - 1)
    cur = lambda i: jnp.minimum(i, nb - 1)
    nxt = lambda i: jnp.where(i >= nb, 0, jnp.minimum(i + 1, nb - 1))
    met = lambda i: nb
    kspec = lambda f: pl.BlockSpec((1, QBLK, kw), lambda b, i: (b, f(i), 0))
    vspec = lambda f: pl.BlockSpec((1, kw, QBLK), lambda b, i: (b, vrow, f(i)))
    t_real, t_meta = _swa_bias_tables()
    return pl.pallas_call(
        functools.partial(_swa_body, n_real=n_real),
        out_shape=jax.ShapeDtypeStruct((B, qw, Lp), BF16),
        grid=(B, Lp // QBLK),
        in_specs=[pl.BlockSpec(memory_space=pltpu.SMEM),
                  pl.BlockSpec((1, qw, QBLK), lambda b, i: (b, 0, i)),
                  kspec(prev), kspec(cur), kspec(nxt), kspec(met),
                  vspec(prev), vspec(cur), vspec(nxt), vspec(met),
                  pl.BlockSpec(t_real.shape, lambda b, i: (0, 0, 0)),
                  pl.BlockSpec(t_meta.shape, lambda b, i: (0, 0))],
        out_specs=pl.BlockSpec((1, qw, QBLK), lambda b, i: (b, 0, i)),
        compiler_params=_cp(("parallel", "arbitrary")),
        name="swa_attention",
    )(sinks, qvt, k, k, k, k, qvt, qvt, qvt, qvt, t_real, t_meta)


def _outproj_body(*refs, n_y, y_t):
    h_ref = refs[0]
    y_refs = refs[1:1 + n_y]
    wo_ref, g_ref, wr_ref, h1_ref, xe_ref, aff_ref = refs[1 + n_y:]
    D = h_ref.shape[2]
    acc = h_ref[0]
    k0 = 0
    for y_ref in y_refs:
        if y_t:
            kk = y_ref.shape[1]
            acc = acc + lax.dot_general(y_ref[0], wo_ref[k0:k0 + kk, :], (((0,), (0,)), ((), ())),
                                        preferred_element_type=F32)
        else:
            kk = y_ref.shape[2]
            acc = acc + _dot(y_ref[0], wo_ref[k0:k0 + kk, :])
        k0 += kk
    h1_ref[0] = acc
    xn = _rms(acc, g_ref[...])
    xh = xn.astype(BF16)
    xl = (xn - xh.astype(F32)).astype(BF16)
    hh = _dot(xh, wr_ref[...])
    logits = hh[:, 0:LANE] + hh[:, LANE:2 * LANE] + _dot(xl, wr_ref[:, 0:LANE])
    lane = lax.broadcasted_iota(I32, logits.shape, 1)
    logits = jnp.where(lane < N_EXP, logits, NEG)
    e = jnp.exp(logits - jnp.max(logits, axis=1, keepdims=True))
    aff = e / jnp.sum(e, axis=1, keepdims=True)
    tm = xn.shape[0]
    xe_ref[0, :, :, 0:D] = xn.reshape(tm, 1, D)
    xe_ref[0, :, :, D:D + LANE] = aff.reshape(tm, 1, LANE)
    aff_ref[0] = aff.T[0:N_EXP, :]


def _outproj(h, ys, wo, g, wr, tm, y_t=False):
    B, Lp, D = h.shape
    row = lambda b, i: (b, i, 0)
    full = lambda b, i: (0, 0)
    y_spec = (lambda y: pl.BlockSpec((1, y.shape[1], tm), lambda b, i: (b, 0, i))) if y_t else (
        lambda y: pl.BlockSpec((1, tm, y.shape[2]), row))
    return pl.pallas_call(
        functools.partial(_outproj_body, n_y=len(ys), y_t=y_t),
        out_shape=(jax.ShapeDtypeStruct((B, Lp, D), F32),
                   jax.ShapeDtypeStruct((B, Lp, 1, D + LANE), F32),
                   jax.ShapeDtypeStruct((B, N_EXP, Lp), F32)),
        grid=(B, Lp // tm),
        in_specs=[pl.BlockSpec((1, tm, D), row)] + [y_spec(y) for y in ys]
        + [pl.BlockSpec(wo.shape, full), pl.BlockSpec((1, D), full), pl.BlockSpec(wr.shape, full)],
        out_specs=(pl.BlockSpec((1, tm, D), row), pl.BlockSpec((1, tm, 1, D + LANE), lambda b, i: (b, i, 0, 0)),
                   pl.BlockSpec((1, N_EXP, tm), lambda b, i: (b, 0, i))),
        compiler_params=_cp(("parallel", "parallel")),
        name="outproj_router",
    )(h, *ys, wo, g.reshape(1, D), wr)


def _topk_body(aff_ref, sel_ref, q_ref, lohi_ref, sel_sc, *, n_real, cap):
    Lp = aff_ref.shape[2]
    nch = Lp // LANE
    cm = n_real // LANE
    aff = aff_ref[0]
    tok = lax.broadcasted_iota(I32, aff.shape, 1)
    keys = jnp.where(tok < n_real + N_META, pltpu.bitcast(aff, I32), -1)

    def search(it, prefix):
        cand = prefix | lax.shift_left(jnp.int32(1), 30 - it)
        cnt = jnp.sum(jnp.where(keys >= cand, 1.0, 0.0), axis=1, keepdims=True)
        return jnp.where(cnt >= cap, cand, prefix)

    thr = lax.fori_loop(0, 31, search, jnp.zeros((N_EXP, 1), I32))
    need = cap - jnp.sum(jnp.where(keys > thr, 1.0, 0.0), axis=1, keepdims=True)

    ri = lax.broadcasted_iota(I32, (LANE, LANE), 0)
    ci = lax.broadcasted_iota(I32, (LANE, LANE), 1)
    upper = (ri <= ci).astype(BF16)
    er = lax.broadcasted_iota(I32, (N_EXP, N_EXP), 0)
    ec = lax.broadcasted_iota(I32, (N_EXP, N_EXP), 1)
    lower = (ec < er).astype(BF16)

    sel_sc[...] = jnp.zeros(sel_sc.shape, F32)

    def select(c, carry):
        c0 = pl.multiple_of(c * LANE, LANE)
        a = pltpu.bitcast(aff_ref[0, :, pl.ds(c0, LANE)], I32)
        t = c0 + lax.broadcasted_iota(I32, a.shape, 1)
        kc = jnp.where(t < n_real + N_META, a, -1)
        eq = (kc == thr).astype(F32)
        rank = _dot(eq.astype(BF16), upper) - eq + carry
        sel = jnp.where((kc > thr) | ((eq > 0) & (rank < need)), 1.0, 0.0)
        sel_sc[:, pl.ds(c0, LANE)] = sel
        return carry + jnp.sum(eq, axis=1, keepdims=True)

    carry = select(jnp.int32(cm), jnp.zeros((N_EXP, 1), F32))
    lax.fori_loop(0, cm, select, carry)

    def offsets(c, carry):
        off_c = carry
        c0 = pl.multiple_of(c * LANE, LANE)
        sel = sel_sc[:, pl.ds(c0, LANE)]
        selb = sel.astype(BF16)
        nt = jnp.sum(sel, axis=0, keepdims=True)
        off_incl = _dot(jnp.broadcast_to(nt, (8, LANE)).astype(BF16), upper)[0:1] + off_c
        r = _dot(lower, selb)
        sel_ref[0, :, pl.ds(c0, LANE)] = sel.astype(I32)
        q_ref[0, :, pl.ds(c0, LANE)] = (off_incl - nt + r).astype(I32)
        lohi_ref[0, 0:1, pl.ds(c0, LANE)] = (off_incl - nt).astype(I32)
        lohi_ref[0, 1:2, pl.ds(c0, LANE)] = off_incl.astype(I32)
        return off_c + jnp.sum(nt, axis=1, keepdims=True)

    lax.fori_loop(0, nch, offsets, jnp.zeros((1, 1), F32))


def _topk(aff_t, n_real, cap):
    B, E, Lp = aff_t.shape
    spec = pl.BlockSpec((1, E, Lp), lambda b: (b, 0, 0))
    return pl.pallas_call(
        functools.partial(_topk_body, n_real=n_real, cap=cap),
        out_shape=(jax.ShapeDtypeStruct((B, E, Lp), I32), jax.ShapeDtypeStruct((B, E, Lp), I32),
                   jax.ShapeDtypeStruct((B, 2, Lp), I32)),
        grid=(B,),
        in_specs=[spec],
        out_specs=(spec, spec, pl.BlockSpec((1, 2, Lp), lambda b: (b, 0, 0))),
        scratch_shapes=[pltpu.VMEM((E, Lp), F32)],
        compiler_params=_cp(("parallel",)),
        name="expert_topk",
    )(aff_t)


def _compact_body(sel_ref, idx_ref, *, n_tok, cap):
    def put(t, cnt):
        idx_ref[0, 0, 0, cnt] = t
        return cnt + sel_ref[0, 0, 0, t]

    lax.fori_loop(0, n_tok, put, jnp.int32(0), unroll=8)

    def fill(s, c):
        idx_ref[0, 0, 0, s] = 0
        return c

    lax.fori_loop(cap, idx_ref.shape[3], fill, 0)


def _compact(sel, n_tok, cap, cp):
    B, E, Lp = sel.shape
    return pl.pallas_call(
        functools.partial(_compact_body, n_tok=n_tok, cap=cap),
        out_shape=jax.ShapeDtypeStruct((B, E, 1, cp), I32),
        grid=(B, E),
        in_specs=[pl.BlockSpec((1, 1, 1, Lp), lambda b, e: (b, e, 0, 0), memory_space=pltpu.SMEM)],
        out_specs=pl.BlockSpec((1, 1, 1, cp), lambda b, e: (b, e, 0, 0), memory_space=pltpu.SMEM),
        compiler_params=_cp(("parallel", "parallel")),
        name="expert_compact",
    )(sel.reshape(B, E, 1, Lp))


def _ffn_body(idxg_ref, idxw_ref, q_ref, xe_ref, wg_ref, wu_ref, wd_ref, z_ref, xbuf, x2d_sc, xb_sc, yacc, ybuf,
              gsem, ssem, *, cap, zr, n_tiles, n_r, n_f):
    s = pl.program_id(0)
    f = pl.program_id(1)
    nf = pl.num_programs(1)
    tr, D = xb_sc.shape
    B = xe_ref.shape[0]
    total = N_EXP * cap
    per = tr // n_f

    def tile(t):
        t = jnp.clip(t, 0, n_tiles - 1)
        return t // (B * n_r), (t // n_r) % B, (t % n_r) * tr

    @pl.when(f == 0)
    def _():
        @pl.when(s == 0)
        def _():
            xb_sc[...] = jnp.zeros(xb_sc.shape, BF16)
            ybuf[...] = jnp.zeros(ybuf.shape, F32)
            for bb in range(B):
                for r0 in range(total, zr + tr, tr):
                    n = min(tr, zr + tr - r0)
                    cp = pltpu.make_async_copy(ybuf.at[0:n], z_ref.at[bb, r0:r0 + n], ssem)
                    cp.start()
                    cp.wait()

        @pl.when(s >= 1)
        def _():
            pltpu.make_async_copy(xe_ref.at[0, 0:tr], xbuf, gsem).wait()
            x2d_sc[...] = xbuf[...].reshape(x2d_sc.shape)
            xb_sc[...] = x2d_sc[:, 0:D].astype(BF16)

    x = xb_sc[...]
    g = _dot(x, wg_ref[0])
    u = _dot(x, wu_ref[0])
    hmid = (g * jax.nn.sigmoid(g) * u).astype(BF16)
    contrib = _dot(hmid, wd_ref[0])

    _, gb, g0 = tile(s)
    _, wb, w0 = tile(s - 2)
    for i in range(per):
        j = f * per + i
        t = idxg_ref[0, 0, 0, g0 + j]
        pltpu.make_async_copy(xe_ref.at[gb, t], xbuf.at[j], gsem).start()
        live = (s >= 2) & (w0 + j < cap)
        dst = jnp.where(live, q_ref[0, 0, 0, idxw_ref[0, 0, 0, w0 + j]], zr + j)
        pltpu.make_async_copy(ybuf.at[j], z_ref.at[wb, dst], ssem).start()

    @pl.when(f == 0)
    def _():
        yacc[...] = contrib

    @pl.when(f > 0)
    def _():
        yacc[...] += contrib

    @pl.when(f == nf - 1)
    def _():
        pltpu.make_async_copy(ybuf, z_ref.at[0, 0:tr], ssem).wait()

        @pl.when(s >= 1)
        def _():
            e, _, _ = tile(s - 1)
            aff = x2d_sc[:, D:D + LANE]
            lane = lax.broadcasted_iota(I32, aff.shape, 1)
            gate = jnp.sum(jnp.where(lane == e, aff, 0.0), axis=1, keepdims=True)
            ybuf[...] = (yacc[...] * gate).reshape(tr, 1, D)

        @pl.when(s == pl.num_programs(0) - 1)
        def _():
            pltpu.make_async_copy(xe_ref.at[0, 0:tr], xbuf, gsem).wait()


def _ffn(idx, q, xe, wg, wu, wd, cap, cr, zr, tf):
    B, E, _, cp = idx.shape
    Lp = q.shape[2]
    De = xe.shape[3]
    D, F = wg.shape[1], wg.shape[2]
    n_r = 3
    tr = cr // n_r
    n_f = F // tf
    n_tiles = E * B * n_r

    def at_tile(off):
        def index_map(s, f):
            t = jnp.clip(s + off, 0, n_tiles - 1)
            return ((t // n_r) % B, t // (B * n_r), 0, 0)
        return index_map

    expert = lambda s: jnp.clip(s - 1, 0, n_tiles - 1) // (B * n_r)
    smem = lambda n, off: pl.BlockSpec((1, 1, 1, n), at_tile(off), memory_space=pltpu.SMEM)
    return pl.pallas_call(
        functools.partial(_ffn_body, cap=cap, zr=zr, n_tiles=n_tiles, n_r=n_r, n_f=n_f),
        out_shape=jax.ShapeDtypeStruct((B, zr + tr, 1, D), F32),
        grid=(n_tiles + 2, n_f),
        in_specs=[smem(cp, 0), smem(cp, -2), smem(Lp, -2), pl.BlockSpec(memory_space=pl.ANY),
                  pl.BlockSpec((1, D, tf), lambda s, f: (expert(s), 0, f)),
                  pl.BlockSpec((1, D, tf), lambda s, f: (expert(s), 0, f)),
                  pl.BlockSpec((1, tf, D), lambda s, f: (expert(s), f, 0))],
        out_specs=pl.BlockSpec(memory_space=pl.ANY),
        scratch_shapes=[pltpu.VMEM((tr, 1, De), F32), pltpu.VMEM((tr, De), F32), pltpu.VMEM((tr, D), BF16),
                        pltpu.VMEM((tr, D), F32),
                        pltpu.VMEM((tr, 1, D), F32), pltpu.SemaphoreType.DMA(()), pltpu.SemaphoreType.DMA(())],
        compiler_params=_cp(("arbitrary", "arbitrary"), 56),
        name="expert_ffn",
    )(idx, idx, q.reshape(B, E, 1, Lp), xe, wg, wu, wd)


def _combine_body(k0_ref, k1_ref, h_ref, lohi_ref, g_ref, z_ref, h2_ref, u_ref, buf, zb_sc, acc_sc, sem, *, nblk):
    b = pl.program_id(0)
    j = pl.program_id(1)
    k0 = k0_ref[b * nblk + j]
    k1 = k1_ref[b * nblk + j]
    lo = lohi_ref[0, 0:1, :]
    hi = lohi_ref[0, 1:2, :]
    acc_sc[...] = jnp.zeros(acc_sc.shape, F32)

    def copy(k, slot):
        r0 = pl.multiple_of(k * ZCHUNK, ZCHUNK)
        return pltpu.make_async_copy(z_ref.at[b, pl.ds(r0, ZCHUNK)], buf.at[slot], sem.at[slot])

    @pl.when(k1 > k0)
    def _():
        copy(k0, 0).start()

    def chunk(k, c):
        slot = lax.rem(k - k0, 2)
        copy(k, slot).wait()

        @pl.when(k + 1 < k1)
        def _():
            copy(k + 1, 1 - slot).start()

        w = k * ZCHUNK + lax.broadcasted_iota(I32, (ZCHUNK, lo.shape[1]), 0)
        band_t = jnp.where((w >= lo) & (w < hi), 1.0, 0.0)
        zb_sc[...] = buf[slot].reshape(zb_sc.shape)
        acc_sc[...] += _dot(band_t.T.astype(BF16), zb_sc[...].astype(BF16))
        return c

    lax.fori_loop(k0, k1, chunk, 0)
    h2 = h_ref[0] + acc_sc[...]
    h2_ref[0] = h2
    u_ref[0] = _rms(h2, g_ref[...]).astype(u_ref.dtype)


def _combine(h1, lohi, z, g, n_rows, u_dtype):
    B, Lp, D = h1.shape
    nblk = n_rows // TOKB
    lo = lohi[:, 0, :n_rows].reshape(B, nblk, TOKB)
    hi = lohi[:, 1, :n_rows].reshape(B, nblk, TOKB)
    k0 = (lo[:, :, 0] // ZCHUNK).reshape(-1).astype(I32)
    k1 = ((hi[:, :, -1] + ZCHUNK - 1) // ZCHUNK).reshape(-1).astype(I32)
    k1 = jnp.where(hi[:, :, -1].reshape(-1) > lo[:, :, 0].reshape(-1), k1, k0)
    row = lambda b, j, *_: (b, j, 0)
    grid_spec = pltpu.PrefetchScalarGridSpec(
        num_scalar_prefetch=2,
        grid=(B, nblk),
        in_specs=[pl.BlockSpec((1, TOKB, D), row),
                  pl.BlockSpec((1, 2, TOKB), lambda b, j, *_: (b, 0, j)),
                  pl.BlockSpec((1, D), lambda b, j, *_: (0, 0)),
                  pl.BlockSpec(memory_space=pl.ANY)],
        out_specs=(pl.BlockSpec((1, TOKB, D), row), pl.BlockSpec((1, TOKB, D), row)),
        scratch_shapes=[pltpu.VMEM((2, ZCHUNK, 1, D), z.dtype), pltpu.VMEM((ZCHUNK, D), F32),
                        pltpu.VMEM((TOKB, D), F32), pltpu.SemaphoreType.DMA((2,))],
    )
    return pl.pallas_call(
        functools.partial(_combine_body, nblk=nblk),
        out_shape=(jax.ShapeDtypeStruct((B, n_rows, D), F32), jax.ShapeDtypeStruct((B, n_rows, D), u_dtype)),
        grid_spec=grid_spec,
        compiler_params=_cp(("arbitrary", "arbitrary")),
        name="moe_combine",
    )(k0, k1, h1, lohi, g.reshape(1, D), z)


def _moe(h1, xe, aff_t, wg, wu, wd, g_next, n_real, n_rows_out, u_dtype):
    B, Lp, D = h1.shape
    cap = (CAP_F * (n_real + N_META)) // N_EXP
    cr = -(-cap // 48) * 48
    cp = -(-(cap + 1) // LANE) * LANE
    zr = -(-(N_EXP * cap) // ZCHUNK) * ZCHUNK
    if zr == N_EXP * cap:
        zr += ZCHUNK
    sel, q, lohi = _topk(aff_t, n_real, cap)
    idx = _compact(sel, n_real + N_META, cap, cp)
    z = _ffn(idx, q, xe, wg, wu, wd, cap, cr, zr, min(wg.shape[2], 512))
    return _combine(h1, lohi, z, g_next, n_rows_out, u_dtype)


def _rope_tables(n_real, Lp):
    pos = jnp.concatenate([jnp.arange(n_real) + N_META, jnp.arange(N_META), jnp.zeros((Lp - n_real - N_META,), I32)])
    inv_freq = 1.0 / (ROPE_THETA ** (jnp.arange(0, MLA_ROPE, 2, dtype=F32) / MLA_ROPE))
    ang = pos.astype(F32)[:, None] * inv_freq[None, :]
    cos, sin = jnp.cos(ang), jnp.sin(ang)
    half = MLA_ROPE // 2
    z = lambda n: jnp.zeros((Lp, n), F32)
    c = jnp.concatenate([cos, cos, z(LANE - 2 * half)], axis=1)
    sn = jnp.concatenate([-sin, z(LANE - half)], axis=1)
    sp = jnp.concatenate([z(half), sin, z(LANE - 2 * half)], axis=1)
    return (c, sn, sp), (cos.T, sin.T)


def kernel(x, meta_tokens, norm_mix, norm_ffn, norm_final, ab_w_in, ab_q_norm, ab_kv_norm, ab_w_uq, ab_w_ukv, ab_rpb, ab_w_out, c_w_in, c_sinks, c_w_out, ec_w_router, ec_w_gate, ec_w_up, ec_w_down):
    B, S, D = x.shape
    depth = norm_mix.shape[0]
    Lp = S + TAIL
    tm = 512
    meta = jnp.broadcast_to(meta_tokens[None].astype(x.dtype), (B, N_META, D))
    h = jnp.concatenate([x, meta, jnp.zeros((B, TAIL - N_META, D), x.dtype)], axis=1)
    u = _norm(h, norm_mix[0], tm)
    tabs, tabs_t = _rope_tables(S, Lp)

    for layer in range(depth):
        if layer % 2 == 0:
            e = layer // 2
            w_in = ab_w_in[e].astype(BF16)
            o2 = MLA_QL + MLA_KVL
            o3 = o2 + MLA_ROPE
            lat = _mm(u, w_in[:, :o2], tm, o2)
            krope = _mm(u, jnp.pad(w_in[:, o2:o3], ((0, 0), (0, LANE - MLA_ROPE))), tm, LANE)
            na_qkv = _mm(u, w_in[:, o3:], tm, 1024)
            wq = jnp.pad(ab_w_uq[e].reshape(MLA_QL, MLA_H, MLA_NOPE + MLA_ROPE),
                         ((0, 0), (0, 0), (0, MLA_HP - MLA_NOPE - MLA_ROPE))).reshape(MLA_QL, MLA_H * MLA_HP)
            wkv = ab_w_ukv[e].reshape(MLA_KVL, MLA_H, MLA_NOPE + MLA_V)
            wk = wkv[:, :, :MLA_NOPE].reshape(MLA_KVL, -1)
            wv = wkv[:, :, MLA_NOPE:].reshape(MLA_KVL, -1)
            qt, k, vt = _mla_up(lat, krope, ab_q_norm[e], ab_kv_norm[e], wq.T.astype(BF16), wk.astype(BF16),
                                wv.T.astype(BF16), tabs, tabs_t, tm)
            y_mla = _mla_attention(qt, k, vt, S, TAIL, min(S, 512))
            y_na = _na_attention(na_qkv, _na_bias_table(ab_rpb[e]), S)
            ys = [y_mla, y_na]
            w_out = ab_w_out[e]
        else:
            o = layer // 2
            qw, kw = SWA_H * SWA_D, SWA_KV * SWA_D
            w_in = c_w_in[o]
            wq = w_in[:, :qw] * (SWA_D ** -0.5 * math.log2(math.e))
            wqv_t = jnp.concatenate([wq, w_in[:, qw + kw:]], axis=1).T.astype(BF16)
            qvt = _mm_t(u, wqv_t, tm, (qw + kw) // 3)
            k = _mm(u, w_in[:, qw:qw + kw].astype(BF16), tm, kw)
            ys = [_swa_attention(qvt, k, c_sinks[o], S)]
            w_out = c_w_out[o]
        wr = jnp.pad(ec_w_router[layer], ((0, 0), (0, LANE - N_EXP)))
        wr_hi = wr.astype(BF16)
        wr = jnp.concatenate([wr_hi, (wr - wr_hi.astype(F32)).astype(BF16)], axis=1)
        h1, xe, aff_t = _outproj(h, ys, w_out.astype(BF16), norm_ffn[layer], wr, 256, y_t=layer % 2 == 1)
        last = layer == depth - 1
        g_next = norm_final if last else norm_mix[layer + 1]
        h, u = _moe(h1, xe, aff_t, ec_w_gate[layer].astype(BF16), ec_w_up[layer].astype(BF16),
                    ec_w_down[layer].astype(BF16), g_next, S, S if last else Lp, F32 if last else BF16)
    return u
```

```python
import functools
import math

import jax
import jax.numpy as jnp
from jax import lax
from jax.experimental import pallas as pl
from jax.experimental.pallas import tpu as pltpu

F32 = jnp.float32
BF16 = jnp.bfloat16
I32 = jnp.int32

N_META = 16
GRID_W = 64
QBLK = 128
EPS = 1e-6
NEG = -1e30

MLA_H = 8
MLA_NOPE = 128
MLA_ROPE = 64
MLA_V = 128
MLA_QL = 512
MLA_KVL = 512
MLA_HP = 256
ROPE_THETA = 10000.0

NA_H = 8
NA_D = 128
NA_KH = 8
NA_KW = 16

SWA_H = 32
SWA_KV = 4
SWA_G = SWA_H // SWA_KV
SWA_D = 64
SWA_WIN = 128

N_EXP = 16
CAP_F = 2

TAIL = 512
LANE = 128
ZCHUNK = 256
TOKB = 256


def _cp(sem, vmem_mb=48):
    return pltpu.CompilerParams(dimension_semantics=sem, vmem_limit_bytes=vmem_mb << 20)


def _dot(a, b):
    return jnp.dot(a, b, preferred_element_type=F32)


def _dot_nt(a, b):
    return lax.dot_general(a, b, (((1,), (1,)), ((), ())), preferred_element_type=F32)


def _rms(x, g):
    return x * lax.rsqrt(jnp.mean(x * x, axis=-1, keepdims=True) + EPS) * g


def _norm_body(x_ref, g_ref, o_ref):
    o_ref[0] = _rms(x_ref[0], g_ref[...]).astype(o_ref.dtype)


def _norm(h, g, tm):
    B, Lp, D = h.shape
    return pl.pallas_call(
        _norm_body,
        out_shape=jax.ShapeDtypeStruct((B, Lp, D), BF16),
        grid=(B, Lp // tm),
        in_specs=[pl.BlockSpec((1, tm, D), lambda b, i: (b, i, 0)), pl.BlockSpec((1, D), lambda b, i: (0, 0))],
        out_specs=pl.BlockSpec((1, tm, D), lambda b, i: (b, i, 0)),
        compiler_params=_cp(("parallel", "parallel")),
        name="rmsnorm",
    )(h, g.reshape(1, D))


def _mm_body(x_ref, w_ref, o_ref):
    o_ref[0] = _dot(x_ref[0], w_ref[...]).astype(o_ref.dtype)


def _mm(x, w, tm, tn):
    B, Lp, K = x.shape
    N = w.shape[1]
    return pl.pallas_call(
        _mm_body,
        out_shape=jax.ShapeDtypeStruct((B, Lp, N), BF16),
        grid=(B, Lp // tm, N // tn),
        in_specs=[pl.BlockSpec((1, tm, K), lambda b, i, j: (b, i, 0)), pl.BlockSpec((K, tn), lambda b, i, j: (0, j))],
        out_specs=pl.BlockSpec((1, tm, tn), lambda b, i, j: (b, i, j)),
        compiler_params=_cp(("parallel", "parallel", "parallel")),
        name="in_proj",
    )(x, w)


def _mm_t_body(x_ref, wt_ref, o_ref):
    o_ref[0] = _dot_nt(wt_ref[...], x_ref[0]).astype(o_ref.dtype)


def _mm_t(x, wt, tm, tn):
    B, Lp, K = x.shape
    N = wt.shape[0]
    return pl.pallas_call(
        _mm_t_body,
        out_shape=jax.ShapeDtypeStruct((B, N, Lp), BF16),
        grid=(B, Lp // tm, N // tn),
        in_specs=[pl.BlockSpec((1, tm, K), lambda b, i, j: (b, i, 0)), pl.BlockSpec((tn, K), lambda b, i, j: (j, 0))],
        out_specs=pl.BlockSpec((1, tn, tm), lambda b, i, j: (b, j, i)),
        compiler_params=_cp(("parallel", "parallel", "parallel")),
        name="in_proj_t",
    )(x, wt)


def _rope(pe, c, sn, sp):
    return pe * c + pltpu.roll(pe, 96, 1) * sn + pltpu.roll(pe, 32, 1) * sp


def _uq_body(lat_ref, g_ref, wt_ref, c_ref, s_ref, o_ref, *, scale):
    xn = _rms(lat_ref[0].astype(F32), g_ref[...]).astype(BF16)
    zt = _dot_nt(wt_ref[...], xn)
    c, s = c_ref[...], s_ref[...]
    half = MLA_ROPE // 2
    dt = o_ref.dtype
    for h in range(MLA_H):
        o = h * MLA_HP
        p1, p2, p3 = o + MLA_NOPE, o + MLA_NOPE + half, o + MLA_NOPE + MLA_ROPE
        x1, x2 = zt[p1:p2], zt[p2:p3]
        o_ref[0, o:p1, :] = (zt[o:p1] * scale).astype(dt)
        o_ref[0, p1:p2, :] = ((x1 * c - x2 * s) * scale).astype(dt)
        o_ref[0, p2:p3, :] = ((x2 * c + x1 * s) * scale).astype(dt)
        o_ref[0, p3:o + MLA_HP, :] = jnp.zeros((o + MLA_HP - p3, zt.shape[1]), dt)


def _ukv_body(lat_ref, g_ref, wk_ref, wvt_ref, kr_ref, c_ref, sn_ref, sp_ref, k_ref, vt_ref):
    xn = _rms(lat_ref[0].astype(F32), g_ref[...]).astype(BF16)
    zk = _dot(xn, wk_ref[...])
    kr = _rope(kr_ref[0].astype(F32), c_ref[...], sn_ref[...], sp_ref[...]).astype(k_ref.dtype)
    for h in range(MLA_H):
        k_ref[0, :, h * MLA_HP:h * MLA_HP + LANE] = zk[:, h * LANE:(h + 1) * LANE].astype(k_ref.dtype)
        k_ref[0, :, h * MLA_HP + LANE:(h + 1) * MLA_HP] = kr
    vt_ref[0] = _dot_nt(wvt_ref[...], xn).astype(vt_ref.dtype)


def _mla_up(lat, krope, qn, kvn, wqt, wk, wvt, tabs, tabs_t, tm):
    B, Lp, _ = lat.shape
    c, sn, sp = tabs
    ct, st = tabs_t
    tab_spec = pl.BlockSpec((tm, LANE), lambda b, i: (i, 0))
    tabt_spec = pl.BlockSpec((MLA_ROPE // 2, tm), lambda b, i: (0, i))
    scale = (MLA_NOPE + MLA_ROPE) ** -0.5 * math.log2(math.e)
    qt = pl.pallas_call(
        functools.partial(_uq_body, scale=scale),
        out_shape=jax.ShapeDtypeStruct((B, MLA_H * MLA_HP, Lp), BF16),
        grid=(B, Lp // tm),
        in_specs=[pl.BlockSpec((1, tm, MLA_QL), lambda b, i: (b, i, 0)),
                  pl.BlockSpec((1, MLA_QL), lambda b, i: (0, 0)),
                  pl.BlockSpec(wqt.shape, lambda b, i: (0, 0)),
                  tabt_spec, tabt_spec],
        out_specs=pl.BlockSpec((1, MLA_H * MLA_HP, tm), lambda b, i: (b, 0, i)),
        compiler_params=_cp(("parallel", "parallel")),
        name="mla_q_up",
    )(lat, qn.reshape(1, -1), wqt, ct, st)
    k, vt = pl.pallas_call(
        _ukv_body,
        out_shape=(jax.ShapeDtypeStruct((B, Lp, MLA_H * MLA_HP), BF16),
                   jax.ShapeDtypeStruct((B, MLA_H * MLA_V, Lp), BF16)),
        grid=(B, Lp // tm),
        in_specs=[pl.BlockSpec((1, tm, MLA_KVL), lambda b, i: (b, i, 1)),
                  pl.BlockSpec((1, MLA_KVL), lambda b, i: (0, 0)),
                  pl.BlockSpec(wk.shape, lambda b, i: (0, 0)),
                  pl.BlockSpec(wvt.shape, lambda b, i: (0, 0)),
                  pl.BlockSpec((1, tm, LANE), lambda b, i: (b, i, 0)),
                  tab_spec, tab_spec, tab_spec],
        out_specs=(pl.BlockSpec((1, tm, MLA_H * MLA_HP), lambda b, i: (b, i, 0)),
                   pl.BlockSpec((1, MLA_H * MLA_V, tm), lambda b, i: (b, 0, i))),
        compiler_params=_cp(("parallel", "parallel")),
        name="mla_kv_up",
    )(lat, kvn.reshape(1, -1), wk, wvt, krope, c, sn, sp)
    return qt, k, vt


def _mla_body(qt_ref, k_ref, vt_ref, o_ref, m_sc, acc_sc, s0_sc, s1_sc, *, n_real, tkc):
    qt = qt_ref[0]
    tq = qt.shape[1]
    nch = n_real // tkc
    m_sc[...] = jnp.full(m_sc.shape, NEG, F32)
    acc_sc[...] = jnp.zeros(acc_sc.shape, F32)

    def scores(c):
        r0 = pl.multiple_of(jnp.minimum(c, nch - 1) * tkc, tkc)
        return _dot(k_ref[0, pl.ds(r0, tkc), :], qt)

    def consume(st, vtc):
        m_prev = m_sc[...]
        m_new = jnp.maximum(m_prev, jnp.max(st, axis=0, keepdims=True))
        p = jnp.exp2((st - m_new).astype(BF16))
        alpha = jnp.exp2(m_prev - m_new)
        vte = jnp.concatenate([vtc, jnp.ones((16, vtc.shape[1]), BF16)], axis=0)
        acc_sc[...] = alpha * acc_sc[...] + _dot(vte, p)
        m_sc[...] = m_new

    def values(c):
        return vt_ref[0, :, pl.ds(pl.multiple_of(c * tkc, tkc), tkc)]

    s0_sc[...] = scores(0)

    pairs = 2 if nch % 4 == 0 else 1

    def body(cb, carry):
        for u in range(pairs):
            c = 2 * (pairs * cb + u)
            s1_sc[...] = scores(c + 1)
            consume(s0_sc[...], values(c))
            s0_sc[...] = scores(c + 2)
            consume(s1_sc[...], values(c + 1))
        return carry

    lax.fori_loop(0, nch // (2 * pairs), body, 0)
    consume(_dot(k_ref[0, n_real:n_real + N_META, :], qt), vt_ref[0, :, n_real:n_real + N_META])
    out = (acc_sc[0:MLA_V, :] / acc_sc[MLA_V:MLA_V + 1, :]).T
    i = pl.program_id(2)
    row = i * tq + lax.broadcasted_iota(I32, (tq, 1), 0)
    o_ref[0] = jnp.where(row < n_real + N_META, out, 0.0).astype(o_ref.dtype)


def _mla_attention(qt, k, vt, n_real, tq, tkc):
    B, Lp, _ = k.shape
    return pl.pallas_call(
        functools.partial(_mla_body, n_real=n_real, tkc=tkc),
        out_shape=jax.ShapeDtypeStruct((B, Lp, MLA_H * MLA_V), BF16),
        grid=(B, MLA_H, Lp // tq),
        in_specs=[pl.BlockSpec((1, MLA_HP, tq), lambda b, h, i: (b, h, i)),
                  pl.BlockSpec((1, Lp, MLA_HP), lambda b, h, i: (b, 0, h)),
                  pl.BlockSpec((1, MLA_V, Lp), lambda b, h, i: (b, h, 0))],
        out_specs=pl.BlockSpec((1, tq, MLA_V), lambda b, h, i: (b, i, h)),
        scratch_shapes=[pltpu.VMEM((1, tq), F32), pltpu.VMEM((MLA_V + 16, tq), F32),
                        pltpu.VMEM((tkc, tq), F32), pltpu.VMEM((tkc, tq), F32)],
        compiler_params=_cp(("parallel", "parallel", "arbitrary")),
        name="mla_attention",
    )(qt, k, vt)


def _na_bias_table(rpb):
    col = jnp.arange(GRID_W)
    cs = jnp.clip(col - NA_KW // 2, 0, GRID_W - NA_KW)
    ok = (col[None, :] >= cs[:, None]) & (col[None, :] < cs[:, None] + NA_KW)
    dc = jnp.clip(col[None, :] - col[:, None], -(NA_KW - 1), NA_KW - 1) + (NA_KW - 1)
    d = jnp.arange(NA_KH)[:, None] + jnp.arange(NA_KH)[None, :]
    t = rpb[:, d][:, :, :, dc]
    t = jnp.where(ok[None, None, None], t, NEG)
    return t.transpose(0, 1, 3, 2, 4).reshape(rpb.shape[0], NA_KH, GRID_W, NA_KH * GRID_W).astype(F32)


def _na_body(q_ref, k_ref, v_ref, t_ref, o_ref, s_sc, sm_sc, p_sc, pm_sc, l_sc, *, n_real, rpg):
    i = pl.program_id(2)
    rows = n_real // GRID_W
    scale = NA_D ** -0.5
    km = k_ref[0, n_real:n_real + N_META, :]
    vm = v_ref[0, n_real:n_real + N_META, :]

    @pl.when(i < rows // rpg)
    def _():
        def window(rr):
            r = i * rpg + rr
            rs = jnp.clip(r - NA_KH // 2, 0, rows - NA_KH)
            return rs - r + (NA_KH - 1), pl.multiple_of(rs * GRID_W, GRID_W)

        for rr in range(rpg):
            d0, k0 = window(rr)
            qr = q_ref[0, rr * GRID_W:(rr + 1) * GRID_W, :]
            s_sc[rr] = _dot_nt(qr, k_ref[0, pl.ds(k0, NA_KH * GRID_W), :]) * scale + t_ref[0, d0]
            sm_sc[rr] = _dot_nt(qr, km) * scale
        for rr in range(rpg):
            s, sm = s_sc[rr], sm_sc[rr]
            m = jnp.maximum(jnp.max(s, axis=1, keepdims=True), jnp.max(sm, axis=1, keepdims=True))
            p = jnp.exp(s - m)
            pm = jnp.exp(sm - m)
            l_sc[rr] = jnp.sum(p, axis=1, keepdims=True) + jnp.sum(pm, axis=1, keepdims=True)
            p_sc[rr] = p.astype(BF16)
            pm_sc[rr] = pm.astype(BF16)
        for rr in range(rpg):
            _, k0 = window(rr)
            o = (_dot(p_sc[rr], v_ref[0, pl.ds(k0, NA_KH * GRID_W), :]) + _dot(pm_sc[rr], vm)) / l_sc[rr]
            o_ref[0, rr * GRID_W:(rr + 1) * GRID_W, :] = o.astype(o_ref.dtype)

    @pl.when(i >= rows // rpg)
    def _():
        qm = q_ref[0, 0:N_META, :]
        sm = _dot_nt(qm, km) * scale
        pm = jnp.exp(sm - jnp.max(sm, axis=1, keepdims=True))
        o = _dot(pm.astype(BF16), vm) / jnp.sum(pm, axis=1, keepdims=True)
        o_ref[0] = jnp.zeros(o_ref.shape[1:], o_ref.dtype)
        o_ref[0, 0:N_META, :] = o.astype(o_ref.dtype)


def _na_attention(qkv, table, n_real):
    B, Lp, _ = qkv.shape
    rpg = TAIL // GRID_W
    tq = rpg * GRID_W
    return pl.pallas_call(
        functools.partial(_na_body, n_real=n_real, rpg=rpg),
        out_shape=jax.ShapeDtypeStruct((B, Lp, NA_H * NA_D), BF16),
        grid=(B, NA_H, Lp // tq),
        in_specs=[pl.BlockSpec((1, tq, NA_D), lambda b, h, i: (b, i, h)),
                  pl.BlockSpec((1, Lp, NA_D), lambda b, h, i: (b, 0, NA_H + h)),
                  pl.BlockSpec((1, Lp, NA_D), lambda b, h, i: (b, 0, 2 * NA_H + h)),
                  pl.BlockSpec((1,) + table.shape[1:], lambda b, h, i: (h, 0, 0, 0))],
        out_specs=pl.BlockSpec((1, tq, NA_D), lambda b, h, i: (b, i, h)),
        scratch_shapes=[pltpu.VMEM((rpg, GRID_W, NA_KH * GRID_W), F32), pltpu.VMEM((rpg, GRID_W, N_META), F32),
                        pltpu.VMEM((rpg, GRID_W, NA_KH * GRID_W), BF16), pltpu.VMEM((rpg, GRID_W, N_META), BF16),
                        pltpu.VMEM((rpg, GRID_W, 1), F32)],
        compiler_params=_cp(("parallel", "parallel", "arbitrary")),
        name="na_attention",
    )(qkv, qkv, qkv, table)


def _swa_bias_tables():
    a = jnp.arange(QBLK)[None, :]
    j = jnp.arange(3 * QBLK)[:, None]
    dist = jnp.abs(j - QBLK - a)
    slopes = 2.0 ** (-8.0 * jnp.arange(1, SWA_H + 1, dtype=F32) / SWA_H)
    real = jnp.where(dist <= SWA_WIN, -slopes[:, None, None] * dist.astype(F32) * math.log2(math.e), NEG)
    meta = jnp.where(N_META + jnp.arange(QBLK)[:, None] - a <= SWA_WIN, 0.0, NEG)
    return real.astype(F32), meta.astype(F32)


def _swa_body(sink_ref, qt_ref, kp_ref, kc_ref, kn_ref, km_ref, vp_ref, vc_ref, vn_ref, vm_ref, tr_ref, tm_ref, o_ref,
              *, n_real):
    i = pl.program_id(1)
    nb = n_real // QBLK
    log2e = math.log2(math.e)
    gw = SWA_G * SWA_D
    kmeta = km_ref[0, 0:N_META, :]
    vmeta_t = vm_ref[0, :, 0:N_META]

    def attend(kall, vall_t, bias_of):
        for kv in range(SWA_KV):
            tile, half = kv // 2, kv % 2
            q64 = jnp.concatenate([qt_ref[0, kv * gw + g * SWA_D:kv * gw + (g + 1) * SWA_D, :] for g in range(SWA_G)],
                                  axis=1)
            z64 = jnp.zeros_like(q64)
            qpad = jnp.concatenate([q64, z64] if half == 0 else [z64, q64], axis=0)
            st = _dot(kall[:, tile * LANE:(tile + 1) * LANE], qpad)
            sm = _dot(kmeta[:, tile * LANE:(tile + 1) * LANE], qpad)
            ps, ls = [], []
            for g in range(SWA_G):
                h = kv * SWA_G + g
                cols = slice(g * QBLK, (g + 1) * QBLK)
                s = st[:, cols] + bias_of(h)
                smg = sm[:, cols]
                sink = sink_ref[h] * log2e
                m = jnp.maximum(jnp.maximum(jnp.max(s, axis=0, keepdims=True), jnp.max(smg, axis=0, keepdims=True)),
                                sink)
                p = jnp.exp2(s - m)
                pm = jnp.exp2(smg - m)
                ls.append(jnp.sum(p, axis=0, keepdims=True) + jnp.sum(pm, axis=0, keepdims=True) + jnp.exp2(sink - m))
                ps.append((p.astype(BF16), pm.astype(BF16)))
            p_all = jnp.concatenate([p for p, _ in ps], axis=1)
            pm_all = jnp.concatenate([pm for _, pm in ps], axis=1)
            rows = slice(kv * SWA_D, (kv + 1) * SWA_D)
            ot = (_dot(vall_t[rows, :], p_all) + _dot(vmeta_t[rows, :], pm_all)) / jnp.concatenate(ls, axis=1)
            for g in range(SWA_G):
                o_ref[0, kv * gw + g * SWA_D:kv * gw + (g + 1) * SWA_D, :] = ot[:, g * QBLK:(g + 1) * QBLK].astype(
                    o_ref.dtype)

    @pl.when(i < nb)
    def _():
        kall = jnp.concatenate([kp_ref[0], kc_ref[0], kn_ref[0]], axis=0)
        vall_t = jnp.concatenate([vp_ref[0], vc_ref[0], vn_ref[0]], axis=1)
        row = lax.broadcasted_iota(I32, (3 * QBLK, QBLK), 0)
        pen = jnp.where(((i == 0) & (row < QBLK)) | ((i == nb - 1) & (row >= 2 * QBLK)), NEG, 0.0)
        attend(kall, vall_t, lambda h: tr_ref[h] + pen)

    @pl.when(i == nb)
    def _():
        attend(kn_ref[0], vn_ref[0], lambda h: tm_ref[...])
        lane = lax.broadcasted_iota(I32, o_ref.shape[1:], 1)
        o_ref[0] = jnp.where(lane < N_META, o_ref[0], jnp.zeros(o_ref.shape[1:], o_ref.dtype))

    @pl.when(i > nb)
    def _():
        o_ref[0] = jnp.zeros(o_ref.shape[1:], o_ref.dtype)


def _swa_attention(qvt, k, sinks, n_real):
    B, Lp, kw = k.shape
    nb = n_real // QBLK
    qw = SWA_H * SWA_D
    vrow = qw // kw
    prev = lambda i: jnp.clip(i - 1, 0, nb - 1)
    cur = lambda i: jnp.minimum(i, nb - 1)
    nxt = lambda i: jnp.where(i >= nb, 0, jnp.minimum(i + 1, nb - 1))
    met = lambda i: nb
    kspec = lambda f: pl.BlockSpec((1, QBLK, kw), lambda b, i: (b, f(i), 0))
    vspec = lambda f: pl.BlockSpec((1, kw, QBLK), lambda b, i: (b, vrow, f(i)))
    t_real, t_meta = _swa_bias_tables()
    return pl.pallas_call(
        functools.partial(_swa_body, n_real=n_real),
        out_shape=jax.ShapeDtypeStruct((B, qw, Lp), BF16),
        grid=(B, Lp // QBLK),
        in_specs=[pl.BlockSpec(memory_space=pltpu.SMEM),
                  pl.BlockSpec((1, qw, QBLK), lambda b, i: (b, 0, i)),
                  kspec(prev), kspec(cur), kspec(nxt), kspec(met),
                  vspec(prev), vspec(cur), vspec(nxt), vspec(met),
                  pl.BlockSpec(t_real.shape, lambda b, i: (0, 0, 0)),
                  pl.BlockSpec(t_meta.shape, lambda b, i: (0, 0))],
        out_specs=pl.BlockSpec((1, qw, QBLK), lambda b, i: (b, 0, i)),
        compiler_params=_cp(("parallel", "arbitrary")),
        name="swa_attention",
    )(sinks, qvt, k, k, k, k, qvt, qvt, qvt, qvt, t_real, t_meta)


def _outproj_body(*refs, n_y, y_t):
    h_ref = refs[0]
    y_refs = refs[1:1 + n_y]
    wo_ref, g_ref, wr_ref, h1_ref, xe_ref, aff_ref = refs[1 + n_y:]
    D = h_ref.shape[2]
    acc = h_ref[0]
    k0 = 0
    for y_ref in y_refs:
        if y_t:
            kk = y_ref.shape[1]
            acc = acc + lax.dot_general(y_ref[0], wo_ref[k0:k0 + kk, :], (((0,), (0,)), ((), ())),
                                        preferred_element_type=F32)
        else:
            kk = y_ref.shape[2]
            acc = acc + _dot(y_ref[0], wo_ref[k0:k0 + kk, :])
        k0 += kk
    h1_ref[0] = acc
    xn = _rms(acc, g_ref[...])
    xh = xn.astype(BF16)
    xl = (xn - xh.astype(F32)).astype(BF16)
    hh = _dot(xh, wr_ref[...])
    logits = hh[:, 0:LANE] + hh[:, LANE:2 * LANE] + _dot(xl, wr_ref[:, 0:LANE])
    lane = lax.broadcasted_iota(I32, logits.shape, 1)
    logits = jnp.where(lane < N_EXP, logits, NEG)
    e = jnp.exp(logits - jnp.max(logits, axis=1, keepdims=True))
    aff = e / jnp.sum(e, axis=1, keepdims=True)
    tm = xn.shape[0]
    xe_ref[0, :, :, 0:D] = xn.reshape(tm, 1, D)
    xe_ref[0, :, :, D:D + LANE] = aff.reshape(tm, 1, LANE)
    aff_ref[0] = aff.T[0:N_EXP, :]


def _outproj(h, ys, wo, g, wr, tm, y_t=False):
    B, Lp, D = h.shape
    row = lambda b, i: (b, i, 0)
    full = lambda b, i: (0, 0)
    y_spec = (lambda y: pl.BlockSpec((1, y.shape[1], tm), lambda b, i: (b, 0, i))) if y_t else (
        lambda y: pl.BlockSpec((1, tm, y.shape[2]), row))
    return pl.pallas_call(
        functools.partial(_outproj_body, n_y=len(ys), y_t=y_t),
        out_shape=(jax.ShapeDtypeStruct((B, Lp, D), F32),
                   jax.ShapeDtypeStruct((B, Lp, 1, D + LANE), F32),
                   jax.ShapeDtypeStruct((B, N_EXP, Lp), F32)),
        grid=(B, Lp // tm),
        in_specs=[pl.BlockSpec((1, tm, D), row)] + [y_spec(y) for y in ys]
        + [pl.BlockSpec(wo.shape, full), pl.BlockSpec((1, D), full), pl.BlockSpec(wr.shape, full)],
        out_specs=(pl.BlockSpec((1, tm, D), row), pl.BlockSpec((1, tm, 1, D + LANE), lambda b, i: (b, i, 0, 0)),
                   pl.BlockSpec((1, N_EXP, tm), lambda b, i: (b, 0, i))),
        compiler_params=_cp(("parallel", "parallel")),
        name="outproj_router",
    )(h, *ys, wo, g.reshape(1, D), wr)


def _topk_body(aff_ref, idx_ref, q_ref, lohi_ref, sel_sc, chunk_sc, *, n_real, cap):
    Lp = aff_ref.shape[2]
    nch = Lp // LANE
    cm = n_real // LANE
    aff = aff_ref[0]
    tok = lax.broadcasted_iota(I32, aff.shape, 1)
    keys = jnp.where(tok < n_real + N_META, pltpu.bitcast(aff, I32), -1)

    def search(it, prefix):
        cand = prefix | lax.shift_left(jnp.int32(1), 30 - it)
        cnt = jnp.sum(jnp.where(keys >= cand, 1.0, 0.0), axis=1, keepdims=True)
        return jnp.where(cnt >= cap, cand, prefix)

    thr = lax.fori_loop(0, 31, search, jnp.zeros((N_EXP, 1), I32))
    need = cap - jnp.sum(jnp.where(keys > thr, 1.0, 0.0), axis=1, keepdims=True)

    ri = lax.broadcasted_iota(I32, (LANE, LANE), 0)
    ci = lax.broadcasted_iota(I32, (LANE, LANE), 1)
    upper = (ri <= ci).astype(BF16)
    er = lax.broadcasted_iota(I32, (N_EXP, N_EXP), 0)
    ec = lax.broadcasted_iota(I32, (N_EXP, N_EXP), 1)
    lower = (ec < er).astype(BF16)

    sel_sc[...] = jnp.zeros(sel_sc.shape, F32)

    def select(c, carry):
        c0 = pl.multiple_of(c * LANE, LANE)
        a = pltpu.bitcast(aff_ref[0, :, pl.ds(c0, LANE)], I32)
        t = c0 + lax.broadcasted_iota(I32, a.shape, 1)
        kc = jnp.where(t < n_real + N_META, a, -1)
        eq = (kc == thr).astype(F32)
        rank = _dot(eq.astype(BF16), upper) - eq + carry
        sel = jnp.where((kc > thr) | ((eq > 0) & (rank < need)), 1.0, 0.0)
        sel_sc[:, pl.ds(c0, LANE)] = sel
        return carry + jnp.sum(eq, axis=1, keepdims=True)

    carry = select(jnp.int32(cm), jnp.zeros((N_EXP, 1), F32))
    lax.fori_loop(0, cm, select, carry)

    chunk_sc[...] = jnp.zeros(chunk_sc.shape, F32)

    def offsets(c, carry):
        off_c = carry
        c0 = pl.multiple_of(c * LANE, LANE)
        sel = sel_sc[:, pl.ds(c0, LANE)]
        selb = sel.astype(BF16)
        nt = jnp.sum(sel, axis=0, keepdims=True)
        off_incl = _dot(jnp.broadcast_to(nt, (8, LANE)).astype(BF16), upper)[0:1] + off_c
        r = _dot(lower, selb)
        q_ref[0, :, pl.ds(c0, LANE)] = (off_incl - nt + r).astype(I32)
        lohi_ref[0, 0:1, pl.ds(c0, LANE)] = (off_incl - nt).astype(I32)
        lohi_ref[0, 1:2, pl.ds(c0, LANE)] = off_incl.astype(I32)
        within = _dot(selb, upper)
        for e in range(N_EXP):
            chunk_sc[e, pl.ds(c, 1), :] = within[e:e + 1, :]
        return off_c + jnp.sum(nt, axis=1, keepdims=True)

    lax.fori_loop(0, nch, offsets, jnp.zeros((1, 1), F32))

    ncp = chunk_sc.shape[1]
    cpad = idx_ref.shape[2]
    cr_ = lax.broadcasted_iota(I32, (ncp, ncp), 0)
    cc_ = lax.broadcasted_iota(I32, (ncp, ncp), 1)
    lower_incl = (cc_ <= cr_).astype(BF16)
    slot = lax.broadcasted_iota(I32, (1, cpad), 1).astype(F32)
    chunk_id = lax.broadcasted_iota(I32, (ncp, cpad), 0).astype(F32)
    for e in range(N_EXP):
        within = chunk_sc[e]
        tot = jnp.broadcast_to(within[:, LANE - 1:LANE], (ncp, LANE))
        cend = _dot(lower_incl, tot.astype(BF16))
        cstart = cend - tot
        cend_t = jnp.concatenate([cend] * (cpad // LANE), axis=1)
        cstart_t = jnp.concatenate([cstart] * (cpad // LANE), axis=1)
        cstar = jnp.sum(jnp.where(cend_t <= slot, 1.0, 0.0), axis=0, keepdims=True)
        onehot = chunk_id == cstar
        srel = slot - jnp.sum(jnp.where(onehot, cstart_t, 0.0), axis=0, keepdims=True)
        g = lax.dot_general(within.astype(BF16), jnp.where(onehot, 1.0, 0.0).astype(BF16), (((0,), (0,)), ((), ())),
                            preferred_element_type=F32)
        lane_in = jnp.sum(jnp.where(g <= srel, 1.0, 0.0), axis=0, keepdims=True)
        idx = jnp.where(slot < cap, cstar * LANE + lane_in, 0.0)
        idx_ref[0, e:e + 1, :] = idx.astype(I32)


def _topk(aff_t, n_real, cap, cp):
    B, E, Lp = aff_t.shape
    spec = pl.BlockSpec((1, E, Lp), lambda b: (b, 0, 0))
    ncp = -(-(Lp // LANE) // 8) * 8
    return pl.pallas_call(
        functools.partial(_topk_body, n_real=n_real, cap=cap),
        out_shape=(jax.ShapeDtypeStruct((B, E, cp), I32), jax.ShapeDtypeStruct((B, E, Lp), I32),
                   jax.ShapeDtypeStruct((B, 2, Lp), I32)),
        grid=(B,),
        in_specs=[spec],
        out_specs=(pl.BlockSpec((1, E, cp), lambda b: (b, 0, 0)), spec, pl.BlockSpec((1, 2, Lp), lambda b: (b, 0, 0))),
        scratch_shapes=[pltpu.VMEM((E, Lp), F32), pltpu.VMEM((E, ncp, LANE), F32)],
        compiler_params=_cp(("parallel",)),
        name="expert_topk",
    )(aff_t)


def _ffn_body(idxg_ref, idxw_ref, q_ref, xe_ref, wg_ref, wu_ref, wd_ref, z_ref, xbuf, x2d_sc, xb_sc, yacc, ybuf,
              gsem, ssem, *, cap, zr, n_tiles, n_r, n_f):
    s = pl.program_id(0)
    f = pl.program_id(1)
    nf = pl.num_programs(1)
    tr, D = xb_sc.shape
    B = xe_ref.shape[0]
    total = N_EXP * cap
    per = tr // n_f

    def tile(t):
        t = jnp.clip(t, 0, n_tiles - 1)
        return t // (B * n_r), (t // n_r) % B, (t % n_r) * tr

    @pl.when(f == 0)
    def _():
        @pl.when(s == 0)
        def _():
            xb_sc[...] = jnp.zeros(xb_sc.shape, BF16)
            ybuf[...] = jnp.zeros(ybuf.shape, F32)
            for bb in range(B):
                for r0 in range(total, zr + tr, tr):
                    n = min(tr, zr + tr - r0)
                    cp = pltpu.make_async_copy(ybuf.at[0:n], z_ref.at[bb, r0:r0 + n], ssem)
                    cp.start()
                    cp.wait()

        @pl.when(s >= 1)
        def _():
            pltpu.make_async_copy(xe_ref.at[0, 0:tr], xbuf, gsem).wait()
            x2d_sc[...] = xbuf[...].reshape(x2d_sc.shape)
            xb_sc[...] = x2d_sc[:, 0:D].astype(BF16)

    x = xb_sc[...]
    g = _dot(x, wg_ref[0])
    u = _dot(x, wu_ref[0])
    hmid = (g * jax.nn.sigmoid(g) * u).astype(BF16)
    contrib = _dot(hmid, wd_ref[0])

    _, gb, g0 = tile(s)
    _, wb, w0 = tile(s - 2)
    for i in range(per):
        j = f * per + i
        t = idxg_ref[0, 0, 0, g0 + j]
        pltpu.make_async_copy(xe_ref.at[gb, t], xbuf.at[j], gsem).start()
        live = (s >= 2) & (w0 + j < cap)
        dst = jnp.where(live, q_ref[0, 0, 0, idxw_ref[0, 0, 0, w0 + j]], zr + j)
        pltpu.make_async_copy(ybuf.at[j], z_ref.at[wb, dst], ssem).start()

    @pl.when(f == 0)
    def _():
        yacc[...] = contrib

    @pl.when(f > 0)
    def _():
        yacc[...] += contrib

    @pl.when(f == nf - 1)
    def _():
        pltpu.make_async_copy(ybuf, z_ref.at[0, 0:tr], ssem).wait()

        @pl.when(s >= 1)
        def _():
            e, _, _ = tile(s - 1)
            aff = x2d_sc[:, D:D + LANE]
            lane = lax.broadcasted_iota(I32, aff.shape, 1)
            gate = jnp.sum(jnp.where(lane == e, aff, 0.0), axis=1, keepdims=True)
            ybuf[...] = (yacc[...] * gate).reshape(tr, 1, D)

        @pl.when(s == pl.num_programs(0) - 1)
        def _():
            pltpu.make_async_copy(xe_ref.at[0, 0:tr], xbuf, gsem).wait()


def _ffn(idx, q, xe, wg, wu, wd, cap, cr, zr, tf):
    B, E, _, cp = idx.shape
    Lp = q.shape[2]
    De = xe.shape[3]
    D, F = wg.shape[1], wg.shape[2]
    n_r = 3
    tr = cr // n_r
    n_f = F // tf
    n_tiles = E * B * n_r

    def at_tile(off):
        def index_map(s, f):
            t = jnp.clip(s + off, 0, n_tiles - 1)
            return ((t // n_r) % B, t // (B * n_r), 0, 0)
        return index_map

    expert = lambda s: jnp.clip(s - 1, 0, n_tiles - 1) // (B * n_r)
    smem = lambda n, off: pl.BlockSpec((1, 1, 1, n), at_tile(off), memory_space=pltpu.SMEM)
    return pl.pallas_call(
        functools.partial(_ffn_body, cap=cap, zr=zr, n_tiles=n_tiles, n_r=n_r, n_f=n_f),
        out_shape=jax.ShapeDtypeStruct((B, zr + tr, 1, D), F32),
        grid=(n_tiles + 2, n_f),
        in_specs=[smem(cp, 0), smem(cp, -2), smem(Lp, -2), pl.BlockSpec(memory_space=pl.ANY),
                  pl.BlockSpec((1, D, tf), lambda s, f: (expert(s), 0, f)),
                  pl.BlockSpec((1, D, tf), lambda s, f: (expert(s), 0, f)),
                  pl.BlockSpec((1, tf, D), lambda s, f: (expert(s), f, 0))],
        out_specs=pl.BlockSpec(memory_space=pl.ANY),
        scratch_shapes=[pltpu.VMEM((tr, 1, De), F32), pltpu.VMEM((tr, De), F32), pltpu.VMEM((tr, D), BF16),
                        pltpu.VMEM((tr, D), F32),
                        pltpu.VMEM((tr, 1, D), F32), pltpu.SemaphoreType.DMA(()), pltpu.SemaphoreType.DMA(())],
        compiler_params=_cp(("arbitrary", "arbitrary"), 56),
        name="expert_ffn",
    )(idx, idx, q.reshape(B, E, 1, Lp), xe, wg, wu, wd)


def _combine_body(k0_ref, k1_ref, h_ref, lohi_ref, g_ref, z_ref, h2_ref, u_ref, buf, zb_sc, acc_sc, sem, *, nblk):
    b = pl.program_id(0)
    n = b * nblk + pl.program_id(1)
    n_total = pl.num_programs(0) * nblk
    k0 = k0_ref[n]
    k1 = k1_ref[n]
    lo = lohi_ref[0, 0:1, :]
    hi = lohi_ref[0, 1:2, :]
    acc_sc[...] = jnp.zeros(acc_sc.shape, F32)

    def copy(bb, k):
        r0 = pl.multiple_of(k * ZCHUNK, ZCHUNK)
        slot = lax.rem(k, 2)
        return pltpu.make_async_copy(z_ref.at[bb, pl.ds(r0, ZCHUNK)], buf.at[slot], sem.at[slot])

    def resident(n_prev, n_next):
        return ((n_prev // nblk == n_next // nblk) & (k1_ref[n_prev] > k0_ref[n_prev])
                & (k1_ref[n_prev] - 1 == k0_ref[n_next]))

    @pl.when(k1 > k0)
    def _():
        @pl.when(n == 0)
        def _():
            copy(b, k0).start()

        @pl.when((n == 0) | jnp.logical_not(resident(jnp.maximum(n - 1, 0), n)))
        def _():
            copy(b, k0).wait()

    def chunk(k, c):
        @pl.when(k + 1 < k1)
        def _():
            copy(b, k + 1).start()

        w = k * ZCHUNK + lax.broadcasted_iota(I32, (ZCHUNK, lo.shape[1]), 0)
        band_t = jnp.where((w >= lo) & (w < hi), 1.0, 0.0)
        zb_sc[...] = buf[lax.rem(k, 2)].reshape(zb_sc.shape)
        acc_sc[...] += _dot(band_t.T.astype(BF16), zb_sc[...].astype(BF16))

        @pl.when(k + 1 < k1)
        def _():
            copy(b, k + 1).wait()
        return c

    lax.fori_loop(k0, k1, chunk, 0)

    n_next = jnp.minimum(n + 1, n_total - 1)

    @pl.when((n + 1 < n_total) & (k1_ref[n_next] > k0_ref[n_next]) & jnp.logical_not(resident(n, n_next)))
    def _():
        copy(n_next // nblk, k0_ref[n_next]).start()

    h2 = h_ref[0] + acc_sc[...]
    h2_ref[0] = h2
    u_ref[0] = _rms(h2, g_ref[...]).astype(u_ref.dtype)


def _combine(h1, lohi, z, g, n_rows, u_dtype):
    B, Lp, D = h1.shape
    nblk = n_rows // TOKB
    lo = lohi[:, 0, :n_rows].reshape(B, nblk, TOKB)
    hi = lohi[:, 1, :n_rows].reshape(B, nblk, TOKB)
    k0 = (lo[:, :, 0] // ZCHUNK).reshape(-1).astype(I32)
    k1 = ((hi[:, :, -1] + ZCHUNK - 1) // ZCHUNK).reshape(-1).astype(I32)
    k1 = jnp.where(hi[:, :, -1].reshape(-1) > lo[:, :, 0].reshape(-1), k1, k0)
    row = lambda b, j, *_: (b, j, 0)
    grid_spec = pltpu.PrefetchScalarGridSpec(
        num_scalar_prefetch=2,
        grid=(B, nblk),
        in_specs=[pl.BlockSpec((1, TOKB, D), row),
                  pl.BlockSpec((1, 2, TOKB), lambda b, j, *_: (b, 0, j)),
                  pl.BlockSpec((1, D), lambda b, j, *_: (0, 0)),
                  pl.BlockSpec(memory_space=pl.ANY)],
        out_specs=(pl.BlockSpec((1, TOKB, D), row), pl.BlockSpec((1, TOKB, D), row)),
        scratch_shapes=[pltpu.VMEM((2, ZCHUNK, 1, D), z.dtype), pltpu.VMEM((ZCHUNK, D), F32),
                        pltpu.VMEM((TOKB, D), F32), pltpu.SemaphoreType.DMA((2,))],
    )
    return pl.pallas_call(
        functools.partial(_combine_body, nblk=nblk),
        out_shape=(jax.ShapeDtypeStruct((B, n_rows, D), F32), jax.ShapeDtypeStruct((B, n_rows, D), u_dtype)),
        grid_spec=grid_spec,
        compiler_params=_cp(("arbitrary", "arbitrary")),
        name="moe_combine",
    )(k0, k1, h1, lohi, g.reshape(1, D), z)


def _moe(h1, xe, aff_t, wg, wu, wd, g_next, n_real, n_rows_out, u_dtype):
    B, Lp, D = h1.shape
    cap = (CAP_F * (n_real + N_META)) // N_EXP
    cr = -(-cap // 48) * 48
    cp = -(-(cap + 1) // LANE) * LANE
    zr = -(-(N_EXP * cap) // ZCHUNK) * ZCHUNK
    if zr == N_EXP * cap:
        zr += ZCHUNK
    idx, q, lohi = _topk(aff_t, n_real, cap, cp)
    idx = idx.reshape(B, N_EXP, 1, cp)
    z = _ffn(idx, q, xe, wg, wu, wd, cap, cr, zr, min(wg.shape[2], 512))
    return _combine(h1, lohi, z, g_next, n_rows_out, u_dtype)


def _rope_tables(n_real, Lp):
    pos = jnp.concatenate([jnp.arange(n_real) + N_META, jnp.arange(N_META), jnp.zeros((Lp - n_real - N_META,), I32)])
    inv_freq = 1.0 / (ROPE_THETA ** (jnp.arange(0, MLA_ROPE, 2, dtype=F32) / MLA_ROPE))
    ang = pos.astype(F32)[:, None] * inv_freq[None, :]
    cos, sin = jnp.cos(ang), jnp.sin(ang)
    half = MLA_ROPE // 2
    z = lambda n: jnp.zeros((Lp, n), F32)
    c = jnp.concatenate([cos, cos, z(LANE - 2 * half)], axis=1)
    sn = jnp.concatenate([-sin, z(LANE - half)], axis=1)
    sp = jnp.concatenate([z(half), sin, z(LANE - 2 * half)], axis=1)
    return (c, sn, sp), (cos.T, sin.T)


def kernel(x, meta_tokens, norm_mix, norm_ffn, norm_final, ab_w_in, ab_q_norm, ab_kv_norm, ab_w_uq, ab_w_ukv, ab_rpb, ab_w_out, c_w_in, c_sinks, c_w_out, ec_w_router, ec_w_gate, ec_w_up, ec_w_down):
    B, S, D = x.shape
    depth = norm_mix.shape[0]
    Lp = S + TAIL
    tm = 512
    meta = jnp.broadcast_to(meta_tokens[None].astype(x.dtype), (B, N_META, D))
    h = jnp.concatenate([x, meta, jnp.zeros((B, TAIL - N_META, D), x.dtype)], axis=1)
    u = _norm(h, norm_mix[0], tm)
    tabs, tabs_t = _rope_tables(S, Lp)

    for layer in range(depth):
        if layer % 2 == 0:
            e = layer // 2
            w_in = ab_w_in[e].astype(BF16)
            o2 = MLA_QL + MLA_KVL
            o3 = o2 + MLA_ROPE
            lat = _mm(u, w_in[:, :o2], tm, o2)
            krope = _mm(u, jnp.pad(w_in[:, o2:o3], ((0, 0), (0, LANE - MLA_ROPE))), tm, LANE)
            na_qkv = _mm(u, w_in[:, o3:], tm, 1024)
            wq = jnp.pad(ab_w_uq[e].reshape(MLA_QL, MLA_H, MLA_NOPE + MLA_ROPE),
                         ((0, 0), (0, 0), (0, MLA_HP - MLA_NOPE - MLA_ROPE))).reshape(MLA_QL, MLA_H * MLA_HP)
            wkv = ab_w_ukv[e].reshape(MLA_KVL, MLA_H, MLA_NOPE + MLA_V)
            wk = wkv[:, :, :MLA_NOPE].reshape(MLA_KVL, -1)
            wv = wkv[:, :, MLA_NOPE:].reshape(MLA_KVL, -1)
            qt, k, vt = _mla_up(lat, krope, ab_q_norm[e], ab_kv_norm[e], wq.T.astype(BF16), wk.astype(BF16),
                                wv.T.astype(BF16), tabs, tabs_t, tm)
            y_mla = _mla_attention(qt, k, vt, S, TAIL, min(S, 512))
            y_na = _na_attention(na_qkv, _na_bias_table(ab_rpb[e]), S)
            ys = [y_mla, y_na]
            w_out = ab_w_out[e]
        else:
            o = layer // 2
            qw, kw = SWA_H * SWA_D, SWA_KV * SWA_D
            w_in = c_w_in[o]
            wq = w_in[:, :qw] * (SWA_D ** -0.5 * math.log2(math.e))
            wqv_t = jnp.concatenate([wq, w_in[:, qw + kw:]], axis=1).T.astype(BF16)
            qvt = _mm_t(u, wqv_t, tm, (qw + kw) // 3)
            k = _mm(u, w_in[:, qw:qw + kw].astype(BF16), tm, kw)
            ys = [_swa_attention(qvt, k, c_sinks[o], S)]
            w_out = c_w_out[o]
        wr = jnp.pad(ec_w_router[layer], ((0, 0), (0, LANE - N_EXP)))
        wr_hi = wr.astype(BF16)
        wr = jnp.concatenate([wr_hi, (wr - wr_hi.astype(F32)).astype(BF16)], axis=1)
        h1, xe, aff_t = _outproj(h, ys, w_out.astype(BF16), norm_ffn[layer], wr, 256, y_t=layer % 2 == 1)
        last = layer == depth - 1
        g_next = norm_final if last else norm_mix[layer + 1]
        h, u = _moe(h1, xe, aff_t, ec_w_gate[layer].astype(BF16), ec_w_up[layer].astype(BF16),
                    ec_w_down[layer].astype(BF16), g_next, S, S if last else Lp, F32 if last else BF16)
    return u
```

```python
import functools
import math

import jax
import jax.numpy as jnp
from jax import lax
from jax.experimental import pallas as pl
from jax.experimental.pallas import tpu as pltpu

F32 = jnp.float32
BF16 = jnp.bfloat16
I32 = jnp.int32

N_META = 16
GRID_W = 64
QBLK = 128
EPS = 1e-6
NEG = -1e30

MLA_H = 8
MLA_NOPE = 128
MLA_ROPE = 64
MLA_V = 128
MLA_QL = 512
MLA_KVL = 512
MLA_HP = 256
ROPE_THETA = 10000.0

NA_H = 8
NA_D = 128
NA_KH = 8
NA_KW = 16

SWA_H = 32
SWA_KV = 4
SWA_G = SWA_H // SWA_KV
SWA_D = 64
SWA_WIN = 128

N_EXP = 16
CAP_F = 2

TAIL = 512
LANE = 128
ZCHUNK = 256
TOKB = 256


def _cp(sem, vmem_mb=48):
    return pltpu.CompilerParams(dimension_semantics=sem, vmem_limit_bytes=vmem_mb << 20)


def _dot(a, b):
    return jnp.dot(a, b, preferred_element_type=F32)


def _dot_nt(a, b):
    return lax.dot_general(a, b, (((1,), (1,)), ((), ())), preferred_element_type=F32)


def _rms(x, g):
    return x * lax.rsqrt(jnp.mean(x * x, axis=-1, keepdims=True) + EPS) * g


def _embed_body(x_ref, meta_ref, g_ref, h_ref, u_ref, *, n_blocks):
    i = pl.program_id(1)

    @pl.when(i < n_blocks)
    def _():
        h_ref[0] = x_ref[0]

    @pl.when(i >= n_blocks)
    def _():
        h_ref[0] = jnp.zeros(h_ref.shape[1:], h_ref.dtype)
        h_ref[0, 0:N_META, :] = meta_ref[...]

    u_ref[0] = _rms(h_ref[0], g_ref[...]).astype(u_ref.dtype)


def _embed(x, meta, g, tm):
    B, S, D = x.shape
    Lp = S + TAIL
    nb = S // tm
    row = lambda b, i: (b, i, 0)
    return pl.pallas_call(
        functools.partial(_embed_body, n_blocks=nb),
        out_shape=(jax.ShapeDtypeStruct((B, Lp, D), x.dtype), jax.ShapeDtypeStruct((B, Lp, D), BF16)),
        grid=(B, Lp // tm),
        in_specs=[pl.BlockSpec((1, tm, D), lambda b, i: (b, jnp.minimum(i, nb - 1), 0)),
                  pl.BlockSpec(meta.shape, lambda b, i: (0, 0)), pl.BlockSpec((1, D), lambda b, i: (0, 0))],
        out_specs=(pl.BlockSpec((1, tm, D), row), pl.BlockSpec((1, tm, D), row)),
        compiler_params=_cp(("parallel", "parallel")),
        name="embed_rmsnorm",
    )(x, meta, g.reshape(1, D))


def _mm_body(x_ref, w_ref, o_ref):
    o_ref[0] = _dot(x_ref[0], w_ref[...]).astype(o_ref.dtype)


def _mm(x, w, tm, tn):
    B, Lp, K = x.shape
    N = w.shape[1]
    return pl.pallas_call(
        _mm_body,
        out_shape=jax.ShapeDtypeStruct((B, Lp, N), BF16),
        grid=(B, Lp // tm, N // tn),
        in_specs=[pl.BlockSpec((1, tm, K), lambda b, i, j: (b, i, 0)), pl.BlockSpec((K, tn), lambda b, i, j: (0, j))],
        out_specs=pl.BlockSpec((1, tm, tn), lambda b, i, j: (b, i, j)),
        compiler_params=_cp(("parallel", "parallel", "parallel")),
        name="in_proj",
    )(x, w)


def _mm_t_body(x_ref, wt_ref, o_ref):
    o_ref[0] = _dot_nt(wt_ref[...], x_ref[0]).astype(o_ref.dtype)


def _mm_t(x, wt, tm, tn):
    B, Lp, K = x.shape
    N = wt.shape[0]
    return pl.pallas_call(
        _mm_t_body,
        out_shape=jax.ShapeDtypeStruct((B, N, Lp), BF16),
        grid=(B, Lp // tm, N // tn),
        in_specs=[pl.BlockSpec((1, tm, K), lambda b, i, j: (b, i, 0)), pl.BlockSpec((tn, K), lambda b, i, j: (j, 0))],
        out_specs=pl.BlockSpec((1, tn, tm), lambda b, i, j: (b, j, i)),
        compiler_params=_cp(("parallel", "parallel", "parallel")),
        name="in_proj_t",
    )(x, wt)


def _rope(pe, c, sn, sp):
    return pe * c + pltpu.roll(pe, 96, 1) * sn + pltpu.roll(pe, 32, 1) * sp


def _uq_body(lat_ref, g_ref, wt_ref, c_ref, s_ref, o_ref, *, scale):
    xn = _rms(lat_ref[0].astype(F32), g_ref[...]).astype(BF16)
    zt = _dot_nt(wt_ref[...], xn)
    c, s = c_ref[...], s_ref[...]
    half = MLA_ROPE // 2
    dt = o_ref.dtype
    for h in range(MLA_H):
        o = h * MLA_HP
        p1, p2, p3 = o + MLA_NOPE, o + MLA_NOPE + half, o + MLA_NOPE + MLA_ROPE
        x1, x2 = zt[p1:p2], zt[p2:p3]
        o_ref[0, o:p1, :] = (zt[o:p1] * scale).astype(dt)
        o_ref[0, p1:p2, :] = ((x1 * c - x2 * s) * scale).astype(dt)
        o_ref[0, p2:p3, :] = ((x2 * c + x1 * s) * scale).astype(dt)
        o_ref[0, p3:o + MLA_HP, :] = jnp.zeros((o + MLA_HP - p3, zt.shape[1]), dt)


def _ukv_body(lat_ref, g_ref, wk_ref, wvt_ref, kr_ref, c_ref, sn_ref, sp_ref, k_ref, vt_ref):
    xn = _rms(lat_ref[0].astype(F32), g_ref[...]).astype(BF16)
    zk = _dot(xn, wk_ref[...])
    kr = _rope(kr_ref[0].astype(F32), c_ref[...], sn_ref[...], sp_ref[...]).astype(k_ref.dtype)
    for h in range(MLA_H):
        k_ref[0, :, h * MLA_HP:h * MLA_HP + LANE] = zk[:, h * LANE:(h + 1) * LANE].astype(k_ref.dtype)
        k_ref[0, :, h * MLA_HP + LANE:(h + 1) * MLA_HP] = kr
    vt_ref[0] = _dot_nt(wvt_ref[...], xn).astype(vt_ref.dtype)


def _mla_up(lat, krope, qn, kvn, wqt, wk, wvt, tabs, tabs_t, tm):
    B, Lp, _ = lat.shape
    c, sn, sp = tabs
    ct, st = tabs_t
    tab_spec = pl.BlockSpec((tm, LANE), lambda b, i: (i, 0))
    tabt_spec = pl.BlockSpec((MLA_ROPE // 2, tm), lambda b, i: (0, i))
    scale = (MLA_NOPE + MLA_ROPE) ** -0.5 * math.log2(math.e)
    qt = pl.pallas_call(
        functools.partial(_uq_body, scale=scale),
        out_shape=jax.ShapeDtypeStruct((B, MLA_H * MLA_HP, Lp), BF16),
        grid=(B, Lp // tm),
        in_specs=[pl.BlockSpec((1, tm, MLA_QL), lambda b, i: (b, i, 0)),
                  pl.BlockSpec((1, MLA_QL), lambda b, i: (0, 0)),
                  pl.BlockSpec(wqt.shape, lambda b, i: (0, 0)),
                  tabt_spec, tabt_spec],
        out_specs=pl.BlockSpec((1, MLA_H * MLA_HP, tm), lambda b, i: (b, 0, i)),
        compiler_params=_cp(("parallel", "parallel")),
        name="mla_q_up",
    )(lat, qn.reshape(1, -1), wqt, ct, st)
    k, vt = pl.pallas_call(
        _ukv_body,
        out_shape=(jax.ShapeDtypeStruct((B, Lp, MLA_H * MLA_HP), BF16),
                   jax.ShapeDtypeStruct((B, MLA_H * MLA_V, Lp), BF16)),
        grid=(B, Lp // tm),
        in_specs=[pl.BlockSpec((1, tm, MLA_KVL), lambda b, i: (b, i, 1)),
                  pl.BlockSpec((1, MLA_KVL), lambda b, i: (0, 0)),
                  pl.BlockSpec(wk.shape, lambda b, i: (0, 0)),
                  pl.BlockSpec(wvt.shape, lambda b, i: (0, 0)),
                  pl.BlockSpec((1, tm, LANE), lambda b, i: (b, i, 0)),
                  tab_spec, tab_spec, tab_spec],
        out_specs=(pl.BlockSpec((1, tm, MLA_H * MLA_HP), lambda b, i: (b, i, 0)),
                   pl.BlockSpec((1, MLA_H * MLA_V, tm), lambda b, i: (b, 0, i))),
        compiler_params=_cp(("parallel", "parallel")),
        name="mla_kv_up",
    )(lat, kvn.reshape(1, -1), wk, wvt, krope, c, sn, sp)
    return qt, k, vt


def _mla_body(qt_ref, k_ref, vt_ref, o_ref, m_sc, acc_sc, s_sc, *, n_real, tkc, ahead):
    qt = qt_ref[0]
    tq = qt.shape[1]
    nch = n_real // tkc
    ring = s_sc.shape[0]
    m_sc[...] = jnp.full(m_sc.shape, NEG, F32)
    acc_sc[...] = jnp.zeros(acc_sc.shape, F32)

    def scores(c):
        return _dot(k_ref[0, pl.ds(pl.multiple_of(c * tkc, tkc), tkc), :], qt)

    def consume(st, vtc):
        m_prev = m_sc[...]
        m_new = jnp.maximum(m_prev, jnp.max(st, axis=0, keepdims=True))
        p = jnp.exp2((st - m_new).astype(BF16))
        alpha = jnp.exp2(m_prev - m_new)
        vte = jnp.concatenate([vtc, jnp.ones((16, vtc.shape[1]), BF16)], axis=0)
        acc_sc[...] = alpha * acc_sc[...] + _dot(vte, p)
        m_sc[...] = m_new

    def values(c):
        return vt_ref[0, :, pl.ds(pl.multiple_of(c * tkc, tkc), tkc)]

    for u in range(ahead):
        s_sc[u] = scores(u)

    def trip(c0, last):
        for u in range(ring):
            if not (last and u + ahead >= ring):
                s_sc[(u + ahead) % ring] = scores(c0 + u + ahead)
            consume(s_sc[u], values(c0 + u))

    def body(cb, carry):
        trip(cb * ring, False)
        return carry

    lax.fori_loop(0, nch // ring - 1, body, 0)
    trip(jnp.int32(nch - ring), True)
    consume(_dot(k_ref[0, n_real:n_real + N_META, :], qt), vt_ref[0, :, n_real:n_real + N_META])
    out = (acc_sc[0:MLA_V, :] / acc_sc[MLA_V:MLA_V + 1, :]).T
    i = pl.program_id(2)
    row = i * tq + lax.broadcasted_iota(I32, (tq, 1), 0)
    o_ref[0] = jnp.where(row < n_real + N_META, out, 0.0).astype(o_ref.dtype)


def _mla_attention(qt, k, vt, n_real, tq, tkc):
    B, Lp, _ = k.shape
    nch = n_real // tkc
    ring, ahead = (8, 3) if nch % 8 == 0 else ((4, 2) if nch % 4 == 0 else (2, 1))
    return pl.pallas_call(
        functools.partial(_mla_body, n_real=n_real, tkc=tkc, ahead=ahead),
        out_shape=jax.ShapeDtypeStruct((B, Lp, MLA_H * MLA_V), BF16),
        grid=(B, MLA_H, Lp // tq),
        in_specs=[pl.BlockSpec((1, MLA_HP, tq), lambda b, h, i: (b, h, i)),
                  pl.BlockSpec((1, Lp, MLA_HP), lambda b, h, i: (b, 0, h)),
                  pl.BlockSpec((1, MLA_V, Lp), lambda b, h, i: (b, h, 0))],
        out_specs=pl.BlockSpec((1, tq, MLA_V), lambda b, h, i: (b, i, h)),
        scratch_shapes=[pltpu.VMEM((1, tq), F32), pltpu.VMEM((MLA_V + 16, tq), F32),
                        pltpu.VMEM((ring, tkc, tq), F32)],
        compiler_params=_cp(("parallel", "parallel", "arbitrary")),
        name="mla_attention",
    )(qt, k, vt)


def _na_bias_table(rpb):
    col = jnp.arange(GRID_W)
    cs = jnp.clip(col - NA_KW // 2, 0, GRID_W - NA_KW)
    ok = (col[None, :] >= cs[:, None]) & (col[None, :] < cs[:, None] + NA_KW)
    dc = jnp.clip(col[None, :] - col[:, None], -(NA_KW - 1), NA_KW - 1) + (NA_KW - 1)
    d = jnp.arange(NA_KH)[:, None] + jnp.arange(NA_KH)[None, :]
    t = rpb[:, d][:, :, :, dc]
    t = jnp.where(ok[None, None, None], t, NEG)
    return t.transpose(0, 1, 3, 2, 4).reshape(rpb.shape[0], NA_KH, GRID_W, NA_KH * GRID_W).astype(F32)


def _na_body(q_ref, k_ref, v_ref, t_ref, o_ref, s_sc, sm_sc, p_sc, pm_sc, l_sc, *, n_real, rpg):
    i = pl.program_id(2)
    rows = n_real // GRID_W
    scale = NA_D ** -0.5
    km = k_ref[0, n_real:n_real + N_META, :]
    vm = v_ref[0, n_real:n_real + N_META, :]

    @pl.when(i < rows // rpg)
    def _():
        def window(rr):
            r = i * rpg + rr
            rs = jnp.clip(r - NA_KH // 2, 0, rows - NA_KH)
            return rs - r + (NA_KH - 1), pl.multiple_of(rs * GRID_W, GRID_W)

        for rr in range(rpg):
            d0, k0 = window(rr)
            qr = q_ref[0, rr * GRID_W:(rr + 1) * GRID_W, :]
            s_sc[rr] = _dot_nt(qr, k_ref[0, pl.ds(k0, NA_KH * GRID_W), :]) * scale + t_ref[0, d0]
            sm_sc[rr] = _dot_nt(qr, km) * scale
        for rr in range(rpg):
            s, sm = s_sc[rr], sm_sc[rr]
            m = jnp.maximum(jnp.max(s, axis=1, keepdims=True), jnp.max(sm, axis=1, keepdims=True))
            p = jnp.exp(s - m)
            pm = jnp.exp(sm - m)
            l_sc[rr] = jnp.sum(p, axis=1, keepdims=True) + jnp.sum(pm, axis=1, keepdims=True)
            p_sc[rr] = p.astype(BF16)
            pm_sc[rr] = pm.astype(BF16)
        for rr in range(rpg):
            _, k0 = window(rr)
            o = (_dot(p_sc[rr], v_ref[0, pl.ds(k0, NA_KH * GRID_W), :]) + _dot(pm_sc[rr], vm)) / l_sc[rr]
            o_ref[0, rr * GRID_W:(rr + 1) * GRID_W, :] = o.astype(o_ref.dtype)

    @pl.when(i >= rows // rpg)
    def _():
        qm = q_ref[0, 0:N_META, :]
        sm = _dot_nt(qm, km) * scale
        pm = jnp.exp(sm - jnp.max(sm, axis=1, keepdims=True))
        o = _dot(pm.astype(BF16), vm) / jnp.sum(pm, axis=1, keepdims=True)
        o_ref[0] = jnp.zeros(o_ref.shape[1:], o_ref.dtype)
        o_ref[0, 0:N_META, :] = o.astype(o_ref.dtype)


def _na_attention(qkv, table, n_real):
    B, Lp, _ = qkv.shape
    rpg = TAIL // GRID_W
    tq = rpg * GRID_W
    return pl.pallas_call(
        functools.partial(_na_body, n_real=n_real, rpg=rpg),
        out_shape=jax.ShapeDtypeStruct((B, Lp, NA_H * NA_D), BF16),
        grid=(B, NA_H, Lp // tq),
        in_specs=[pl.BlockSpec((1, tq, NA_D), lambda b, h, i: (b, i, h)),
                  pl.BlockSpec((1, Lp, NA_D), lambda b, h, i: (b, 0, NA_H + h)),
                  pl.BlockSpec((1, Lp, NA_D), lambda b, h, i: (b, 0, 2 * NA_H + h)),
                  pl.BlockSpec((1,) + table.shape[1:], lambda b, h, i: (h, 0, 0, 0))],
        out_specs=pl.BlockSpec((1, tq, NA_D), lambda b, h, i: (b, i, h)),
        scratch_shapes=[pltpu.VMEM((rpg, GRID_W, NA_KH * GRID_W), F32), pltpu.VMEM((rpg, GRID_W, N_META), F32),
                        pltpu.VMEM((rpg, GRID_W, NA_KH * GRID_W), BF16), pltpu.VMEM((rpg, GRID_W, N_META), BF16),
                        pltpu.VMEM((rpg, GRID_W, 1), F32)],
        compiler_params=_cp(("parallel", "parallel", "arbitrary")),
        name="na_attention",
    )(qkv, qkv, qkv, table)


def _swa_bias_tables():
    a = jnp.arange(QBLK)[None, :]
    j = jnp.arange(3 * QBLK)[:, None]
    dist = jnp.abs(j - QBLK - a)
    slopes = 2.0 ** (-8.0 * jnp.arange(1, SWA_H + 1, dtype=F32) / SWA_H)
    real = jnp.where(dist <= SWA_WIN, -slopes[:, None, None] * dist.astype(F32) * math.log2(math.e), NEG)
    meta = jnp.where(N_META + jnp.arange(QBLK)[:, None] - a <= SWA_WIN, 0.0, NEG)
    return real.astype(F32), meta.astype(F32)


def _swa_body(sink_ref, qt_ref, kp_ref, kc_ref, kn_ref, km_ref, vp_ref, vc_ref, vn_ref, vm_ref, tr_ref, tm_ref, o_ref,
              *, n_real):
    i = pl.program_id(1)
    nb = n_real // QBLK
    log2e = math.log2(math.e)
    gw = SWA_G * SWA_D
    kmeta = km_ref[0, 0:N_META, :]
    vmeta_t = vm_ref[0, :, 0:N_META]

    def attend(kall, vall_t, bias_of):
        for kv in range(SWA_KV):
            tile, half = kv // 2, kv % 2
            q64 = jnp.concatenate([qt_ref[0, kv * gw + g * SWA_D:kv * gw + (g + 1) * SWA_D, :] for g in range(SWA_G)],
                                  axis=1)
            z64 = jnp.zeros_like(q64)
            qpad = jnp.concatenate([q64, z64] if half == 0 else [z64, q64], axis=0)
            st = _dot(kall[:, tile * LANE:(tile + 1) * LANE], qpad)
            sm = _dot(kmeta[:, tile * LANE:(tile + 1) * LANE], qpad)
            ps, ls = [], []
            for g in range(SWA_G):
                h = kv * SWA_G + g
                cols = slice(g * QBLK, (g + 1) * QBLK)
                s = st[:, cols] + bias_of(h)
                smg = sm[:, cols]
                sink = sink_ref[h] * log2e
                m = jnp.maximum(jnp.maximum(jnp.max(s, axis=0, keepdims=True), jnp.max(smg, axis=0, keepdims=True)),
                                sink)
                p = jnp.exp2(s - m)
                pm = jnp.exp2(smg - m)
                ls.append(jnp.sum(p, axis=0, keepdims=True) + jnp.sum(pm, axis=0, keepdims=True) + jnp.exp2(sink - m))
                ps.append((p.astype(BF16), pm.astype(BF16)))
            p_all = jnp.concatenate([p for p, _ in ps], axis=1)
            pm_all = jnp.concatenate([pm for _, pm in ps], axis=1)
            rows = slice(kv * SWA_D, (kv + 1) * SWA_D)
            ot = (_dot(vall_t[rows, :], p_all) + _dot(vmeta_t[rows, :], pm_all)) / jnp.concatenate(ls, axis=1)
            for g in range(SWA_G):
                o_ref[0, kv * gw + g * SWA_D:kv * gw + (g + 1) * SWA_D, :] = ot[:, g * QBLK:(g + 1) * QBLK].astype(
                    o_ref.dtype)

    @pl.when(i < nb)
    def _():
        kall = jnp.concatenate([kp_ref[0], kc_ref[0], kn_ref[0]], axis=0)
        vall_t = jnp.concatenate([vp_ref[0], vc_ref[0], vn_ref[0]], axis=1)
        row = lax.broadcasted_iota(I32, (3 * QBLK, QBLK), 0)
        pen = jnp.where(((i == 0) & (row < QBLK)) | ((i == nb - 1) & (row >= 2 * QBLK)), NEG, 0.0)
        attend(kall, vall_t, lambda h: tr_ref[h] + pen)

    @pl.when(i == nb)
    def _():
        attend(kn_ref[0], vn_ref[0], lambda h: tm_ref[...])
        lane = lax.broadcasted_iota(I32, o_ref.shape[1:], 1)
        o_ref[0] = jnp.where(lane < N_META, o_ref[0], jnp.zeros(o_ref.shape[1:], o_ref.dtype))

    @pl.when(i > nb)
    def _():
        o_ref[0] = jnp.zeros(o_ref.shape[1:], o_ref.dtype)


def _swa_attention(qvt, k, sinks, n_real):
    B, Lp, kw = k.shape
    nb = n_real // QBLK
    qw = SWA_H * SWA_D
    vrow = qw // kw
    prev = lambda i: jnp.clip(i - 1, 0, nb - 1)
    cur = lambda i: jnp.minimum(i, nb - 1)
    nxt = lambda i: jnp.where(i >= nb, 0, jnp.minimum(i + 1, nb - 1))
    met = lambda i: nb
    kspec = lambda f: pl.BlockSpec((1, QBLK, kw), lambda b, i: (b, f(i), 0))
    vspec = lambda f: pl.BlockSpec((1, kw, QBLK), lambda b, i: (b, vrow, f(i)))
    t_real, t_meta = _swa_bias_tables()
    return pl.pallas_call(
        functools.partial(_swa_body, n_real=n_real),
        out_shape=jax.ShapeDtypeStruct((B, qw, Lp), BF16),
        grid=(B, Lp // QBLK),
        in_specs=[pl.BlockSpec(memory_space=pltpu.SMEM),
                  pl.BlockSpec((1, qw, QBLK), lambda b, i: (b, 0, i)),
                  kspec(prev), kspec(cur), kspec(nxt), kspec(met),
                  vspec(prev), vspec(cur), vspec(nxt), vspec(met),
                  pl.BlockSpec(t_real.shape, lambda b, i: (0, 0, 0)),
                  pl.BlockSpec(t_meta.shape, lambda b, i: (0, 0))],
        out_specs=pl.BlockSpec((1, qw, QBLK), lambda b, i: (b, 0, i)),
        compiler_params=_cp(("parallel", "arbitrary")),
        name="swa_attention",
    )(sinks, qvt, k, k, k, k, qvt, qvt, qvt, qvt, t_real, t_meta)


def _outproj_body(*refs, n_y, y_t):
    h_ref = refs[0]
    y_refs = refs[1:1 + n_y]
    wo_ref, g_ref, wr_ref, h1_ref, xe_ref, aff_ref = refs[1 + n_y:]
    D = h_ref.shape[2]
    acc = h_ref[0]
    k0 = 0
    for y_ref in y_refs:
        if y_t:
            kk = y_ref.shape[1]
            acc = acc + lax.dot_general(y_ref[0], wo_ref[k0:k0 + kk, :], (((0,), (0,)), ((), ())),
                                        preferred_element_type=F32)
        else:
            kk = y_ref.shape[2]
            acc = acc + _dot(y_ref[0], wo_ref[k0:k0 + kk, :])
        k0 += kk
    h1_ref[0] = acc
    xn = _rms(acc, g_ref[...])
    xh = xn.astype(BF16)
    xl = (xn - xh.astype(F32)).astype(BF16)
    hh = _dot(xh, wr_ref[...])
    logits = hh[:, 0:LANE] + hh[:, LANE:2 * LANE] + _dot(xl, wr_ref[:, 0:LANE])
    lane = lax.broadcasted_iota(I32, logits.shape, 1)
    logits = jnp.where(lane < N_EXP, logits, NEG)
    e = jnp.exp(logits - jnp.max(logits, axis=1, keepdims=True))
    aff = e / jnp.sum(e, axis=1, keepdims=True)
    tm = xn.shape[0]
    xe_ref[0, :, :, 0:D] = xn.reshape(tm, 1, D)
    xe_ref[0, :, :, D:D + LANE] = aff.reshape(tm, 1, LANE)
    aff_ref[0] = aff.T[0:N_EXP, :]


def _outproj(h, ys, wo, g, wr, tm, y_t=False):
    B, Lp, D = h.shape
    row = lambda b, i: (b, i, 0)
    full = lambda b, i: (0, 0)
    y_spec = (lambda y: pl.BlockSpec((1, y.shape[1], tm), lambda b, i: (b, 0, i))) if y_t else (
        lambda y: pl.BlockSpec((1, tm, y.shape[2]), row))
    return pl.pallas_call(
        functools.partial(_outproj_body, n_y=len(ys), y_t=y_t),
        out_shape=(jax.ShapeDtypeStruct((B, Lp, D), F32),
                   jax.ShapeDtypeStruct((B, Lp, 1, D + LANE), F32),
                   jax.ShapeDtypeStruct((B, N_EXP, Lp), F32)),
        grid=(B, Lp // tm),
        in_specs=[pl.BlockSpec((1, tm, D), row)] + [y_spec(y) for y in ys]
        + [pl.BlockSpec(wo.shape, full), pl.BlockSpec((1, D), full), pl.BlockSpec(wr.shape, full)],
        out_specs=(pl.BlockSpec((1, tm, D), row), pl.BlockSpec((1, tm, 1, D + LANE), lambda b, i: (b, i, 0, 0)),
                   pl.BlockSpec((1, N_EXP, tm), lambda b, i: (b, 0, i))),
        compiler_params=_cp(("parallel", "parallel")),
        name="outproj_router",
    )(h, *ys, wo, g.reshape(1, D), wr)


def _topk_body(aff_ref, idx_ref, q_ref, lohi_ref, sel_sc, chunk_sc, *, n_real, cap):
    Lp = aff_ref.shape[2]
    nch = Lp // LANE
    cm = n_real // LANE
    aff = aff_ref[0]
    tok = lax.broadcasted_iota(I32, aff.shape, 1)
    keys = jnp.where(tok < n_real + N_META, pltpu.bitcast(aff, I32), -1)

    def search(it, prefix):
        cand = prefix | lax.shift_left(jnp.int32(1), 30 - it)
        cnt = jnp.sum(jnp.where(keys >= cand, 1.0, 0.0), axis=1, keepdims=True)
        return jnp.where(cnt >= cap, cand, prefix)

    thr = lax.fori_loop(0, 31, search, jnp.zeros((N_EXP, 1), I32))
    need = cap - jnp.sum(jnp.where(keys > thr, 1.0, 0.0), axis=1, keepdims=True)

    ri = lax.broadcasted_iota(I32, (LANE, LANE), 0)
    ci = lax.broadcasted_iota(I32, (LANE, LANE), 1)
    upper = (ri <= ci).astype(BF16)
    er = lax.broadcasted_iota(I32, (N_EXP, N_EXP), 0)
    ec = lax.broadcasted_iota(I32, (N_EXP, N_EXP), 1)
    lower = (ec < er).astype(BF16)

    sel_sc[...] = jnp.zeros(sel_sc.shape, F32)

    def select(c, carry):
        c0 = pl.multiple_of(c * LANE, LANE)
        a = pltpu.bitcast(aff_ref[0, :, pl.ds(c0, LANE)], I32)
        t = c0 + lax.broadcasted_iota(I32, a.shape, 1)
        kc = jnp.where(t < n_real + N_META, a, -1)
        eq = (kc == thr).astype(F32)
        rank = _dot(eq.astype(BF16), upper) - eq + carry
        sel = jnp.where((kc > thr) | ((eq > 0) & (rank < need)), 1.0, 0.0)
        sel_sc[:, pl.ds(c0, LANE)] = sel
        return carry + jnp.sum(eq, axis=1, keepdims=True)

    carry = select(jnp.int32(cm), jnp.zeros((N_EXP, 1), F32))
    lax.fori_loop(0, cm, select, carry)

    chunk_sc[...] = jnp.zeros(chunk_sc.shape, F32)

    def offsets(c, carry):
        off_c = carry
        c0 = pl.multiple_of(c * LANE, LANE)
        sel = sel_sc[:, pl.ds(c0, LANE)]
        selb = sel.astype(BF16)
        nt = jnp.sum(sel, axis=0, keepdims=True)
        off_incl = _dot(jnp.broadcast_to(nt, (8, LANE)).astype(BF16), upper)[0:1] + off_c
        r = _dot(lower, selb)
        q_ref[0, :, pl.ds(c0, LANE)] = (off_incl - nt + r).astype(I32)
        lohi_ref[0, 0:1, pl.ds(c0, LANE)] = (off_incl - nt).astype(I32)
        lohi_ref[0, 1:2, pl.ds(c0, LANE)] = off_incl.astype(I32)
        within = _dot(selb, upper)
        for e in range(N_EXP):
            chunk_sc[e, pl.ds(c, 1), :] = within[e:e + 1, :]
        return off_c + jnp.sum(nt, axis=1, keepdims=True)

    lax.fori_loop(0, nch, offsets, jnp.zeros((1, 1), F32))

    ncp = chunk_sc.shape[1]
    cpad = idx_ref.shape[2]
    cr_ = lax.broadcasted_iota(I32, (ncp, ncp), 0)
    cc_ = lax.broadcasted_iota(I32, (ncp, ncp), 1)
    lower_incl = (cc_ <= cr_).astype(BF16)
    slot = lax.broadcasted_iota(I32, (1, cpad), 1).astype(F32)
    chunk_id = lax.broadcasted_iota(I32, (ncp, cpad), 0).astype(F32)
    for e in range(N_EXP):
        within = chunk_sc[e]
        tot = jnp.broadcast_to(within[:, LANE - 1:LANE], (ncp, LANE))
        cend = _dot(lower_incl, tot.astype(BF16))
        cstart = cend - tot
        cend_t = jnp.concatenate([cend] * (cpad // LANE), axis=1)
        cstart_t = jnp.concatenate([cstart] * (cpad // LANE), axis=1)
        cstar = jnp.sum(jnp.where(cend_t <= slot, 1.0, 0.0), axis=0, keepdims=True)
        onehot = chunk_id == cstar
        srel = slot - jnp.sum(jnp.where(onehot, cstart_t, 0.0), axis=0, keepdims=True)
        g = lax.dot_general(within.astype(BF16), jnp.where(onehot, 1.0, 0.0).astype(BF16), (((0,), (0,)), ((), ())),
                            preferred_element_type=F32)
        lane_in = jnp.sum(jnp.where(g <= srel, 1.0, 0.0), axis=0, keepdims=True)
        idx = jnp.where(slot < cap, cstar * LANE + lane_in, 0.0)
        idx_ref[0, e:e + 1, :] = idx.astype(I32)


def _topk(aff_t, n_real, cap, cp):
    B, E, Lp = aff_t.shape
    spec = pl.BlockSpec((1, E, Lp), lambda b: (b, 0, 0))
    ncp = -(-(Lp // LANE) // 8) * 8
    return pl.pallas_call(
        functools.partial(_topk_body, n_real=n_real, cap=cap),
        out_shape=(jax.ShapeDtypeStruct((B, E, cp), I32), jax.ShapeDtypeStruct((B, E, Lp), I32),
                   jax.ShapeDtypeStruct((B, 2, Lp), I32)),
        grid=(B,),
        in_specs=[spec],
        out_specs=(pl.BlockSpec((1, E, cp), lambda b: (b, 0, 0)), spec, pl.BlockSpec((1, 2, Lp), lambda b: (b, 0, 0))),
        scratch_shapes=[pltpu.VMEM((E, Lp), F32), pltpu.VMEM((E, ncp, LANE), F32)],
        compiler_params=_cp(("parallel",)),
        name="expert_topk",
    )(aff_t)


def _ffn_body(idxg_ref, idxw_ref, q_ref, xe_ref, wg_ref, wu_ref, wd_ref, z_ref, xbuf, x2d_sc, xb_sc, yacc, ybuf,
              gsem, ssem, *, cap, zr, n_tiles, n_r, n_f):
    s = pl.program_id(0)
    f = pl.program_id(1)
    nf = pl.num_programs(1)
    tr, D = xb_sc.shape
    B = xe_ref.shape[0]
    total = N_EXP * cap
    per = tr // n_f

    def tile(t):
        t = jnp.clip(t, 0, n_tiles - 1)
        return t // (B * n_r), (t // n_r) % B, (t % n_r) * tr

    @pl.when(f == 0)
    def _():
        @pl.when(s == 0)
        def _():
            xb_sc[...] = jnp.zeros(xb_sc.shape, BF16)
            ybuf[...] = jnp.zeros(ybuf.shape, F32)
            for bb in range(B):
                for r0 in range(total, zr + tr, tr):
                    n = min(tr, zr + tr - r0)
                    cp = pltpu.make_async_copy(ybuf.at[0:n], z_ref.at[bb, r0:r0 + n], ssem)
                    cp.start()
                    cp.wait()

        @pl.when(s >= 1)
        def _():
            pltpu.make_async_copy(xe_ref.at[0, 0:tr], xbuf, gsem).wait()
            x2d_sc[...] = xbuf[...].reshape(x2d_sc.shape)
            xb_sc[...] = x2d_sc[:, 0:D].astype(BF16)

    x = xb_sc[...]
    g = _dot(x, wg_ref[0, 0].astype(BF16))
    u = _dot(x, wu_ref[0, 0].astype(BF16))
    hmid = (g * jax.nn.sigmoid(g) * u).astype(BF16)
    contrib = _dot(hmid, wd_ref[0, 0].astype(BF16))

    _, gb, g0 = tile(s)
    _, wb, w0 = tile(s - 2)
    for i in range(per):
        j = f * per + i
        t = idxg_ref[0, 0, 0, g0 + j]
        pltpu.make_async_copy(xe_ref.at[gb, t], xbuf.at[j], gsem).start()
        live = (s >= 2) & (w0 + j < cap)
        dst = jnp.where(live, q_ref[0, 0, 0, idxw_ref[0, 0, 0, w0 + j]], zr + j)
        pltpu.make_async_copy(ybuf.at[j], z_ref.at[wb, dst], ssem).start()

    @pl.when(f == 0)
    def _():
        yacc[...] = contrib

    @pl.when(f > 0)
    def _():
        yacc[...] += contrib

    @pl.when(f == nf - 1)
    def _():
        pltpu.make_async_copy(ybuf, z_ref.at[0, 0:tr], ssem).wait()

        @pl.when(s >= 1)
        def _():
            e, _, _ = tile(s - 1)
            aff = x2d_sc[:, D:D + LANE]
            lane = lax.broadcasted_iota(I32, aff.shape, 1)
            gate = jnp.sum(jnp.where(lane == e, aff, 0.0), axis=1, keepdims=True)
            ybuf[...] = (yacc[...] * gate).reshape(tr, 1, D)

        @pl.when(s == pl.num_programs(0) - 1)
        def _():
            pltpu.make_async_copy(xe_ref.at[0, 0:tr], xbuf, gsem).wait()


def _ffn(idx, q, xe, wg, wu, wd, layer, cap, cr, zr, tf):
    B, E, _, cp = idx.shape
    Lp = q.shape[2]
    De = xe.shape[3]
    D, F = wg.shape[2], wg.shape[3]
    n_r = 3
    tr = cr // n_r
    n_f = F // tf
    n_tiles = E * B * n_r

    def at_tile(off):
        def index_map(s, f):
            t = jnp.clip(s + off, 0, n_tiles - 1)
            return ((t // n_r) % B, t // (B * n_r), 0, 0)
        return index_map

    expert = lambda s: jnp.clip(s - 1, 0, n_tiles - 1) // (B * n_r)
    smem = lambda n, off: pl.BlockSpec((1, 1, 1, n), at_tile(off), memory_space=pltpu.SMEM)
    return pl.pallas_call(
        functools.partial(_ffn_body, cap=cap, zr=zr, n_tiles=n_tiles, n_r=n_r, n_f=n_f),
        out_shape=jax.ShapeDtypeStruct((B, zr + tr, 1, D), F32),
        grid=(n_tiles + 2, n_f),
        in_specs=[smem(cp, 0), smem(cp, -2), smem(Lp, -2), pl.BlockSpec(memory_space=pl.ANY),
                  pl.BlockSpec((1, 1, D, tf), lambda s, f: (layer, expert(s), 0, f)),
                  pl.BlockSpec((1, 1, D, tf), lambda s, f: (layer, expert(s), 0, f)),
                  pl.BlockSpec((1, 1, tf, D), lambda s, f: (layer, expert(s), f, 0))],
        out_specs=pl.BlockSpec(memory_space=pl.ANY),
        scratch_shapes=[pltpu.VMEM((tr, 1, De), F32), pltpu.VMEM((tr, De), F32), pltpu.VMEM((tr, D), BF16),
                        pltpu.VMEM((tr, D), F32),
                        pltpu.VMEM((tr, 1, D), F32), pltpu.SemaphoreType.DMA(()), pltpu.SemaphoreType.DMA(())],
        compiler_params=_cp(("arbitrary", "arbitrary"), 56),
        name="expert_ffn",
    )(idx, idx, q.reshape(B, E, 1, Lp), xe, wg, wu, wd)


def _combine_body(k0_ref, k1_ref, h_ref, lohi_ref, g_ref, z_ref, h2_ref, u_ref, buf, zb_sc, acc_sc, sem, *, nblk):
    b = pl.program_id(0)
    n = b * nblk + pl.program_id(1)
    n_total = pl.num_programs(0) * nblk
    k0 = k0_ref[n]
    k1 = k1_ref[n]
    lo = lohi_ref[0, 0:1, :]
    hi = lohi_ref[0, 1:2, :]
    acc_sc[...] = jnp.zeros(acc_sc.shape, F32)

    def copy(bb, k):
        r0 = pl.multiple_of(k * ZCHUNK, ZCHUNK)
        slot = lax.rem(k, 2)
        return pltpu.make_async_copy(z_ref.at[bb, pl.ds(r0, ZCHUNK)], buf.at[slot], sem.at[slot])

    def resident(n_prev, n_next):
        return ((n_prev // nblk == n_next // nblk) & (k1_ref[n_prev] > k0_ref[n_prev])
                & (k1_ref[n_prev] - 1 == k0_ref[n_next]))

    @pl.when(k1 > k0)
    def _():
        @pl.when(n == 0)
        def _():
            copy(b, k0).start()

        @pl.when((n == 0) | jnp.logical_not(resident(jnp.maximum(n - 1, 0), n)))
        def _():
            copy(b, k0).wait()

    def chunk(k, c):
        @pl.when(k + 1 < k1)
        def _():
            copy(b, k + 1).start()

        w = k * ZCHUNK + lax.broadcasted_iota(I32, (ZCHUNK, lo.shape[1]), 0)
        band_t = jnp.where((w >= lo) & (w < hi), 1.0, 0.0)
        zb_sc[...] = buf[lax.rem(k, 2)].reshape(zb_sc.shape)
        acc_sc[...] += _dot(band_t.T.astype(BF16), zb_sc[...].astype(BF16))

        @pl.when(k + 1 < k1)
        def _():
            copy(b, k + 1).wait()
        return c

    lax.fori_loop(k0, k1, chunk, 0)

    n_next = jnp.minimum(n + 1, n_total - 1)

    @pl.when((n + 1 < n_total) & (k1_ref[n_next] > k0_ref[n_next]) & jnp.logical_not(resident(n, n_next)))
    def _():
        copy(n_next // nblk, k0_ref[n_next]).start()

    h2 = h_ref[0] + acc_sc[...]
    h2_ref[0] = h2
    u_ref[0] = _rms(h2, g_ref[...]).astype(u_ref.dtype)


def _combine(h1, lohi, z, g, n_rows, u_dtype):
    B, Lp, D = h1.shape
    nblk = n_rows // TOKB
    lo = lohi[:, 0, :n_rows].reshape(B, nblk, TOKB)
    hi = lohi[:, 1, :n_rows].reshape(B, nblk, TOKB)
    k0 = (lo[:, :, 0] // ZCHUNK).reshape(-1).astype(I32)
    k1 = ((hi[:, :, -1] + ZCHUNK - 1) // ZCHUNK).reshape(-1).astype(I32)
    k1 = jnp.where(hi[:, :, -1].reshape(-1) > lo[:, :, 0].reshape(-1), k1, k0)
    row = lambda b, j, *_: (b, j, 0)
    grid_spec = pltpu.PrefetchScalarGridSpec(
        num_scalar_prefetch=2,
        grid=(B, nblk),
        in_specs=[pl.BlockSpec((1, TOKB, D), row),
                  pl.BlockSpec((1, 2, TOKB), lambda b, j, *_: (b, 0, j)),
                  pl.BlockSpec((1, D), lambda b, j, *_: (0, 0)),
                  pl.BlockSpec(memory_space=pl.ANY)],
        out_specs=(pl.BlockSpec((1, TOKB, D), row), pl.BlockSpec((1, TOKB, D), row)),
        scratch_shapes=[pltpu.VMEM((2, ZCHUNK, 1, D), z.dtype), pltpu.VMEM((ZCHUNK, D), F32),
                        pltpu.VMEM((TOKB, D), F32), pltpu.SemaphoreType.DMA((2,))],
    )
    return pl.pallas_call(
        functools.partial(_combine_body, nblk=nblk),
        out_shape=(jax.ShapeDtypeStruct((B, n_rows, D), F32), jax.ShapeDtypeStruct((B, n_rows, D), u_dtype)),
        grid_spec=grid_spec,
        compiler_params=_cp(("arbitrary", "arbitrary")),
        name="moe_combine",
    )(k0, k1, h1, lohi, g.reshape(1, D), z)


def _moe(h1, xe, aff_t, wg, wu, wd, layer, g_next, n_real, n_rows_out, u_dtype):
    B, Lp, D = h1.shape
    cap = (CAP_F * (n_real + N_META)) // N_EXP
    cr = -(-cap // 48) * 48
    cp = -(-(cap + 1) // LANE) * LANE
    zr = -(-(N_EXP * cap) // ZCHUNK) * ZCHUNK
    if zr == N_EXP * cap:
        zr += ZCHUNK
    idx, q, lohi = _topk(aff_t, n_real, cap, cp)
    idx = idx.reshape(B, N_EXP, 1, cp)
    z = _ffn(idx, q, xe, wg, wu, wd, layer, cap, cr, zr, min(wg.shape[3], 256))
    return _combine(h1, lohi, z, g_next, n_rows_out, u_dtype)


def _rope_tables(n_real, Lp):
    pos = jnp.concatenate([jnp.arange(n_real) + N_META, jnp.arange(N_META), jnp.zeros((Lp - n_real - N_META,), I32)])
    inv_freq = 1.0 / (ROPE_THETA ** (jnp.arange(0, MLA_ROPE, 2, dtype=F32) / MLA_ROPE))
    ang = pos.astype(F32)[:, None] * inv_freq[None, :]
    cos, sin = jnp.cos(ang), jnp.sin(ang)
    half = MLA_ROPE // 2
    z = lambda n: jnp.zeros((Lp, n), F32)
    c = jnp.concatenate([cos, cos, z(LANE - 2 * half)], axis=1)
    sn = jnp.concatenate([-sin, z(LANE - half)], axis=1)
    sp = jnp.concatenate([z(half), sin, z(LANE - 2 * half)], axis=1)
    return (c, sn, sp), (cos.T, sin.T)


def kernel(x, meta_tokens, norm_mix, norm_ffn, norm_final, ab_w_in, ab_q_norm, ab_kv_norm, ab_w_uq, ab_w_ukv, ab_rpb, ab_w_out, c_w_in, c_sinks, c_w_out, ec_w_router, ec_w_gate, ec_w_up, ec_w_down):
    B, S, D = x.shape
    depth = norm_mix.shape[0]
    Lp = S + TAIL
    tm = 512
    h, u = _embed(x, meta_tokens.astype(x.dtype), norm_mix[0], tm)
    tabs, tabs_t = _rope_tables(S, Lp)

    for layer in range(depth):
        if layer % 2 == 0:
            e = layer // 2
            w_in = ab_w_in[e].astype(BF16)
            o2 = MLA_QL + MLA_KVL
            o3 = o2 + MLA_ROPE
            lat = _mm(u, w_in[:, :o2], tm, o2)
            krope = _mm(u, jnp.pad(w_in[:, o2:o3], ((0, 0), (0, LANE - MLA_ROPE))), tm, LANE)
            na_qkv = _mm(u, w_in[:, o3:], tm, 1024)
            wq = jnp.pad(ab_w_uq[e].reshape(MLA_QL, MLA_H, MLA_NOPE + MLA_ROPE),
                         ((0, 0), (0, 0), (0, MLA_HP - MLA_NOPE - MLA_ROPE))).reshape(MLA_QL, MLA_H * MLA_HP)
            wkv = ab_w_ukv[e].reshape(MLA_KVL, MLA_H, MLA_NOPE + MLA_V)
            wk = wkv[:, :, :MLA_NOPE].reshape(MLA_KVL, -1)
            wv = wkv[:, :, MLA_NOPE:].reshape(MLA_KVL, -1)
            qt, k, vt = _mla_up(lat, krope, ab_q_norm[e], ab_kv_norm[e], wq.T.astype(BF16), wk.astype(BF16),
                                wv.T.astype(BF16), tabs, tabs_t, tm)
            y_mla = _mla_attention(qt, k, vt, S, TAIL, min(S, 512))
            y_na = _na_attention(na_qkv, _na_bias_table(ab_rpb[e]), S)
            ys = [y_mla, y_na]
            w_out = ab_w_out[e]
        else:
            o = layer // 2
            qw, kw = SWA_H * SWA_D, SWA_KV * SWA_D
            w_in = c_w_in[o]
            wq = w_in[:, :qw] * (SWA_D ** -0.5 * math.log2(math.e))
            wqv_t = jnp.concatenate([wq, w_in[:, qw + kw:]], axis=1).T.astype(BF16)
            qvt = _mm_t(u, wqv_t, tm, (qw + kw) // 3)
            k = _mm(u, w_in[:, qw:qw + kw].astype(BF16), tm, kw)
            ys = [_swa_attention(qvt, k, c_sinks[o], S)]
            w_out = c_w_out[o]
        wr = jnp.pad(ec_w_router[layer], ((0, 0), (0, LANE - N_EXP)))
        wr_hi = wr.astype(BF16)
        wr = jnp.concatenate([wr_hi, (wr - wr_hi.astype(F32)).astype(BF16)], axis=1)
        h1, xe, aff_t = _outproj(h, ys, w_out.astype(BF16), norm_ffn[layer], wr, 256, y_t=layer % 2 == 1)
        last = layer == depth - 1
        g_next = norm_final if last else norm_mix[layer + 1]
        h, u = _moe(h1, xe, aff_t, ec_w_gate, ec_w_up, ec_w_down, layer, g_next, S,
                    S if last else Lp, F32 if last else BF16)
    return u
```

```python
import functools
import math

import jax
import jax.numpy as jnp
from jax import lax
from jax.experimental import pallas as pl
from jax.experimental.pallas import tpu as pltpu

F32 = jnp.float32
BF16 = jnp.bfloat16
I32 = jnp.int32

N_META = 16
GRID_W = 64
QBLK = 128
EPS = 1e-6
NEG = -1e30

MLA_H = 8
MLA_NOPE = 128
MLA_ROPE = 64
MLA_V = 128
MLA_QL = 512
MLA_KVL = 512
MLA_HP = 256
ROPE_THETA = 10000.0

NA_H = 8
NA_D = 128
NA_KH = 8
NA_KW = 16

SWA_H = 32
SWA_KV = 4
SWA_G = SWA_H // SWA_KV
SWA_D = 64
SWA_WIN = 128

N_EXP = 16
CAP_F = 2

TAIL = 512
LANE = 128
ZCHUNK = 256
TOKB = 512


def _cp(sem, vmem_mb=48):
    return pltpu.CompilerParams(dimension_semantics=sem, vmem_limit_bytes=vmem_mb << 20)


def _dot(a, b):
    return jnp.dot(a, b, preferred_element_type=F32)


def _dot_nt(a, b):
    return lax.dot_general(a, b, (((1,), (1,)), ((), ())), preferred_element_type=F32)


def _rms(x, g):
    return x * lax.rsqrt(jnp.mean(x * x, axis=-1, keepdims=True) + EPS) * g


def _embed_body(x_ref, meta_ref, g_ref, h_ref, u_ref, *, n_blocks):
    i = pl.program_id(1)

    @pl.when(i < n_blocks)
    def _():
        h_ref[0] = x_ref[0]

    @pl.when(i >= n_blocks)
    def _():
        h_ref[0] = jnp.zeros(h_ref.shape[1:], h_ref.dtype)
        h_ref[0, 0:N_META, :] = meta_ref[...]

    u_ref[0] = _rms(h_ref[0], g_ref[...]).astype(u_ref.dtype)


def _embed(x, meta, g, tm):
    B, S, D = x.shape
    Lp = S + TAIL
    nb = S // tm
    row = lambda b, i: (b, i, 0)
    return pl.pallas_call(
        functools.partial(_embed_body, n_blocks=nb),
        out_shape=(jax.ShapeDtypeStruct((B, Lp, D), x.dtype), jax.ShapeDtypeStruct((B, Lp, D), BF16)),
        grid=(B, Lp // tm),
        in_specs=[pl.BlockSpec((1, tm, D), lambda b, i: (b, jnp.minimum(i, nb - 1), 0)),
                  pl.BlockSpec(meta.shape, lambda b, i: (0, 0)), pl.BlockSpec((1, D), lambda b, i: (0, 0))],
        out_specs=(pl.BlockSpec((1, tm, D), row), pl.BlockSpec((1, tm, D), row)),
        compiler_params=_cp(("parallel", "parallel")),
        name="embed_rmsnorm",
    )(x, meta, g.reshape(1, D))


def _mm_body(x_ref, w_ref, o_ref):
    o_ref[0] = _dot(x_ref[0], w_ref[...]).astype(o_ref.dtype)


def _mm(x, w, tm, tn):
    B, Lp, K = x.shape
    N = w.shape[1]
    return pl.pallas_call(
        _mm_body,
        out_shape=jax.ShapeDtypeStruct((B, Lp, N), BF16),
        grid=(B, Lp // tm, N // tn),
        in_specs=[pl.BlockSpec((1, tm, K), lambda b, i, j: (b, i, 0)), pl.BlockSpec((K, tn), lambda b, i, j: (0, j))],
        out_specs=pl.BlockSpec((1, tm, tn), lambda b, i, j: (b, i, j)),
        compiler_params=_cp(("parallel", "parallel", "parallel")),
        name="in_proj",
    )(x, w)


def _mm_t_body(x_ref, wt_ref, o_ref):
    o_ref[0] = _dot_nt(wt_ref[...], x_ref[0]).astype(o_ref.dtype)


def _mm_t(x, wt, tm, tn):
    B, Lp, K = x.shape
    N = wt.shape[0]
    return pl.pallas_call(
        _mm_t_body,
        out_shape=jax.ShapeDtypeStruct((B, N, Lp), BF16),
        grid=(B, Lp // tm, N // tn),
        in_specs=[pl.BlockSpec((1, tm, K), lambda b, i, j: (b, i, 0)), pl.BlockSpec((tn, K), lambda b, i, j: (j, 0))],
        out_specs=pl.BlockSpec((1, tn, tm), lambda b, i, j: (b, j, i)),
        compiler_params=_cp(("parallel", "parallel", "parallel")),
        name="in_proj_t",
    )(x, wt)


def _rope(pe, c, sn, sp):
    return pe * c + pltpu.roll(pe, 96, 1) * sn + pltpu.roll(pe, 32, 1) * sp


def _uq_body(lat_ref, g_ref, wt_ref, c_ref, s_ref, o_ref, *, scale):
    xn = _rms(lat_ref[0].astype(F32), g_ref[...]).astype(BF16)
    zt = _dot_nt(wt_ref[...], xn)
    c, s = c_ref[...], s_ref[...]
    half = MLA_ROPE // 2
    dt = o_ref.dtype
    for h in range(MLA_H):
        o = h * MLA_HP
        p1, p2, p3 = o + MLA_NOPE, o + MLA_NOPE + half, o + MLA_NOPE + MLA_ROPE
        x1, x2 = zt[p1:p2], zt[p2:p3]
        o_ref[0, o:p1, :] = (zt[o:p1] * scale).astype(dt)
        o_ref[0, p1:p2, :] = ((x1 * c - x2 * s) * scale).astype(dt)
        o_ref[0, p2:p3, :] = ((x2 * c + x1 * s) * scale).astype(dt)
        o_ref[0, p3:o + MLA_HP, :] = jnp.zeros((o + MLA_HP - p3, zt.shape[1]), dt)


def _ukv_body(lat_ref, g_ref, wk_ref, wvt_ref, kr_ref, c_ref, sn_ref, sp_ref, k_ref, vt_ref):
    xn = _rms(lat_ref[0].astype(F32), g_ref[...]).astype(BF16)
    zk = _dot(xn, wk_ref[...])
    kr = _rope(kr_ref[0].astype(F32), c_ref[...], sn_ref[...], sp_ref[...]).astype(k_ref.dtype)
    for h in range(MLA_H):
        k_ref[0, :, h * MLA_HP:h * MLA_HP + LANE] = zk[:, h * LANE:(h + 1) * LANE].astype(k_ref.dtype)
        k_ref[0, :, h * MLA_HP + LANE:(h + 1) * MLA_HP] = kr
    vt_ref[0] = _dot_nt(wvt_ref[...], xn).astype(vt_ref.dtype)


def _mla_up(lat, krope, qn, kvn, wqt, wk, wvt, tabs, tabs_t, tm):
    B, Lp, _ = lat.shape
    c, sn, sp = tabs
    ct, st = tabs_t
    tab_spec = pl.BlockSpec((tm, LANE), lambda b, i: (i, 0))
    tabt_spec = pl.BlockSpec((MLA_ROPE // 2, tm), lambda b, i: (0, i))
    scale = (MLA_NOPE + MLA_ROPE) ** -0.5 * math.log2(math.e)
    qt = pl.pallas_call(
        functools.partial(_uq_body, scale=scale),
        out_shape=jax.ShapeDtypeStruct((B, MLA_H * MLA_HP, Lp), BF16),
        grid=(B, Lp // tm),
        in_specs=[pl.BlockSpec((1, tm, MLA_QL), lambda b, i: (b, i, 0)),
                  pl.BlockSpec((1, MLA_QL), lambda b, i: (0, 0)),
                  pl.BlockSpec(wqt.shape, lambda b, i: (0, 0)),
                  tabt_spec, tabt_spec],
        out_specs=pl.BlockSpec((1, MLA_H * MLA_HP, tm), lambda b, i: (b, 0, i)),
        compiler_params=_cp(("parallel", "parallel")),
        name="mla_q_up",
    )(lat, qn.reshape(1, -1), wqt, ct, st)
    k, vt = pl.pallas_call(
        _ukv_body,
        out_shape=(jax.ShapeDtypeStruct((B, Lp, MLA_H * MLA_HP), BF16),
                   jax.ShapeDtypeStruct((B, MLA_H * MLA_V, Lp), BF16)),
        grid=(B, Lp // tm),
        in_specs=[pl.BlockSpec((1, tm, MLA_KVL), lambda b, i: (b, i, 1)),
                  pl.BlockSpec((1, MLA_KVL), lambda b, i: (0, 0)),
                  pl.BlockSpec(wk.shape, lambda b, i: (0, 0)),
                  pl.BlockSpec(wvt.shape, lambda b, i: (0, 0)),
                  pl.BlockSpec((1, tm, LANE), lambda b, i: (b, i, 0)),
                  tab_spec, tab_spec, tab_spec],
        out_specs=(pl.BlockSpec((1, tm, MLA_H * MLA_HP), lambda b, i: (b, i, 0)),
                   pl.BlockSpec((1, MLA_H * MLA_V, tm), lambda b, i: (b, 0, i))),
        compiler_params=_cp(("parallel", "parallel")),
        name="mla_kv_up",
    )(lat, kvn.reshape(1, -1), wk, wvt, krope, c, sn, sp)
    return qt, k, vt


def _mla_body(qt_ref, k_ref, vt_ref, o_ref, m_sc, acc_sc, s_sc, *, n_real, tkc, ahead):
    qt = qt_ref[0]
    tq = qt.shape[1]
    nch = n_real // tkc
    ring = s_sc.shape[0]
    m_sc[...] = jnp.full(m_sc.shape, NEG, F32)
    acc_sc[...] = jnp.zeros(acc_sc.shape, F32)

    def scores(c):
        return _dot(k_ref[0, pl.ds(pl.multiple_of(c * tkc, tkc), tkc), :], qt)

    def consume(st, vtc):
        m_prev = m_sc[...]
        m_new = jnp.maximum(m_prev, jnp.max(st, axis=0, keepdims=True))
        p = jnp.exp2((st - m_new).astype(BF16))
        alpha = jnp.exp2(m_prev - m_new)
        vte = jnp.concatenate([vtc, jnp.ones((16, vtc.shape[1]), BF16)], axis=0)
        acc_sc[...] = alpha * acc_sc[...] + _dot(vte, p)
        m_sc[...] = m_new

    def values(c):
        return vt_ref[0, :, pl.ds(pl.multiple_of(c * tkc, tkc), tkc)]

    for u in range(ahead):
        s_sc[u] = scores(u)

    def trip(c0, last):
        for u in range(ring):
            if not (last and u + ahead >= ring):
                s_sc[(u + ahead) % ring] = scores(c0 + u + ahead)
            consume(s_sc[u], values(c0 + u))

    def body(cb, carry):
        trip(cb * ring, False)
        return carry

    lax.fori_loop(0, nch // ring - 1, body, 0)
    trip(jnp.int32(nch - ring), True)
    consume(_dot(k_ref[0, n_real:n_real + N_META, :], qt), vt_ref[0, :, n_real:n_real + N_META])
    out = (acc_sc[0:MLA_V, :] / acc_sc[MLA_V:MLA_V + 1, :]).T
    i = pl.program_id(2)
    row = i * tq + lax.broadcasted_iota(I32, (tq, 1), 0)
    o_ref[0] = jnp.where(row < n_real + N_META, out, 0.0).astype(o_ref.dtype)


def _mla_attention(qt, k, vt, n_real, tq, tkc):
    B, Lp, _ = k.shape
    nch = n_real // tkc
    ring, ahead = (8, 3) if nch % 8 == 0 else ((4, 2) if nch % 4 == 0 else (2, 1))
    return pl.pallas_call(
        functools.partial(_mla_body, n_real=n_real, tkc=tkc, ahead=ahead),
        out_shape=jax.ShapeDtypeStruct((B, Lp, MLA_H * MLA_V), BF16),
        grid=(B, MLA_H, Lp // tq),
        in_specs=[pl.BlockSpec((1, MLA_HP, tq), lambda b, h, i: (b, h, i)),
                  pl.BlockSpec((1, Lp, MLA_HP), lambda b, h, i: (b, 0, h)),
                  pl.BlockSpec((1, MLA_V, Lp), lambda b, h, i: (b, h, 0))],
        out_specs=pl.BlockSpec((1, tq, MLA_V), lambda b, h, i: (b, i, h)),
        scratch_shapes=[pltpu.VMEM((1, tq), F32), pltpu.VMEM((MLA_V + 16, tq), F32),
                        pltpu.VMEM((ring, tkc, tq), F32)],
        compiler_params=_cp(("parallel", "parallel", "arbitrary")),
        name="mla_attention",
    )(qt, k, vt)


NA_GROUP = 4
NA_UNION = NA_KH + NA_GROUP


def _na_bias_table(rpb):
    col = jnp.arange(GRID_W)
    cs = jnp.clip(col - NA_KW // 2, 0, GRID_W - NA_KW)
    ok_c = (col[None, :] >= cs[:, None]) & (col[None, :] < cs[:, None] + NA_KW)
    dc = jnp.clip(col[None, :] - col[:, None], -(NA_KW - 1), NA_KW - 1) + (NA_KW - 1)
    i = jnp.arange(NA_GROUP)
    j = jnp.arange(NA_UNION)
    start = jnp.stack([0 * i, i, 0 * i + NA_UNION - NA_KH])
    d0 = jnp.stack([NA_KH - 1 - i, 0 * i + NA_KH // 2 - 1, NA_GROUP - 1 - i])
    rel = j[None, None, :] - start[:, :, None]
    ok_r = (rel >= 0) & (rel < NA_KH)
    dr = jnp.clip(d0[:, :, None] + rel, 0, 2 * NA_KH - 2)
    t = rpb[:, dr][:, :, :, :, dc]
    t = jnp.where(ok_r[None, :, :, :, None, None] & ok_c[None, None, None, None], t * math.log2(math.e), NEG)
    t = t.transpose(0, 1, 3, 5, 2, 4)
    return t.reshape(rpb.shape[0], 3, NA_UNION * GRID_W, NA_GROUP * GRID_W).astype(F32)


def _na_body(qt_ref, k_ref, vt_ref, t_ref, o_ref, s_sc, sm_sc, p_sc, pm_sc, l_sc, *, n_real, gps):
    i = pl.program_id(2)
    rows = n_real // GRID_W
    n_groups = rows // NA_GROUP
    gw = NA_GROUP * GRID_W
    km = k_ref[0, n_real:n_real + N_META, :]
    vmt = vt_ref[0, :, n_real:n_real + N_META]

    @pl.when(i < n_groups // gps)
    def _():
        def window(gg):
            g = i * gps + gg
            variant = jnp.where(g == 0, 0, jnp.where(g == n_groups - 1, 2, 1))
            u0 = jnp.where(g == 0, 0, jnp.where(g == n_groups - 1, rows - NA_UNION, g * NA_GROUP - NA_KH // 2))
            return variant, pl.multiple_of(u0 * GRID_W, NA_GROUP * GRID_W)

        for gg in range(gps):
            variant, k0 = window(gg)
            qt = qt_ref[0, :, gg * gw:(gg + 1) * gw]
            s_sc[gg] = _dot(k_ref[0, pl.ds(k0, NA_UNION * GRID_W), :], qt) + t_ref[0, variant]
            sm_sc[gg] = _dot(km, qt)
        for gg in range(gps):
            s, sm = s_sc[gg], sm_sc[gg]
            m = jnp.maximum(jnp.max(s, axis=0, keepdims=True), jnp.max(sm, axis=0, keepdims=True))
            p = jnp.exp2(s - m)
            pm = jnp.exp2(sm - m)
            l_sc[gg] = jnp.sum(p, axis=0, keepdims=True) + jnp.sum(pm, axis=0, keepdims=True)
            p_sc[gg] = p.astype(BF16)
            pm_sc[gg] = pm.astype(BF16)
        for gg in range(gps):
            _, k0 = window(gg)
            ot = (_dot(vt_ref[0, :, pl.ds(k0, NA_UNION * GRID_W)], p_sc[gg]) + _dot(vmt, pm_sc[gg])) / l_sc[gg]
            o_ref[0, :, gg * gw:(gg + 1) * gw] = ot.astype(o_ref.dtype)

    @pl.when(i >= n_groups // gps)
    def _():
        sm = _dot(km, qt_ref[0, :, 0:gw])
        pm = jnp.exp2(sm - jnp.max(sm, axis=0, keepdims=True))
        ot = _dot(vmt, pm.astype(BF16)) / jnp.sum(pm, axis=0, keepdims=True)
        lane = lax.broadcasted_iota(I32, ot.shape, 1)
        o_ref[0] = jnp.zeros(o_ref.shape[1:], o_ref.dtype)
        o_ref[0, :, 0:gw] = jnp.where(lane < N_META, ot, 0.0).astype(o_ref.dtype)


def _na_attention(qvt, k, table, n_real):
    B, Lp, _ = k.shape
    gps = TAIL // (NA_GROUP * GRID_W)
    gw, nk = NA_GROUP * GRID_W, NA_UNION * GRID_W
    tq = gps * gw
    assert n_real // GRID_W >= NA_UNION and (n_real // GRID_W) % (NA_GROUP * gps) == 0
    return pl.pallas_call(
        functools.partial(_na_body, n_real=n_real, gps=gps),
        out_shape=jax.ShapeDtypeStruct((B, NA_H * NA_D, Lp), BF16),
        grid=(B, NA_H, Lp // tq),
        in_specs=[pl.BlockSpec((1, NA_D, tq), lambda b, h, i: (b, h, i)),
                  pl.BlockSpec((1, Lp, NA_D), lambda b, h, i: (b, 0, h)),
                  pl.BlockSpec((1, NA_D, Lp), lambda b, h, i: (b, NA_H + h, 0)),
                  pl.BlockSpec((1,) + table.shape[1:], lambda b, h, i: (h, 0, 0, 0))],
        out_specs=pl.BlockSpec((1, NA_D, tq), lambda b, h, i: (b, h, i)),
        scratch_shapes=[pltpu.VMEM((gps, nk, gw), F32), pltpu.VMEM((gps, N_META, gw), F32),
                        pltpu.VMEM((gps, nk, gw), BF16), pltpu.VMEM((gps, N_META, gw), BF16),
                        pltpu.VMEM((gps, 1, gw), F32)],
        compiler_params=_cp(("parallel", "parallel", "arbitrary")),
        name="na_attention",
    )(qvt, k, qvt, table)


def _swa_bias_tables():
    a = jnp.arange(QBLK)[None, :]
    j = jnp.arange(3 * QBLK)[:, None]
    dist = jnp.abs(j - QBLK - a)
    slopes = 2.0 ** (-8.0 * jnp.arange(1, SWA_H + 1, dtype=F32) / SWA_H)
    real = jnp.where(dist <= SWA_WIN, -slopes[:, None, None] * dist.astype(F32) * math.log2(math.e), NEG)
    meta = jnp.where(N_META + jnp.arange(QBLK)[:, None] - a <= SWA_WIN, 0.0, NEG)
    return real.astype(F32), meta.astype(F32)


def _swa_body(sink_ref, qt_ref, kp_ref, kc_ref, kn_ref, km_ref, vp_ref, vc_ref, vn_ref, vm_ref, tr_ref, tm_ref, o_ref,
              *, n_real):
    i = pl.program_id(1)
    nb = n_real // QBLK
    log2e = math.log2(math.e)
    gw = SWA_G * SWA_D
    kmeta = km_ref[0, 0:N_META, :]
    vmeta_t = vm_ref[0, :, 0:N_META]

    def attend(kall, vall_t, bias_of):
        for kv in range(SWA_KV):
            tile, half = kv // 2, kv % 2
            q64 = jnp.concatenate([qt_ref[0, kv * gw + g * SWA_D:kv * gw + (g + 1) * SWA_D, :] for g in range(SWA_G)],
                                  axis=1)
            z64 = jnp.zeros_like(q64)
            qpad = jnp.concatenate([q64, z64] if half == 0 else [z64, q64], axis=0)
            st = _dot(kall[:, tile * LANE:(tile + 1) * LANE], qpad)
            sm = _dot(kmeta[:, tile * LANE:(tile + 1) * LANE], qpad)
            ps, ls = [], []
            for g in range(SWA_G):
                h = kv * SWA_G + g
                cols = slice(g * QBLK, (g + 1) * QBLK)
                s = st[:, cols] + bias_of(h)
                smg = sm[:, cols]
                sink = sink_ref[h] * log2e
                m = jnp.maximum(jnp.maximum(jnp.max(s, axis=0, keepdims=True), jnp.max(smg, axis=0, keepdims=True)),
                                sink)
                p = jnp.exp2(s - m)
                pm = jnp.exp2(smg - m)
                ls.append(jnp.sum(p, axis=0, keepdims=True) + jnp.sum(pm, axis=0, keepdims=True) + jnp.exp2(sink - m))
                ps.append((p.astype(BF16), pm.astype(BF16)))
            p_all = jnp.concatenate([p for p, _ in ps], axis=1)
            pm_all = jnp.concatenate([pm for _, pm in ps], axis=1)
            rows = slice(kv * SWA_D, (kv + 1) * SWA_D)
            ot = (_dot(vall_t[rows, :], p_all) + _dot(vmeta_t[rows, :], pm_all)) / jnp.concatenate(ls, axis=1)
            for g in range(SWA_G):
                o_ref[0, kv * gw + g * SWA_D:kv * gw + (g + 1) * SWA_D, :] = ot[:, g * QBLK:(g + 1) * QBLK].astype(
                    o_ref.dtype)

    @pl.when(i < nb)
    def _():
        kall = jnp.concatenate([kp_ref[0], kc_ref[0], kn_ref[0]], axis=0)
        vall_t = jnp.concatenate([vp_ref[0], vc_ref[0], vn_ref[0]], axis=1)
        row = lax.broadcasted_iota(I32, (3 * QBLK, QBLK), 0)
        pen = jnp.where(((i == 0) & (row < QBLK)) | ((i == nb - 1) & (row >= 2 * QBLK)), NEG, 0.0)
        attend(kall, vall_t, lambda h: tr_ref[h] + pen)

    @pl.when(i == nb)
    def _():
        attend(kn_ref[0], vn_ref[0], lambda h: tm_ref[...])
        lane = lax.broadcasted_iota(I32, o_ref.shape[1:], 1)
        o_ref[0] = jnp.where(lane < N_META, o_ref[0], jnp.zeros(o_ref.shape[1:], o_ref.dtype))

    @pl.when(i > nb)
    def _():
        o_ref[0] = jnp.zeros(o_ref.shape[1:], o_ref.dtype)


def _swa_attention(qvt, k, sinks, n_real):
    B, Lp, kw = k.shape
    nb = n_real // QBLK
    qw = SWA_H * SWA_D
    vrow = qw // kw
    prev = lambda i: jnp.clip(i - 1, 0, nb - 1)
    cur = lambda i: jnp.minimum(i, nb - 1)
    nxt = lambda i: jnp.where(i >= nb, 0, jnp.minimum(i + 1, nb - 1))
    met = lambda i: nb
    kspec = lambda f: pl.BlockSpec((1, QBLK, kw), lambda b, i: (b, f(i), 0))
    vspec = lambda f: pl.BlockSpec((1, kw, QBLK), lambda b, i: (b, vrow, f(i)))
    t_real, t_meta = _swa_bias_tables()
    return pl.pallas_call(
        functools.partial(_swa_body, n_real=n_real),
        out_shape=jax.ShapeDtypeStruct((B, qw, Lp), BF16),
        grid=(B, Lp // QBLK),
        in_specs=[pl.BlockSpec(memory_space=pltpu.SMEM),
                  pl.BlockSpec((1, qw, QBLK), lambda b, i: (b, 0, i)),
                  kspec(prev), kspec(cur), kspec(nxt), kspec(met),
                  vspec(prev), vspec(cur), vspec(nxt), vspec(met),
                  pl.BlockSpec(t_real.shape, lambda b, i: (0, 0, 0)),
                  pl.BlockSpec(t_meta.shape, lambda b, i: (0, 0))],
        out_specs=pl.BlockSpec((1, qw, QBLK), lambda b, i: (b, 0, i)),
        compiler_params=_cp(("parallel", "arbitrary")),
        name="swa_attention",
    )(sinks, qvt, k, k, k, k, qvt, qvt, qvt, qvt, t_real, t_meta)


def _outproj_body(*refs, n_y, y_t):
    h_ref = refs[0]
    y_refs = refs[1:1 + n_y]
    wo_ref, g_ref, wr_ref, h1_ref, xe_ref, aff_ref = refs[1 + n_y:]
    D = h_ref.shape[2]
    acc = h_ref[0]
    k0 = 0
    for y_ref, t in zip(y_refs, y_t):
        if t:
            kk = y_ref.shape[1]
            acc = acc + lax.dot_general(y_ref[0], wo_ref[k0:k0 + kk, :], (((0,), (0,)), ((), ())),
                                        preferred_element_type=F32)
        else:
            kk = y_ref.shape[2]
            acc = acc + _dot(y_ref[0], wo_ref[k0:k0 + kk, :])
        k0 += kk
    h1_ref[0] = acc
    xn = _rms(acc, g_ref[...])
    xh = xn.astype(BF16)
    xl = (xn - xh.astype(F32)).astype(BF16)
    hh = _dot(xh, wr_ref[...])
    logits = hh[:, 0:LANE] + hh[:, LANE:2 * LANE] + _dot(xl, wr_ref[:, 0:LANE])
    lane = lax.broadcasted_iota(I32, logits.shape, 1)
    logits = jnp.where(lane < N_EXP, logits, NEG)
    e = jnp.exp(logits - jnp.max(logits, axis=1, keepdims=True))
    aff = e / jnp.sum(e, axis=1, keepdims=True)
    tm = xn.shape[0]
    xe_ref[0, :, :, 0:D] = xn.reshape(tm, 1, D)
    xe_ref[0, :, :, D:D + LANE] = aff.reshape(tm, 1, LANE)
    aff_ref[0] = aff.T[0:N_EXP, :]


def _outproj(h, ys, wo, g, wr, tm, y_t):
    B, Lp, D = h.shape
    row = lambda b, i: (b, i, 0)
    full = lambda b, i: (0, 0)
    y_spec = lambda y, t: (pl.BlockSpec((1, y.shape[1], tm), lambda b, i: (b, 0, i)) if t
                           else pl.BlockSpec((1, tm, y.shape[2]), row))
    return pl.pallas_call(
        functools.partial(_outproj_body, n_y=len(ys), y_t=y_t),
        out_shape=(jax.ShapeDtypeStruct((B, Lp, D), F32),
                   jax.ShapeDtypeStruct((B, Lp, 1, D + LANE), F32),
                   jax.ShapeDtypeStruct((B, N_EXP, Lp), F32)),
        grid=(B, Lp // tm),
        in_specs=[pl.BlockSpec((1, tm, D), row)] + [y_spec(y, t) for y, t in zip(ys, y_t)]
        + [pl.BlockSpec(wo.shape, full), pl.BlockSpec((1, D), full), pl.BlockSpec(wr.shape, full)],
        out_specs=(pl.BlockSpec((1, tm, D), row), pl.BlockSpec((1, tm, 1, D + LANE), lambda b, i: (b, i, 0, 0)),
                   pl.BlockSpec((1, N_EXP, tm), lambda b, i: (b, 0, i))),
        compiler_params=_cp(("parallel", "parallel")),
        name="outproj_router",
    )(h, *ys, wo, g.reshape(1, D), wr)


def _topk_body(aff_ref, idx_ref, q_ref, lohi_ref, sel_sc, chunk_sc, *, n_real, cap):
    Lp = aff_ref.shape[2]
    nch = Lp // LANE
    cm = n_real // LANE
    aff = aff_ref[0]
    tok = lax.broadcasted_iota(I32, aff.shape, 1)
    keys = jnp.where(tok < n_real + N_META, pltpu.bitcast(aff, I32), -1)

    def search(it, prefix):
        cand = prefix | lax.shift_left(jnp.int32(1), 30 - it)
        cnt = jnp.sum(jnp.where(keys >= cand, 1.0, 0.0), axis=1, keepdims=True)
        return jnp.where(cnt >= cap, cand, prefix)

    thr = lax.fori_loop(0, 31, search, jnp.zeros((N_EXP, 1), I32))
    need = cap - jnp.sum(jnp.where(keys > thr, 1.0, 0.0), axis=1, keepdims=True)

    ri = lax.broadcasted_iota(I32, (LANE, LANE), 0)
    ci = lax.broadcasted_iota(I32, (LANE, LANE), 1)
    upper = (ri <= ci).astype(BF16)
    er = lax.broadcasted_iota(I32, (N_EXP, N_EXP), 0)
    ec = lax.broadcasted_iota(I32, (N_EXP, N_EXP), 1)
    lower = (ec < er).astype(BF16)

    sel_sc[...] = jnp.zeros(sel_sc.shape, F32)

    def select(c, carry):
        c0 = pl.multiple_of(c * LANE, LANE)
        a = pltpu.bitcast(aff_ref[0, :, pl.ds(c0, LANE)], I32)
        t = c0 + lax.broadcasted_iota(I32, a.shape, 1)
        kc = jnp.where(t < n_real + N_META, a, -1)
        eq = (kc == thr).astype(F32)
        rank = _dot(eq.astype(BF16), upper) - eq + carry
        sel = jnp.where((kc > thr) | ((eq > 0) & (rank < need)), 1.0, 0.0)
        sel_sc[:, pl.ds(c0, LANE)] = sel
        return carry + jnp.sum(eq, axis=1, keepdims=True)

    carry = select(jnp.int32(cm), jnp.zeros((N_EXP, 1), F32))
    lax.fori_loop(0, cm, select, carry)

    chunk_sc[...] = jnp.zeros(chunk_sc.shape, F32)

    def offsets(c, carry):
        off_c = carry
        c0 = pl.multiple_of(c * LANE, LANE)
        sel = sel_sc[:, pl.ds(c0, LANE)]
        selb = sel.astype(BF16)
        nt = jnp.sum(sel, axis=0, keepdims=True)
        off_incl = _dot(jnp.broadcast_to(nt, (8, LANE)).astype(BF16), upper)[0:1] + off_c
        r = _dot(lower, selb)
        q_ref[0, :, pl.ds(c0, LANE)] = (off_incl - nt + r).astype(I32)
        lohi_ref[0, 0:1, pl.ds(c0, LANE)] = (off_incl - nt).astype(I32)
        lohi_ref[0, 1:2, pl.ds(c0, LANE)] = off_incl.astype(I32)
        within = _dot(selb, upper)
        for e in range(N_EXP):
            chunk_sc[e, pl.ds(c, 1), :] = within[e:e + 1, :]
        return off_c + jnp.sum(nt, axis=1, keepdims=True)

    lax.fori_loop(0, nch, offsets, jnp.zeros((1, 1), F32))

    ncp = chunk_sc.shape[1]
    cpad = idx_ref.shape[2]
    cr_ = lax.broadcasted_iota(I32, (ncp, ncp), 0)
    cc_ = lax.broadcasted_iota(I32, (ncp, ncp), 1)
    lower_incl = (cc_ <= cr_).astype(BF16)
    slot = lax.broadcasted_iota(I32, (1, cpad), 1).astype(F32)
    chunk_id = lax.broadcasted_iota(I32, (ncp, cpad), 0).astype(F32)
    for e in range(N_EXP):
        within = chunk_sc[e]
        tot = jnp.broadcast_to(within[:, LANE - 1:LANE], (ncp, LANE))
        cend = _dot(lower_incl, tot.astype(BF16))
        cstart = cend - tot
        cend_t = jnp.concatenate([cend] * (cpad // LANE), axis=1)
        cstart_t = jnp.concatenate([cstart] * (cpad // LANE), axis=1)
        cstar = jnp.sum(jnp.where(cend_t <= slot, 1.0, 0.0), axis=0, keepdims=True)
        onehot = chunk_id == cstar
        srel = slot - jnp.sum(jnp.where(onehot, cstart_t, 0.0), axis=0, keepdims=True)
        g = lax.dot_general(within.astype(BF16), jnp.where(onehot, 1.0, 0.0).astype(BF16), (((0,), (0,)), ((), ())),
                            preferred_element_type=F32)
        lane_in = jnp.sum(jnp.where(g <= srel, 1.0, 0.0), axis=0, keepdims=True)
        idx = jnp.where(slot < cap, cstar * LANE + lane_in, 0.0)
        idx_ref[0, e:e + 1, :] = idx.astype(I32)


def _topk(aff_t, n_real, cap, cp):
    B, E, Lp = aff_t.shape
    spec = pl.BlockSpec((1, E, Lp), lambda b: (b, 0, 0))
    ncp = -(-(Lp // LANE) // 8) * 8
    return pl.pallas_call(
        functools.partial(_topk_body, n_real=n_real, cap=cap),
        out_shape=(jax.ShapeDtypeStruct((B, E, cp), I32), jax.ShapeDtypeStruct((B, E, Lp), I32),
                   jax.ShapeDtypeStruct((B, 2, Lp), I32)),
        grid=(B,),
        in_specs=[spec],
        out_specs=(pl.BlockSpec((1, E, cp), lambda b: (b, 0, 0)), spec, pl.BlockSpec((1, 2, Lp), lambda b: (b, 0, 0))),
        scratch_shapes=[pltpu.VMEM((E, Lp), F32), pltpu.VMEM((E, ncp, LANE), F32)],
        compiler_params=_cp(("parallel",)),
        name="expert_topk",
    )(aff_t)


def _ffn_body(idxg_ref, idxw_ref, q_ref, xe_ref, wg_ref, wu_ref, wd_ref, z_ref, xbuf, x2d_sc, xb_sc, yacc, ybuf,
              gsem, ssem, *, cap, zr, n_tiles, n_r, n_f):
    s = pl.program_id(0)
    f = pl.program_id(1)
    nf = pl.num_programs(1)
    tr, D = xb_sc.shape
    B = xe_ref.shape[0]
    total = N_EXP * cap
    per = tr // n_f

    def tile(t):
        t = jnp.clip(t, 0, n_tiles - 1)
        return t // (B * n_r), (t // n_r) % B, (t % n_r) * tr

    @pl.when(f == 0)
    def _():
        @pl.when(s == 0)
        def _():
            xb_sc[...] = jnp.zeros(xb_sc.shape, BF16)
            ybuf[...] = jnp.zeros(ybuf.shape, F32)
            for bb in range(B):
                for r0 in range(total, zr + tr, tr):
                    n = min(tr, zr + tr - r0)
                    cp = pltpu.make_async_copy(ybuf.at[0:n], z_ref.at[bb, r0:r0 + n], ssem)
                    cp.start()
                    cp.wait()

        @pl.when(s >= 1)
        def _():
            pltpu.make_async_copy(xe_ref.at[0, 0:tr], xbuf, gsem).wait()
            x2d_sc[...] = xbuf[...].reshape(x2d_sc.shape)
            xb_sc[...] = x2d_sc[:, 0:D].astype(BF16)

    x = xb_sc[...]
    g = _dot(x, wg_ref[0, 0].astype(BF16))
    u = _dot(x, wu_ref[0, 0].astype(BF16))
    hmid = (g * jax.nn.sigmoid(g) * u).astype(BF16)
    contrib = _dot(hmid, wd_ref[0, 0].astype(BF16))

    _, gb, g0 = tile(s)
    _, wb, w0 = tile(s - 2)
    for i in range(per):
        j = f * per + i
        t = idxg_ref[0, 0, 0, g0 + j]
        pltpu.make_async_copy(xe_ref.at[gb, t], xbuf.at[j], gsem).start()
        live = (s >= 2) & (w0 + j < cap)
        dst = jnp.where(live, q_ref[0, 0, 0, idxw_ref[0, 0, 0, w0 + j]], zr + j)
        pltpu.make_async_copy(ybuf.at[j], z_ref.at[wb, dst], ssem).start()

    @pl.when(f == 0)
    def _():
        yacc[...] = contrib

    @pl.when(f > 0)
    def _():
        yacc[...] += contrib

    @pl.when(f == nf - 1)
    def _():
        pltpu.make_async_copy(ybuf, z_ref.at[0, 0:tr], ssem).wait()

        @pl.when(s >= 1)
        def _():
            e, _, _ = tile(s - 1)
            aff = x2d_sc[:, D:D + LANE]
            lane = lax.broadcasted_iota(I32, aff.shape, 1)
            gate = jnp.sum(jnp.where(lane == e, aff, 0.0), axis=1, keepdims=True)
            ybuf[...] = (yacc[...] * gate).reshape(tr, 1, D)

        @pl.when(s == pl.num_programs(0) - 1)
        def _():
            pltpu.make_async_copy(xe_ref.at[0, 0:tr], xbuf, gsem).wait()


def _ffn(idx, q, xe, wg, wu, wd, layer, cap, cr, zr, tf):
    B, E, _, cp = idx.shape
    Lp = q.shape[2]
    De = xe.shape[3]
    D, F = wg.shape[2], wg.shape[3]
    n_r = 3
    tr = cr // n_r
    n_f = F // tf
    n_tiles = E * B * n_r

    def at_tile(off):
        def index_map(s, f):
            t = jnp.clip(s + off, 0, n_tiles - 1)
            return ((t // n_r) % B, t // (B * n_r), 0, 0)
        return index_map

    expert = lambda s: jnp.clip(s - 1, 0, n_tiles - 1) // (B * n_r)
    smem = lambda n, off: pl.BlockSpec((1, 1, 1, n), at_tile(off), memory_space=pltpu.SMEM)
    return pl.pallas_call(
        functools.partial(_ffn_body, cap=cap, zr=zr, n_tiles=n_tiles, n_r=n_r, n_f=n_f),
        out_shape=jax.ShapeDtypeStruct((B, zr + tr, 1, D), F32),
        grid=(n_tiles + 2, n_f),
        in_specs=[smem(cp, 0), smem(cp, -2), smem(Lp, -2), pl.BlockSpec(memory_space=pl.ANY),
                  pl.BlockSpec((1, 1, D, tf), lambda s, f: (layer, expert(s), 0, f)),
                  pl.BlockSpec((1, 1, D, tf), lambda s, f: (layer, expert(s), 0, f)),
                  pl.BlockSpec((1, 1, tf, D), lambda s, f: (layer, expert(s), f, 0))],
        out_specs=pl.BlockSpec(memory_space=pl.ANY),
        scratch_shapes=[pltpu.VMEM((tr, 1, De), F32), pltpu.VMEM((tr, De), F32), pltpu.VMEM((tr, D), BF16),
                        pltpu.VMEM((tr, D), F32),
                        pltpu.VMEM((tr, 1, D), F32), pltpu.SemaphoreType.DMA(()), pltpu.SemaphoreType.DMA(())],
        compiler_params=_cp(("arbitrary", "arbitrary"), 56),
        name="expert_ffn",
    )(idx, idx, q.reshape(B, E, 1, Lp), xe, wg, wu, wd)


def _combine_body(k0_ref, k1_ref, h_ref, lohi_ref, g_ref, z_ref, h2_ref, u_ref, buf, zb_sc, acc_sc, sem, *, nblk):
    b = pl.program_id(0)
    n = b * nblk + pl.program_id(1)
    n_total = pl.num_programs(0) * nblk
    k0 = k0_ref[n]
    k1 = k1_ref[n]
    lo = lohi_ref[0, 0:1, :]
    hi = lohi_ref[0, 1:2, :]
    acc_sc[...] = jnp.zeros(acc_sc.shape, F32)

    def copy(bb, k):
        r0 = pl.multiple_of(k * ZCHUNK, ZCHUNK)
        slot = lax.rem(k, 2)
        return pltpu.make_async_copy(z_ref.at[bb, pl.ds(r0, ZCHUNK)], buf.at[slot], sem.at[slot])

    def resident(n_prev, n_next):
        return ((n_prev // nblk == n_next // nblk) & (k1_ref[n_prev] > k0_ref[n_prev])
                & (k1_ref[n_prev] - 1 == k0_ref[n_next]))

    @pl.when(k1 > k0)
    def _():
        @pl.when(n == 0)
        def _():
            copy(b, k0).start()

        @pl.when((n == 0) | jnp.logical_not(resident(jnp.maximum(n - 1, 0), n)))
        def _():
            copy(b, k0).wait()

    def chunk(k, c):
        @pl.when(k + 1 < k1)
        def _():
            copy(b, k + 1).start()

        w = k * ZCHUNK + lax.broadcasted_iota(I32, (ZCHUNK, lo.shape[1]), 0)
        band_t = jnp.where((w >= lo) & (w < hi), 1.0, 0.0)
        zb_sc[...] = buf[lax.rem(k, 2)].reshape(zb_sc.shape)
        acc_sc[...] += _dot(band_t.T.astype(BF16), zb_sc[...].astype(BF16))

        @pl.when(k + 1 < k1)
        def _():
            copy(b, k + 1).wait()
        return c

    lax.fori_loop(k0, k1, chunk, 0)

    n_next = jnp.minimum(n + 1, n_total - 1)

    @pl.when((n + 1 < n_total) & (k1_ref[n_next] > k0_ref[n_next]) & jnp.logical_not(resident(n, n_next)))
    def _():
        copy(n_next // nblk, k0_ref[n_next]).start()

    h2 = h_ref[0] + acc_sc[...]
    h2_ref[0] = h2
    u_ref[0] = _rms(h2, g_ref[...]).astype(u_ref.dtype)


def _combine(h1, lohi, z, g, n_rows, u_dtype):
    B, Lp, D = h1.shape
    nblk = n_rows // TOKB
    lo = lohi[:, 0, :n_rows].reshape(B, nblk, TOKB)
    hi = lohi[:, 1, :n_rows].reshape(B, nblk, TOKB)
    k0 = (lo[:, :, 0] // ZCHUNK).reshape(-1).astype(I32)
    k1 = ((hi[:, :, -1] + ZCHUNK - 1) // ZCHUNK).reshape(-1).astype(I32)
    k1 = jnp.where(hi[:, :, -1].reshape(-1) > lo[:, :, 0].reshape(-1), k1, k0)
    row = lambda b, j, *_: (b, j, 0)
    grid_spec = pltpu.PrefetchScalarGridSpec(
        num_scalar_prefetch=2,
        grid=(B, nblk),
        in_specs=[pl.BlockSpec((1, TOKB, D), row),
                  pl.BlockSpec((1, 2, TOKB), lambda b, j, *_: (b, 0, j)),
                  pl.BlockSpec((1, D), lambda b, j, *_: (0, 0)),
                  pl.BlockSpec(memory_space=pl.ANY)],
        out_specs=(pl.BlockSpec((1, TOKB, D), row), pl.BlockSpec((1, TOKB, D), row)),
        scratch_shapes=[pltpu.VMEM((2, ZCHUNK, 1, D), z.dtype), pltpu.VMEM((ZCHUNK, D), F32),
                        pltpu.VMEM((TOKB, D), F32), pltpu.SemaphoreType.DMA((2,))],
    )
    return pl.pallas_call(
        functools.partial(_combine_body, nblk=nblk),
        out_shape=(jax.ShapeDtypeStruct((B, n_rows, D), F32), jax.ShapeDtypeStruct((B, n_rows, D), u_dtype)),
        grid_spec=grid_spec,
        compiler_params=_cp(("arbitrary", "arbitrary")),
        name="moe_combine",
    )(k0, k1, h1, lohi, g.reshape(1, D), z)


def _moe(h1, xe, aff_t, wg, wu, wd, layer, g_next, n_real, n_rows_out, u_dtype):
    B, Lp, D = h1.shape
    cap = (CAP_F * (n_real + N_META)) // N_EXP
    cr = -(-cap // 48) * 48
    cp = -(-(cap + 1) // LANE) * LANE
    zr = -(-(N_EXP * cap) // ZCHUNK) * ZCHUNK
    if zr == N_EXP * cap:
        zr += ZCHUNK
    idx, q, lohi = _topk(aff_t, n_real, cap, cp)
    idx = idx.reshape(B, N_EXP, 1, cp)
    z = _ffn(idx, q, xe, wg, wu, wd, layer, cap, cr, zr, min(wg.shape[3], 256))
    return _combine(h1, lohi, z, g_next, n_rows_out, u_dtype)


def _rope_tables(n_real, Lp):
    pos = jnp.concatenate([jnp.arange(n_real) + N_META, jnp.arange(N_META), jnp.zeros((Lp - n_real - N_META,), I32)])
    inv_freq = 1.0 / (ROPE_THETA ** (jnp.arange(0, MLA_ROPE, 2, dtype=F32) / MLA_ROPE))
    ang = pos.astype(F32)[:, None] * inv_freq[None, :]
    cos, sin = jnp.cos(ang), jnp.sin(ang)
    half = MLA_ROPE // 2
    z = lambda n: jnp.zeros((Lp, n), F32)
    c = jnp.concatenate([cos, cos, z(LANE - 2 * half)], axis=1)
    sn = jnp.concatenate([-sin, z(LANE - half)], axis=1)
    sp = jnp.concatenate([z(half), sin, z(LANE - 2 * half)], axis=1)
    return (c, sn, sp), (cos.T, sin.T)


def kernel(x, meta_tokens, norm_mix, norm_ffn, norm_final, ab_w_in, ab_q_norm, ab_kv_norm, ab_w_uq, ab_w_ukv, ab_rpb, ab_w_out, c_w_in, c_sinks, c_w_out, ec_w_router, ec_w_gate, ec_w_up, ec_w_down):
    B, S, D = x.shape
    depth = norm_mix.shape[0]
    Lp = S + TAIL
    tm = 512
    h, u = _embed(x, meta_tokens.astype(x.dtype), norm_mix[0], tm)
    tabs, tabs_t = _rope_tables(S, Lp)

    for layer in range(depth):
        if layer % 2 == 0:
            e = layer // 2
            w_in = ab_w_in[e].astype(BF16)
            o2 = MLA_QL + MLA_KVL
            o3 = o2 + MLA_ROPE
            lat = _mm(u, w_in[:, :o2], tm, o2)
            krope = _mm(u, jnp.pad(w_in[:, o2:o3], ((0, 0), (0, LANE - MLA_ROPE))), tm, LANE)
            nw = NA_H * NA_D
            na_wq = ab_w_in[e][:, o3:o3 + nw] * (NA_D ** -0.5 * math.log2(math.e))
            na_wqv_t = jnp.concatenate([na_wq, ab_w_in[e][:, o3 + 2 * nw:]], axis=1).T.astype(BF16)
            na_qvt = _mm_t(u, na_wqv_t, tm, nw)
            na_k = _mm(u, w_in[:, o3 + nw:o3 + 2 * nw], tm, nw)
            wq = jnp.pad(ab_w_uq[e].reshape(MLA_QL, MLA_H, MLA_NOPE + MLA_ROPE),
                         ((0, 0), (0, 0), (0, MLA_HP - MLA_NOPE - MLA_ROPE))).reshape(MLA_QL, MLA_H * MLA_HP)
            wkv = ab_w_ukv[e].reshape(MLA_KVL, MLA_H, MLA_NOPE + MLA_V)
            wk = wkv[:, :, :MLA_NOPE].reshape(MLA_KVL, -1)
            wv = wkv[:, :, MLA_NOPE:].reshape(MLA_KVL, -1)
            qt, k, vt = _mla_up(lat, krope, ab_q_norm[e], ab_kv_norm[e], wq.T.astype(BF16), wk.astype(BF16),
                                wv.T.astype(BF16), tabs, tabs_t, tm)
            y_mla = _mla_attention(qt, k, vt, S, TAIL, min(S, 512))
            y_na = _na_attention(na_qvt, na_k, _na_bias_table(ab_rpb[e]), S)
            ys, y_t = [y_mla, y_na], (False, True)
            w_out = ab_w_out[e]
        else:
            o = layer // 2
            qw, kw = SWA_H * SWA_D, SWA_KV * SWA_D
            w_in = c_w_in[o]
            wq = w_in[:, :qw] * (SWA_D ** -0.5 * math.log2(math.e))
            wqv_t = jnp.concatenate([wq, w_in[:, qw + kw:]], axis=1).T.astype(BF16)
            qvt = _mm_t(u, wqv_t, tm, (qw + kw) // 3)
            k = _mm(u, w_in[:, qw:qw + kw].astype(BF16), tm, kw)
            ys, y_t = [_swa_attention(qvt, k, c_sinks[o], S)], (True,)
            w_out = c_w_out[o]
        wr = jnp.pad(ec_w_router[layer], ((0, 0), (0, LANE - N_EXP)))
        wr_hi = wr.astype(BF16)
        wr = jnp.concatenate([wr_hi, (wr - wr_hi.astype(F32)).astype(BF16)], axis=1)
        h1, xe, aff_t = _outproj(h, ys, w_out.astype(BF16), norm_ffn[layer], wr, 256, y_t)
        last = layer == depth - 1
        g_next = norm_final if last else norm_mix[layer + 1]
        h, u = _moe(h1, xe, aff_t, ec_w_gate, ec_w_up, ec_w_down, layer, g_next, S,
                    S if last else Lp, F32 if last else BF16)
    return u
```

```python
import functools
import math

import jax
import jax.numpy as jnp
from jax import lax
from jax.experimental import pallas as pl
from jax.experimental.pallas import tpu as pltpu

F32 = jnp.float32
BF16 = jnp.bfloat16
I32 = jnp.int32

N_META = 16
GRID_W = 64
QBLK = 128
EPS = 1e-6
NEG = -1e30

MLA_H = 8
MLA_NOPE = 128
MLA_ROPE = 64
MLA_V = 128
MLA_QL = 512
MLA_KVL = 512
MLA_HP = 256
ROPE_THETA = 10000.0

NA_H = 8
NA_D = 128
NA_KH = 8
NA_KW = 16

SWA_H = 32
SWA_KV = 4
SWA_G = SWA_H // SWA_KV
SWA_D = 64
SWA_WIN = 128

N_EXP = 16
CAP_F = 2

TAIL = 512
LANE = 128
ZCHUNK = 256
TOKB = 512


def _cp(sem, vmem_mb=48):
    return pltpu.CompilerParams(dimension_semantics=sem, vmem_limit_bytes=vmem_mb << 20)


def _dot(a, b):
    return jnp.dot(a, b, preferred_element_type=F32)


def _dot_nt(a, b):
    return lax.dot_general(a, b, (((1,), (1,)), ((), ())), preferred_element_type=F32)


def _rms(x, g):
    return x * lax.rsqrt(jnp.mean(x * x, axis=-1, keepdims=True) + EPS) * g


def _embed_body(x_ref, meta_ref, g_ref, h_ref, u_ref, *, n_blocks):
    i = pl.program_id(1)

    @pl.when(i < n_blocks)
    def _():
        h_ref[0] = x_ref[0]

    @pl.when(i >= n_blocks)
    def _():
        h_ref[0] = jnp.zeros(h_ref.shape[1:], h_ref.dtype)
        h_ref[0, 0:N_META, :] = meta_ref[...]

    u_ref[0] = _rms(h_ref[0], g_ref[...]).astype(u_ref.dtype)


def _embed(x, meta, g, tm):
    B, S, D = x.shape
    Lp = S + TAIL
    nb = S // tm
    row = lambda b, i: (b, i, 0)
    return pl.pallas_call(
        functools.partial(_embed_body, n_blocks=nb),
        out_shape=(jax.ShapeDtypeStruct((B, Lp, D), x.dtype), jax.ShapeDtypeStruct((B, Lp, D), BF16)),
        grid=(B, Lp // tm),
        in_specs=[pl.BlockSpec((1, tm, D), lambda b, i: (b, jnp.minimum(i, nb - 1), 0)),
                  pl.BlockSpec(meta.shape, lambda b, i: (0, 0)), pl.BlockSpec((1, D), lambda b, i: (0, 0))],
        out_specs=(pl.BlockSpec((1, tm, D), row), pl.BlockSpec((1, tm, D), row)),
        compiler_params=_cp(("parallel", "parallel")),
        name="embed_rmsnorm",
    )(x, meta, g.reshape(1, D))


def _mm_body(x_ref, w_ref, o_ref):
    o_ref[0] = _dot(x_ref[0], w_ref[...]).astype(o_ref.dtype)


def _mm(x, w, tm, tn):
    B, Lp, K = x.shape
    N = w.shape[1]
    return pl.pallas_call(
        _mm_body,
        out_shape=jax.ShapeDtypeStruct((B, Lp, N), BF16),
        grid=(B, Lp // tm, N // tn),
        in_specs=[pl.BlockSpec((1, tm, K), lambda b, i, j: (b, i, 0)), pl.BlockSpec((K, tn), lambda b, i, j: (0, j))],
        out_specs=pl.BlockSpec((1, tm, tn), lambda b, i, j: (b, i, j)),
        compiler_params=_cp(("parallel", "parallel", "parallel")),
        name="in_proj",
    )(x, w)


def _mm_t_body(x_ref, wt_ref, o_ref):
    o_ref[0] = _dot_nt(wt_ref[...], x_ref[0]).astype(o_ref.dtype)


def _mm_t(x, wt, tm, tn):
    B, Lp, K = x.shape
    N = wt.shape[0]
    return pl.pallas_call(
        _mm_t_body,
        out_shape=jax.ShapeDtypeStruct((B, N, Lp), BF16),
        grid=(B, Lp // tm, N // tn),
        in_specs=[pl.BlockSpec((1, tm, K), lambda b, i, j: (b, i, 0)), pl.BlockSpec((tn, K), lambda b, i, j: (j, 0))],
        out_specs=pl.BlockSpec((1, tn, tm), lambda b, i, j: (b, j, i)),
        compiler_params=_cp(("parallel", "parallel", "parallel")),
        name="in_proj_t",
    )(x, wt)


def _rope(pe, c, sn, sp):
    return pe * c + pltpu.roll(pe, 96, 1) * sn + pltpu.roll(pe, 32, 1) * sp


def _uq_body(lat_ref, g_ref, wt_ref, c_ref, s_ref, o_ref, *, scale):
    xn = _rms(lat_ref[0].astype(F32), g_ref[...]).astype(BF16)
    zt = _dot_nt(wt_ref[...], xn)
    c, s = c_ref[...], s_ref[...]
    half = MLA_ROPE // 2
    dt = o_ref.dtype
    for h in range(MLA_H):
        o = h * MLA_HP
        p1, p2, p3 = o + MLA_NOPE, o + MLA_NOPE + half, o + MLA_NOPE + MLA_ROPE
        x1, x2 = zt[p1:p2], zt[p2:p3]
        o_ref[0, o:p1, :] = (zt[o:p1] * scale).astype(dt)
        o_ref[0, p1:p2, :] = ((x1 * c - x2 * s) * scale).astype(dt)
        o_ref[0, p2:p3, :] = ((x2 * c + x1 * s) * scale).astype(dt)
        o_ref[0, p3:o + MLA_HP, :] = jnp.zeros((o + MLA_HP - p3, zt.shape[1]), dt)


def _ukv_body(lat_ref, g_ref, wk_ref, wvt_ref, kr_ref, c_ref, sn_ref, sp_ref, k_ref, vt_ref):
    xn = _rms(lat_ref[0].astype(F32), g_ref[...]).astype(BF16)
    zk = _dot(xn, wk_ref[...])
    kr = _rope(kr_ref[0].astype(F32), c_ref[...], sn_ref[...], sp_ref[...]).astype(k_ref.dtype)
    for h in range(MLA_H):
        k_ref[0, :, h * MLA_HP:h * MLA_HP + LANE] = zk[:, h * LANE:(h + 1) * LANE].astype(k_ref.dtype)
        k_ref[0, :, h * MLA_HP + LANE:(h + 1) * MLA_HP] = kr
    vt_ref[0] = _dot_nt(wvt_ref[...], xn).astype(vt_ref.dtype)


def _mla_up(lat, krope, qn, kvn, wqt, wk, wvt, tabs, tabs_t, tm):
    B, Lp, _ = lat.shape
    c, sn, sp = tabs
    ct, st = tabs_t
    tab_spec = pl.BlockSpec((tm, LANE), lambda b, i: (i, 0))
    tabt_spec = pl.BlockSpec((MLA_ROPE // 2, tm), lambda b, i: (0, i))
    scale = (MLA_NOPE + MLA_ROPE) ** -0.5 * math.log2(math.e)
    qt = pl.pallas_call(
        functools.partial(_uq_body, scale=scale),
        out_shape=jax.ShapeDtypeStruct((B, MLA_H * MLA_HP, Lp), BF16),
        grid=(B, Lp // tm),
        in_specs=[pl.BlockSpec((1, tm, MLA_QL), lambda b, i: (b, i, 0)),
                  pl.BlockSpec((1, MLA_QL), lambda b, i: (0, 0)),
                  pl.BlockSpec(wqt.shape, lambda b, i: (0, 0)),
                  tabt_spec, tabt_spec],
        out_specs=pl.BlockSpec((1, MLA_H * MLA_HP, tm), lambda b, i: (b, 0, i)),
        compiler_params=_cp(("parallel", "parallel")),
        name="mla_q_up",
    )(lat, qn.reshape(1, -1), wqt, ct, st)
    k, vt = pl.pallas_call(
        _ukv_body,
        out_shape=(jax.ShapeDtypeStruct((B, Lp, MLA_H * MLA_HP), BF16),
                   jax.ShapeDtypeStruct((B, MLA_H * MLA_V, Lp), BF16)),
        grid=(B, Lp // tm),
        in_specs=[pl.BlockSpec((1, tm, MLA_KVL), lambda b, i: (b, i, 1)),
                  pl.BlockSpec((1, MLA_KVL), lambda b, i: (0, 0)),
                  pl.BlockSpec(wk.shape, lambda b, i: (0, 0)),
                  pl.BlockSpec(wvt.shape, lambda b, i: (0, 0)),
                  pl.BlockSpec((1, tm, LANE), lambda b, i: (b, i, 0)),
                  tab_spec, tab_spec, tab_spec],
        out_specs=(pl.BlockSpec((1, tm, MLA_H * MLA_HP), lambda b, i: (b, i, 0)),
                   pl.BlockSpec((1, MLA_H * MLA_V, tm), lambda b, i: (b, 0, i))),
        compiler_params=_cp(("parallel", "parallel")),
        name="mla_kv_up",
    )(lat, kvn.reshape(1, -1), wk, wvt, krope, c, sn, sp)
    return qt, k, vt


def _mla_body(qt_ref, k_ref, vt_ref, o_ref, m_sc, acc_sc, s_sc, mx_sc, *, n_real, tkc, ahead):
    qt = qt_ref[0]
    tq = qt.shape[1]
    nch = n_real // tkc
    ring = s_sc.shape[0]
    m_sc[...] = jnp.full(m_sc.shape, NEG, F32)
    acc_sc[...] = jnp.zeros(acc_sc.shape, F32)

    def scores(slot, c):
        st = _dot(k_ref[0, pl.ds(pl.multiple_of(c * tkc, tkc), tkc), :], qt)
        s_sc[slot] = st
        mx_sc[slot] = jnp.max(st, axis=0, keepdims=True)

    def consume(st, mx, vtc):
        m_prev = m_sc[...]
        m_new = jnp.maximum(m_prev, mx)
        p = jnp.exp2((st - m_new).astype(BF16))
        alpha = jnp.exp2(m_prev - m_new)
        vte = jnp.concatenate([vtc, jnp.ones((16, vtc.shape[1]), BF16)], axis=0)
        acc_sc[...] = alpha * acc_sc[...] + _dot(vte, p)
        m_sc[...] = m_new

    def values(c):
        return vt_ref[0, :, pl.ds(pl.multiple_of(c * tkc, tkc), tkc)]

    for u in range(ahead):
        scores(u, u)

    def trip(c0, last):
        for u in range(ring):
            if not (last and u + ahead >= ring):
                scores((u + ahead) % ring, c0 + u + ahead)
            consume(s_sc[u], mx_sc[u], values(c0 + u))

    def body(cb, carry):
        trip(cb * ring, False)
        return carry

    lax.fori_loop(0, nch // ring - 1, body, 0)
    trip(jnp.int32(nch - ring), True)
    st_meta = _dot(k_ref[0, n_real:n_real + N_META, :], qt)
    consume(st_meta, jnp.max(st_meta, axis=0, keepdims=True), vt_ref[0, :, n_real:n_real + N_META])
    out = (acc_sc[0:MLA_V, :] / acc_sc[MLA_V:MLA_V + 1, :]).T
    i = pl.program_id(2)
    row = i * tq + lax.broadcasted_iota(I32, (tq, 1), 0)
    o_ref[0] = jnp.where(row < n_real + N_META, out, 0.0).astype(o_ref.dtype)


def _mla_attention(qt, k, vt, n_real, tq, tkc):
    B, Lp, _ = k.shape
    nch = n_real // tkc
    ring, ahead = (8, 3) if nch % 8 == 0 else ((4, 2) if nch % 4 == 0 else (2, 1))
    return pl.pallas_call(
        functools.partial(_mla_body, n_real=n_real, tkc=tkc, ahead=ahead),
        out_shape=jax.ShapeDtypeStruct((B, Lp, MLA_H * MLA_V), BF16),
        grid=(B, MLA_H, Lp // tq),
        in_specs=[pl.BlockSpec((1, MLA_HP, tq), lambda b, h, i: (b, h, i)),
                  pl.BlockSpec((1, Lp, MLA_HP), lambda b, h, i: (b, 0, h)),
                  pl.BlockSpec((1, MLA_V, Lp), lambda b, h, i: (b, h, 0))],
        out_specs=pl.BlockSpec((1, tq, MLA_V), lambda b, h, i: (b, i, h)),
        scratch_shapes=[pltpu.VMEM((1, tq), F32), pltpu.VMEM((MLA_V + 16, tq), F32),
                        pltpu.VMEM((ring, tkc, tq), F32), pltpu.VMEM((ring, 1, tq), F32)],
        compiler_params=_cp(("parallel", "parallel", "arbitrary")),
        name="mla_attention",
    )(qt, k, vt)


NA_GROUP = 4
NA_UNION = NA_KH + NA_GROUP


def _na_bias_table(rpb):
    col = jnp.arange(GRID_W)
    cs = jnp.clip(col - NA_KW // 2, 0, GRID_W - NA_KW)
    ok_c = (col[None, :] >= cs[:, None]) & (col[None, :] < cs[:, None] + NA_KW)
    dc = jnp.clip(col[None, :] - col[:, None], -(NA_KW - 1), NA_KW - 1) + (NA_KW - 1)
    i = jnp.arange(NA_GROUP)
    j = jnp.arange(NA_UNION)
    start = jnp.stack([0 * i, i, 0 * i + NA_UNION - NA_KH])
    d0 = jnp.stack([NA_KH - 1 - i, 0 * i + NA_KH // 2 - 1, NA_GROUP - 1 - i])
    rel = j[None, None, :] - start[:, :, None]
    ok_r = (rel >= 0) & (rel < NA_KH)
    dr = jnp.clip(d0[:, :, None] + rel, 0, 2 * NA_KH - 2)
    t = rpb[:, dr][:, :, :, :, dc]
    t = jnp.where(ok_r[None, :, :, :, None, None] & ok_c[None, None, None, None], t * math.log2(math.e), NEG)
    t = t.transpose(0, 1, 3, 5, 2, 4)
    return t.reshape(rpb.shape[0], 3, NA_UNION * GRID_W, NA_GROUP * GRID_W).astype(F32)


def _na_body(qt_ref, k_ref, vt_ref, t_ref, o_ref, s_sc, sm_sc, p_sc, pm_sc, l_sc, *, n_real, gps):
    i = pl.program_id(2)
    rows = n_real // GRID_W
    n_groups = rows // NA_GROUP
    gw = NA_GROUP * GRID_W
    km = k_ref[0, n_real:n_real + N_META, :]
    vmt = vt_ref[0, :, n_real:n_real + N_META]

    @pl.when(i < n_groups // gps)
    def _():
        def window(gg):
            g = i * gps + gg
            variant = jnp.where(g == 0, 0, jnp.where(g == n_groups - 1, 2, 1))
            u0 = jnp.where(g == 0, 0, jnp.where(g == n_groups - 1, rows - NA_UNION, g * NA_GROUP - NA_KH // 2))
            return variant, pl.multiple_of(u0 * GRID_W, NA_GROUP * GRID_W)

        for gg in range(gps):
            variant, k0 = window(gg)
            qt = qt_ref[0, :, gg * gw:(gg + 1) * gw]
            s_sc[gg] = _dot(k_ref[0, pl.ds(k0, NA_UNION * GRID_W), :], qt) + t_ref[0, variant]
            sm_sc[gg] = _dot(km, qt)
        for gg in range(gps):
            s, sm = s_sc[gg], sm_sc[gg]
            m = jnp.maximum(jnp.max(s, axis=0, keepdims=True), jnp.max(sm, axis=0, keepdims=True))
            p = jnp.exp2(s - m)
            pm = jnp.exp2(sm - m)
            l_sc[gg] = jnp.sum(p, axis=0, keepdims=True) + jnp.sum(pm, axis=0, keepdims=True)
            p_sc[gg] = p.astype(BF16)
            pm_sc[gg] = pm.astype(BF16)
        for gg in range(gps):
            _, k0 = window(gg)
            ot = (_dot(vt_ref[0, :, pl.ds(k0, NA_UNION * GRID_W)], p_sc[gg]) + _dot(vmt, pm_sc[gg])) / l_sc[gg]
            o_ref[0, :, gg * gw:(gg + 1) * gw] = ot.astype(o_ref.dtype)

    @pl.when(i >= n_groups // gps)
    def _():
        sm = _dot(km, qt_ref[0, :, 0:gw])
        pm = jnp.exp2(sm - jnp.max(sm, axis=0, keepdims=True))
        ot = _dot(vmt, pm.astype(BF16)) / jnp.sum(pm, axis=0, keepdims=True)
        lane = lax.broadcasted_iota(I32, ot.shape, 1)
        o_ref[0] = jnp.zeros(o_ref.shape[1:], o_ref.dtype)
        o_ref[0, :, 0:gw] = jnp.where(lane < N_META, ot, 0.0).astype(o_ref.dtype)


def _na_attention(qvt, k, table, n_real):
    B, Lp, _ = k.shape
    gps = TAIL // (NA_GROUP * GRID_W)
    gw, nk = NA_GROUP * GRID_W, NA_UNION * GRID_W
    tq = gps * gw
    assert n_real // GRID_W >= NA_UNION and (n_real // GRID_W) % (NA_GROUP * gps) == 0
    return pl.pallas_call(
        functools.partial(_na_body, n_real=n_real, gps=gps),
        out_shape=jax.ShapeDtypeStruct((B, NA_H * NA_D, Lp), BF16),
        grid=(B, NA_H, Lp // tq),
        in_specs=[pl.BlockSpec((1, NA_D, tq), lambda b, h, i: (b, h, i)),
                  pl.BlockSpec((1, Lp, NA_D), lambda b, h, i: (b, 0, h)),
                  pl.BlockSpec((1, NA_D, Lp), lambda b, h, i: (b, NA_H + h, 0)),
                  pl.BlockSpec((1,) + table.shape[1:], lambda b, h, i: (h, 0, 0, 0))],
        out_specs=pl.BlockSpec((1, NA_D, tq), lambda b, h, i: (b, h, i)),
        scratch_shapes=[pltpu.VMEM((gps, nk, gw), F32), pltpu.VMEM((gps, N_META, gw), F32),
                        pltpu.VMEM((gps, nk, gw), BF16), pltpu.VMEM((gps, N_META, gw), BF16),
                        pltpu.VMEM((gps, 1, gw), F32)],
        compiler_params=_cp(("parallel", "parallel", "arbitrary")),
        name="na_attention",
    )(qvt, k, qvt, table)


def _swa_bias_tables():
    a = jnp.arange(QBLK)[None, :]
    j = jnp.arange(3 * QBLK)[:, None]
    dist = jnp.abs(j - QBLK - a)
    slopes = 2.0 ** (-8.0 * jnp.arange(1, SWA_H + 1, dtype=F32) / SWA_H)
    real = jnp.where(dist <= SWA_WIN, -slopes[:, None, None] * dist.astype(F32) * math.log2(math.e), NEG)
    meta = jnp.where(N_META + jnp.arange(QBLK)[:, None] - a <= SWA_WIN, 0.0, NEG)
    return real.astype(F32), meta.astype(F32)


def _swa_body(sink_ref, qt_ref, kp_ref, kc_ref, kn_ref, km_ref, vp_ref, vc_ref, vn_ref, vm_ref, tr_ref, tm_ref, o_ref,
              st_sc, sm_sc, p_sc, pm_sc, l_sc, *, n_real):
    i = pl.program_id(1)
    nb = n_real // QBLK
    log2e = math.log2(math.e)
    gw = SWA_G * SWA_D
    kmeta = km_ref[0, 0:N_META, :]
    vmeta_t = vm_ref[0, :, 0:N_META]

    def attend(kall, vall_t, bias_of):
        nk = kall.shape[0]
        for kv in range(SWA_KV):
            tile, half = kv // 2, kv % 2
            q64 = jnp.concatenate([qt_ref[0, kv * gw + g * SWA_D:kv * gw + (g + 1) * SWA_D, :] for g in range(SWA_G)],
                                  axis=1)
            z64 = jnp.zeros_like(q64)
            qpad = jnp.concatenate([q64, z64] if half == 0 else [z64, q64], axis=0)
            st_sc[kv, 0:nk] = _dot(kall[:, tile * LANE:(tile + 1) * LANE], qpad)
            sm_sc[kv] = _dot(kmeta[:, tile * LANE:(tile + 1) * LANE], qpad)
        for kv in range(SWA_KV):
            for g in range(SWA_G):
                h = kv * SWA_G + g
                cols = slice(g * QBLK, (g + 1) * QBLK)
                s = st_sc[kv, 0:nk, cols] + bias_of(h)
                smg = sm_sc[kv, :, cols]
                sink = sink_ref[h] * log2e
                m = jnp.maximum(jnp.maximum(jnp.max(s, axis=0, keepdims=True), jnp.max(smg, axis=0, keepdims=True)),
                                sink)
                p = jnp.exp2(s - m)
                pm = jnp.exp2(smg - m)
                l_sc[kv, :, cols] = (jnp.sum(p, axis=0, keepdims=True) + jnp.sum(pm, axis=0, keepdims=True)
                                     + jnp.exp2(sink - m))
                p_sc[kv, 0:nk, cols] = p.astype(BF16)
                pm_sc[kv, :, cols] = pm.astype(BF16)
        for kv in range(SWA_KV):
            rows = slice(kv * SWA_D, (kv + 1) * SWA_D)
            ot = (_dot(vall_t[rows, :], p_sc[kv, 0:nk]) + _dot(vmeta_t[rows, :], pm_sc[kv])) / l_sc[kv]
            for g in range(SWA_G):
                o_ref[0, kv * gw + g * SWA_D:kv * gw + (g + 1) * SWA_D, :] = ot[:, g * QBLK:(g + 1) * QBLK].astype(
                    o_ref.dtype)

    @pl.when(i < nb)
    def _():
        kall = jnp.concatenate([kp_ref[0], kc_ref[0], kn_ref[0]], axis=0)
        vall_t = jnp.concatenate([vp_ref[0], vc_ref[0], vn_ref[0]], axis=1)
        row = lax.broadcasted_iota(I32, (3 * QBLK, QBLK), 0)
        pen = jnp.where(((i == 0) & (row < QBLK)) | ((i == nb - 1) & (row >= 2 * QBLK)), NEG, 0.0)
        attend(kall, vall_t, lambda h: tr_ref[h] + pen)

    @pl.when(i == nb)
    def _():
        attend(kn_ref[0], vn_ref[0], lambda h: tm_ref[...])
        lane = lax.broadcasted_iota(I32, o_ref.shape[1:], 1)
        o_ref[0] = jnp.where(lane < N_META, o_ref[0], jnp.zeros(o_ref.shape[1:], o_ref.dtype))

    @pl.when(i > nb)
    def _():
        o_ref[0] = jnp.zeros(o_ref.shape[1:], o_ref.dtype)


def _swa_attention(qvt, k, sinks, n_real):
    B, Lp, kw = k.shape
    nb = n_real // QBLK
    qw = SWA_H * SWA_D
    vrow = qw // kw
    prev = lambda i: jnp.clip(i - 1, 0, nb - 1)
    cur = lambda i: jnp.minimum(i, nb - 1)
    nxt = lambda i: jnp.where(i >= nb, 0, jnp.minimum(i + 1, nb - 1))
    met = lambda i: nb
    kspec = lambda f: pl.BlockSpec((1, QBLK, kw), lambda b, i: (b, f(i), 0))
    vspec = lambda f: pl.BlockSpec((1, kw, QBLK), lambda b, i: (b, vrow, f(i)))
    t_real, t_meta = _swa_bias_tables()
    return pl.pallas_call(
        functools.partial(_swa_body, n_real=n_real),
        out_shape=jax.ShapeDtypeStruct((B, qw, Lp), BF16),
        grid=(B, Lp // QBLK),
        in_specs=[pl.BlockSpec(memory_space=pltpu.SMEM),
                  pl.BlockSpec((1, qw, QBLK), lambda b, i: (b, 0, i)),
                  kspec(prev), kspec(cur), kspec(nxt), kspec(met),
                  vspec(prev), vspec(cur), vspec(nxt), vspec(met),
                  pl.BlockSpec(t_real.shape, lambda b, i: (0, 0, 0)),
                  pl.BlockSpec(t_meta.shape, lambda b, i: (0, 0))],
        out_specs=pl.BlockSpec((1, qw, QBLK), lambda b, i: (b, 0, i)),
        scratch_shapes=[pltpu.VMEM((SWA_KV, 3 * QBLK, SWA_G * QBLK), F32), pltpu.VMEM((SWA_KV, N_META, SWA_G * QBLK), F32),
                        pltpu.VMEM((SWA_KV, 3 * QBLK, SWA_G * QBLK), BF16),
                        pltpu.VMEM((SWA_KV, N_META, SWA_G * QBLK), BF16), pltpu.VMEM((SWA_KV, 1, SWA_G * QBLK), F32)],
        compiler_params=_cp(("parallel", "arbitrary")),
        name="swa_attention",
    )(sinks, qvt, k, k, k, k, qvt, qvt, qvt, qvt, t_real, t_meta)


def _outproj_body(*refs, n_y, y_t):
    h_ref = refs[0]
    y_refs = refs[1:1 + n_y]
    wo_ref, g_ref, wr_ref, h1_ref, xe_ref, aff_ref = refs[1 + n_y:]
    D = h_ref.shape[2]
    acc = h_ref[0]
    k0 = 0
    for y_ref, t in zip(y_refs, y_t):
        if t:
            kk = y_ref.shape[1]
            acc = acc + lax.dot_general(y_ref[0], wo_ref[k0:k0 + kk, :], (((0,), (0,)), ((), ())),
                                        preferred_element_type=F32)
        else:
            kk = y_ref.shape[2]
            acc = acc + _dot(y_ref[0], wo_ref[k0:k0 + kk, :])
        k0 += kk
    h1_ref[0] = acc
    xn = _rms(acc, g_ref[...])
    xh = xn.astype(BF16)
    xl = (xn - xh.astype(F32)).astype(BF16)
    hh = _dot(xh, wr_ref[...])
    logits = hh[:, 0:LANE] + hh[:, LANE:2 * LANE] + _dot(xl, wr_ref[:, 0:LANE])
    lane = lax.broadcasted_iota(I32, logits.shape, 1)
    logits = jnp.where(lane < N_EXP, logits, NEG)
    e = jnp.exp(logits - jnp.max(logits, axis=1, keepdims=True))
    aff = e / jnp.sum(e, axis=1, keepdims=True)
    tm = xn.shape[0]
    xe_ref[0, :, :, 0:D] = xn.reshape(tm, 1, D)
    xe_ref[0, :, :, D:D + LANE] = aff.reshape(tm, 1, LANE)
    aff_ref[0] = aff.T[0:N_EXP, :]


def _outproj(h, ys, wo, g, wr, tm, y_t):
    B, Lp, D = h.shape
    row = lambda b, i: (b, i, 0)
    full = lambda b, i: (0, 0)
    y_spec = lambda y, t: (pl.BlockSpec((1, y.shape[1], tm), lambda b, i: (b, 0, i)) if t
                           else pl.BlockSpec((1, tm, y.shape[2]), row))
    return pl.pallas_call(
        functools.partial(_outproj_body, n_y=len(ys), y_t=y_t),
        out_shape=(jax.ShapeDtypeStruct((B, Lp, D), F32),
                   jax.ShapeDtypeStruct((B, Lp, 1, D + LANE), F32),
                   jax.ShapeDtypeStruct((B, N_EXP, Lp), F32)),
        grid=(B, Lp // tm),
        in_specs=[pl.BlockSpec((1, tm, D), row)] + [y_spec(y, t) for y, t in zip(ys, y_t)]
        + [pl.BlockSpec(wo.shape, full), pl.BlockSpec((1, D), full), pl.BlockSpec(wr.shape, full)],
        out_specs=(pl.BlockSpec((1, tm, D), row), pl.BlockSpec((1, tm, 1, D + LANE), lambda b, i: (b, i, 0, 0)),
                   pl.BlockSpec((1, N_EXP, tm), lambda b, i: (b, 0, i))),
        compiler_params=_cp(("parallel", "parallel")),
        name="outproj_router",
    )(h, *ys, wo, g.reshape(1, D), wr)


def _topk_body(aff_ref, idx_ref, q_ref, lohi_ref, sel_sc, chunk_sc, *, n_real, cap):
    Lp = aff_ref.shape[2]
    nch = Lp // LANE
    cm = n_real // LANE
    aff = aff_ref[0]
    tok = lax.broadcasted_iota(I32, aff.shape, 1)
    keys = jnp.where(tok < n_real + N_META, pltpu.bitcast(aff, I32), -1)

    def search(it, prefix):
        cand = prefix | lax.shift_left(jnp.int32(1), 30 - it)
        cnt = jnp.sum(jnp.where(keys >= cand, 1.0, 0.0), axis=1, keepdims=True)
        return jnp.where(cnt >= cap, cand, prefix)

    thr = lax.fori_loop(0, 31, search, jnp.zeros((N_EXP, 1), I32))
    need = cap - jnp.sum(jnp.where(keys > thr, 1.0, 0.0), axis=1, keepdims=True)

    ri = lax.broadcasted_iota(I32, (LANE, LANE), 0)
    ci = lax.broadcasted_iota(I32, (LANE, LANE), 1)
    upper = (ri <= ci).astype(BF16)
    er = lax.broadcasted_iota(I32, (N_EXP, N_EXP), 0)
    ec = lax.broadcasted_iota(I32, (N_EXP, N_EXP), 1)
    lower = (ec < er).astype(BF16)

    sel_sc[...] = jnp.zeros(sel_sc.shape, F32)

    def select(c, carry):
        c0 = pl.multiple_of(c * LANE, LANE)
        a = pltpu.bitcast(aff_ref[0, :, pl.ds(c0, LANE)], I32)
        t = c0 + lax.broadcasted_iota(I32, a.shape, 1)
        kc = jnp.where(t < n_real + N_META, a, -1)
        eq = (kc == thr).astype(F32)
        rank = _dot(eq.astype(BF16), upper) - eq + carry
        sel = jnp.where((kc > thr) | ((eq > 0) & (rank < need)), 1.0, 0.0)
        sel_sc[:, pl.ds(c0, LANE)] = sel
        return carry + jnp.sum(eq, axis=1, keepdims=True)

    carry = select(jnp.int32(cm), jnp.zeros((N_EXP, 1), F32))
    lax.fori_loop(0, cm, select, carry)

    chunk_sc[...] = jnp.zeros(chunk_sc.shape, F32)

    def offsets(c, carry):
        off_c = carry
        c0 = pl.multiple_of(c * LANE, LANE)
        sel = sel_sc[:, pl.ds(c0, LANE)]
        selb = sel.astype(BF16)
        nt = jnp.sum(sel, axis=0, keepdims=True)
        off_incl = _dot(jnp.broadcast_to(nt, (8, LANE)).astype(BF16), upper)[0:1] + off_c
        r = _dot(lower, selb)
        q_ref[0, :, pl.ds(c0, LANE)] = (off_incl - nt + r).astype(I32)
        lohi_ref[0, 0:1, pl.ds(c0, LANE)] = (off_incl - nt).astype(I32)
        lohi_ref[0, 1:2, pl.ds(c0, LANE)] = off_incl.astype(I32)
        within = _dot(selb, upper)
        for e in range(N_EXP):
            chunk_sc[e, pl.ds(c, 1), :] = within[e:e + 1, :]
        return off_c + jnp.sum(nt, axis=1, keepdims=True)

    lax.fori_loop(0, nch, offsets, jnp.zeros((1, 1), F32))

    ncp = chunk_sc.shape[1]
    cpad = idx_ref.shape[2]
    cr_ = lax.broadcasted_iota(I32, (ncp, ncp), 0)
    cc_ = lax.broadcasted_iota(I32, (ncp, ncp), 1)
    lower_incl = (cc_ <= cr_).astype(BF16)
    slot = lax.broadcasted_iota(I32, (1, cpad), 1).astype(F32)
    chunk_id = lax.broadcasted_iota(I32, (ncp, cpad), 0).astype(F32)
    for e in range(N_EXP):
        within = chunk_sc[e]
        tot = jnp.broadcast_to(within[:, LANE - 1:LANE], (ncp, LANE))
        cend = _dot(lower_incl, tot.astype(BF16))
        cstart = cend - tot
        cend_t = jnp.concatenate([cend] * (cpad // LANE), axis=1)
        cstart_t = jnp.concatenate([cstart] * (cpad // LANE), axis=1)
        cstar = jnp.sum(jnp.where(cend_t <= slot, 1.0, 0.0), axis=0, keepdims=True)
        onehot = chunk_id == cstar
        srel = slot - jnp.sum(jnp.where(onehot, cstart_t, 0.0), axis=0, keepdims=True)
        g = lax.dot_general(within.astype(BF16), jnp.where(onehot, 1.0, 0.0).astype(BF16), (((0,), (0,)), ((), ())),
                            preferred_element_type=F32)
        lane_in = jnp.sum(jnp.where(g <= srel, 1.0, 0.0), axis=0, keepdims=True)
        idx = jnp.where(slot < cap, cstar * LANE + lane_in, 0.0)
        idx_ref[0, e:e + 1, :] = idx.astype(I32)


def _topk(aff_t, n_real, cap, cp):
    B, E, Lp = aff_t.shape
    spec = pl.BlockSpec((1, E, Lp), lambda b: (b, 0, 0))
    ncp = -(-(Lp // LANE) // 8) * 8
    return pl.pallas_call(
        functools.partial(_topk_body, n_real=n_real, cap=cap),
        out_shape=(jax.ShapeDtypeStruct((B, E, cp), I32), jax.ShapeDtypeStruct((B, E, Lp), I32),
                   jax.ShapeDtypeStruct((B, 2, Lp), I32)),
        grid=(B,),
        in_specs=[spec],
        out_specs=(pl.BlockSpec((1, E, cp), lambda b: (b, 0, 0)), spec, pl.BlockSpec((1, 2, Lp), lambda b: (b, 0, 0))),
        scratch_shapes=[pltpu.VMEM((E, Lp), F32), pltpu.VMEM((E, ncp, LANE), F32)],
        compiler_params=_cp(("parallel",)),
        name="expert_topk",
    )(aff_t)


def _ffn_body(idxg_ref, idxw_ref, q_ref, xe_ref, wg_ref, wu_ref, wd_ref, z_ref, xbuf, x2d_sc, xb_sc, yacc, ybuf,
              gsem, ssem, *, cap, zr, n_tiles, n_r, n_f):
    s = pl.program_id(0)
    f = pl.program_id(1)
    nf = pl.num_programs(1)
    tr, D = xb_sc.shape
    B = xe_ref.shape[0]
    total = N_EXP * cap
    per = tr // n_f

    def tile(t):
        t = jnp.clip(t, 0, n_tiles - 1)
        return t // (B * n_r), (t // n_r) % B, (t % n_r) * tr

    @pl.when(f == 0)
    def _():
        @pl.when(s == 0)
        def _():
            xb_sc[...] = jnp.zeros(xb_sc.shape, BF16)
            ybuf[...] = jnp.zeros(ybuf.shape, F32)
            for bb in range(B):
                for r0 in range(total, zr + tr, tr):
                    n = min(tr, zr + tr - r0)
                    cp = pltpu.make_async_copy(ybuf.at[0:n], z_ref.at[bb, r0:r0 + n], ssem)
                    cp.start()
                    cp.wait()

        @pl.when(s >= 1)
        def _():
            pltpu.make_async_copy(xe_ref.at[0, 0:tr], xbuf, gsem).wait()
            x2d_sc[...] = xbuf[...].reshape(x2d_sc.shape)
            xb_sc[...] = x2d_sc[:, 0:D].astype(BF16)

    x = xb_sc[...]
    g = _dot(x, wg_ref[0, 0].astype(BF16))
    u = _dot(x, wu_ref[0, 0].astype(BF16))
    hmid = (g * jax.nn.sigmoid(g) * u).astype(BF16)
    contrib = _dot(hmid, wd_ref[0, 0].astype(BF16))

    _, gb, g0 = tile(s)
    _, wb, w0 = tile(s - 2)
    for i in range(per):
        j = f * per + i
        t = idxg_ref[0, 0, 0, g0 + j]
        pltpu.make_async_copy(xe_ref.at[gb, t], xbuf.at[j], gsem).start()
        live = (s >= 2) & (w0 + j < cap)
        dst = jnp.where(live, q_ref[0, 0, 0, idxw_ref[0, 0, 0, w0 + j]], zr + j)
        pltpu.make_async_copy(ybuf.at[j], z_ref.at[wb, dst], ssem).start()

    @pl.when(f == 0)
    def _():
        yacc[...] = contrib

    @pl.when(f > 0)
    def _():
        yacc[...] += contrib

    @pl.when(f == nf - 1)
    def _():
        pltpu.make_async_copy(ybuf, z_ref.at[0, 0:tr], ssem).wait()

        @pl.when(s >= 1)
        def _():
            e, _, _ = tile(s - 1)
            aff = x2d_sc[:, D:D + LANE]
            lane = lax.broadcasted_iota(I32, aff.shape, 1)
            gate = jnp.sum(jnp.where(lane == e, aff, 0.0), axis=1, keepdims=True)
            ybuf[...] = (yacc[...] * gate).reshape(tr, 1, D)

        @pl.when(s == pl.num_programs(0) - 1)
        def _():
            pltpu.make_async_copy(xe_ref.at[0, 0:tr], xbuf, gsem).wait()


def _ffn(idx, q, xe, wg, wu, wd, layer, cap, cr, zr, tf):
    B, E, _, cp = idx.shape
    Lp = q.shape[2]
    De = xe.shape[3]
    D, F = wg.shape[2], wg.shape[3]
    n_r = 3
    tr = cr // n_r
    n_f = F // tf
    n_tiles = E * B * n_r

    def at_tile(off):
        def index_map(s, f):
            t = jnp.clip(s + off, 0, n_tiles - 1)
            return ((t // n_r) % B, t // (B * n_r), 0, 0)
        return index_map

    expert = lambda s: jnp.clip(s - 1, 0, n_tiles - 1) // (B * n_r)
    smem = lambda n, off: pl.BlockSpec((1, 1, 1, n), at_tile(off), memory_space=pltpu.SMEM)
    return pl.pallas_call(
        functools.partial(_ffn_body, cap=cap, zr=zr, n_tiles=n_tiles, n_r=n_r, n_f=n_f),
        out_shape=jax.ShapeDtypeStruct((B, zr + tr, 1, D), F32),
        grid=(n_tiles + 2, n_f),
        in_specs=[smem(cp, 0), smem(cp, -2), smem(Lp, -2), pl.BlockSpec(memory_space=pl.ANY),
                  pl.BlockSpec((1, 1, D, tf), lambda s, f: (layer, expert(s), 0, f)),
                  pl.BlockSpec((1, 1, D, tf), lambda s, f: (layer, expert(s), 0, f)),
                  pl.BlockSpec((1, 1, tf, D), lambda s, f: (layer, expert(s), f, 0))],
        out_specs=pl.BlockSpec(memory_space=pl.ANY),
        scratch_shapes=[pltpu.VMEM((tr, 1, De), F32), pltpu.VMEM((tr, De), F32), pltpu.VMEM((tr, D), BF16),
                        pltpu.VMEM((tr, D), F32),
                        pltpu.VMEM((tr, 1, D), F32), pltpu.SemaphoreType.DMA(()), pltpu.SemaphoreType.DMA(())],
        compiler_params=_cp(("arbitrary", "arbitrary"), 56),
        name="expert_ffn",
    )(idx, idx, q.reshape(B, E, 1, Lp), xe, wg, wu, wd)


def _combine_body(k0_ref, k1_ref, h_ref, lohi_ref, g_ref, z_ref, *rest, nblk, want_h):
    if want_h:
        h2_ref, u_ref, buf, zb_sc, acc_sc, sem = rest
    else:
        h2_ref, (u_ref, buf, zb_sc, acc_sc, sem) = None, rest
    b = pl.program_id(0)
    n = b * nblk + pl.program_id(1)
    n_total = pl.num_programs(0) * nblk
    k0 = k0_ref[n]
    k1 = k1_ref[n]
    lo = lohi_ref[0, 0:1, :]
    hi = lohi_ref[0, 1:2, :]
    acc_sc[...] = jnp.zeros(acc_sc.shape, F32)

    def copy(bb, k):
        r0 = pl.multiple_of(k * ZCHUNK, ZCHUNK)
        slot = lax.rem(k, 2)
        return pltpu.make_async_copy(z_ref.at[bb, pl.ds(r0, ZCHUNK)], buf.at[slot], sem.at[slot])

    def resident(n_prev, n_next):
        return ((n_prev // nblk == n_next // nblk) & (k1_ref[n_prev] > k0_ref[n_prev])
                & (k1_ref[n_prev] - 1 == k0_ref[n_next]))

    @pl.when(k1 > k0)
    def _():
        @pl.when(n == 0)
        def _():
            copy(b, k0).start()

        @pl.when((n == 0) | jnp.logical_not(resident(jnp.maximum(n - 1, 0), n)))
        def _():
            copy(b, k0).wait()

    def chunk(k, c):
        @pl.when(k + 1 < k1)
        def _():
            copy(b, k + 1).start()

        w = k * ZCHUNK + lax.broadcasted_iota(I32, (ZCHUNK, lo.shape[1]), 0)
        band_t = jnp.where((w >= lo) & (w < hi), 1.0, 0.0)
        zb_sc[...] = buf[lax.rem(k, 2)].reshape(zb_sc.shape)
        acc_sc[...] += _dot(band_t.T.astype(BF16), zb_sc[...].astype(BF16))

        @pl.when(k + 1 < k1)
        def _():
            copy(b, k + 1).wait()
        return c

    lax.fori_loop(k0, k1, chunk, 0)

    n_next = jnp.minimum(n + 1, n_total - 1)

    @pl.when((n + 1 < n_total) & (k1_ref[n_next] > k0_ref[n_next]) & jnp.logical_not(resident(n, n_next)))
    def _():
        copy(n_next // nblk, k0_ref[n_next]).start()

    h2 = h_ref[0] + acc_sc[...]
    if want_h:
        h2_ref[0] = h2
    u_ref[0] = _rms(h2, g_ref[...]).astype(u_ref.dtype)


def _combine(h1, lohi, z, g, n_rows, u_dtype, want_h):
    B, Lp, D = h1.shape
    nblk = n_rows // TOKB
    lo = lohi[:, 0, :n_rows].reshape(B, nblk, TOKB)
    hi = lohi[:, 1, :n_rows].reshape(B, nblk, TOKB)
    k0 = (lo[:, :, 0] // ZCHUNK).reshape(-1).astype(I32)
    k1 = ((hi[:, :, -1] + ZCHUNK - 1) // ZCHUNK).reshape(-1).astype(I32)
    k1 = jnp.where(hi[:, :, -1].reshape(-1) > lo[:, :, 0].reshape(-1), k1, k0)
    row = lambda b, j, *_: (b, j, 0)
    grid_spec = pltpu.PrefetchScalarGridSpec(
        num_scalar_prefetch=2,
        grid=(B, nblk),
        in_specs=[pl.BlockSpec((1, TOKB, D), row),
                  pl.BlockSpec((1, 2, TOKB), lambda b, j, *_: (b, 0, j)),
                  pl.BlockSpec((1, D), lambda b, j, *_: (0, 0)),
                  pl.BlockSpec(memory_space=pl.ANY)],
        out_specs=(pl.BlockSpec((1, TOKB, D), row),) * (2 if want_h else 1),
        scratch_shapes=[pltpu.VMEM((2, ZCHUNK, 1, D), z.dtype), pltpu.VMEM((ZCHUNK, D), F32),
                        pltpu.VMEM((TOKB, D), F32), pltpu.SemaphoreType.DMA((2,))],
    )
    out_shape = (jax.ShapeDtypeStruct((B, n_rows, D), F32),) * want_h + (jax.ShapeDtypeStruct((B, n_rows, D), u_dtype),)
    outs = pl.pallas_call(
        functools.partial(_combine_body, nblk=nblk, want_h=want_h),
        out_shape=out_shape,
        grid_spec=grid_spec,
        compiler_params=_cp(("arbitrary", "arbitrary")),
        name="moe_combine",
    )(k0, k1, h1, lohi, g.reshape(1, D), z)
    return outs if want_h else (None, outs[0])


def _moe(h1, xe, aff_t, wg, wu, wd, layer, g_next, n_real, n_rows_out, u_dtype, want_h):
    B, Lp, D = h1.shape
    cap = (CAP_F * (n_real + N_META)) // N_EXP
    cr = -(-cap // 48) * 48
    cp = -(-(cap + 1) // LANE) * LANE
    zr = -(-(N_EXP * cap) // ZCHUNK) * ZCHUNK
    if zr == N_EXP * cap:
        zr += ZCHUNK
    idx, q, lohi = _topk(aff_t, n_real, cap, cp)
    idx = idx.reshape(B, N_EXP, 1, cp)
    z = _ffn(idx, q, xe, wg, wu, wd, layer, cap, cr, zr, min(wg.shape[3], 256))
    return _combine(h1, lohi, z, g_next, n_rows_out, u_dtype, want_h)


def _rope_tables(n_real, Lp):
    pos = jnp.concatenate([jnp.arange(n_real) + N_META, jnp.arange(N_META), jnp.zeros((Lp - n_real - N_META,), I32)])
    inv_freq = 1.0 / (ROPE_THETA ** (jnp.arange(0, MLA_ROPE, 2, dtype=F32) / MLA_ROPE))
    ang = pos.astype(F32)[:, None] * inv_freq[None, :]
    cos, sin = jnp.cos(ang), jnp.sin(ang)
    half = MLA_ROPE // 2
    z = lambda n: jnp.zeros((Lp, n), F32)
    c = jnp.concatenate([cos, cos, z(LANE - 2 * half)], axis=1)
    sn = jnp.concatenate([-sin, z(LANE - half)], axis=1)
    sp = jnp.concatenate([z(half), sin, z(LANE - 2 * half)], axis=1)
    return (c, sn, sp), (cos.T, sin.T)


def kernel(x, meta_tokens, norm_mix, norm_ffn, norm_final, ab_w_in, ab_q_norm, ab_kv_norm, ab_w_uq, ab_w_ukv, ab_rpb, ab_w_out, c_w_in, c_sinks, c_w_out, ec_w_router, ec_w_gate, ec_w_up, ec_w_down):
    B, S, D = x.shape
    depth = norm_mix.shape[0]
    Lp = S + TAIL
    tm = 512
    h, u = _embed(x, meta_tokens.astype(x.dtype), norm_mix[0], tm)
    tabs, tabs_t = _rope_tables(S, Lp)

    for layer in range(depth):
        if layer % 2 == 0:
            e = layer // 2
            w_in = ab_w_in[e].astype(BF16)
            o2 = MLA_QL + MLA_KVL
            o3 = o2 + MLA_ROPE
            lat = _mm(u, w_in[:, :o2], tm, o2)
            krope = _mm(u, jnp.pad(w_in[:, o2:o3], ((0, 0), (0, LANE - MLA_ROPE))), tm, LANE)
            nw = NA_H * NA_D
            na_wq = ab_w_in[e][:, o3:o3 + nw] * (NA_D ** -0.5 * math.log2(math.e))
            na_wqv_t = jnp.concatenate([na_wq, ab_w_in[e][:, o3 + 2 * nw:]], axis=1).T.astype(BF16)
            na_qvt = _mm_t(u, na_wqv_t, tm, nw)
            na_k = _mm(u, w_in[:, o3 + nw:o3 + 2 * nw], tm, nw)
            wq = jnp.pad(ab_w_uq[e].reshape(MLA_QL, MLA_H, MLA_NOPE + MLA_ROPE),
                         ((0, 0), (0, 0), (0, MLA_HP - MLA_NOPE - MLA_ROPE))).reshape(MLA_QL, MLA_H * MLA_HP)
            wkv = ab_w_ukv[e].reshape(MLA_KVL, MLA_H, MLA_NOPE + MLA_V)
            wk = wkv[:, :, :MLA_NOPE].reshape(MLA_KVL, -1)
            wv = wkv[:, :, MLA_NOPE:].reshape(MLA_KVL, -1)
            qt, k, vt = _mla_up(lat, krope, ab_q_norm[e], ab_kv_norm[e], wq.T.astype(BF16), wk.astype(BF16),
                                wv.T.astype(BF16), tabs, tabs_t, tm)
            y_mla = _mla_attention(qt, k, vt, S, TAIL, min(S, 512))
            y_na = _na_attention(na_qvt, na_k, _na_bias_table(ab_rpb[e]), S)
            ys, y_t = [y_mla, y_na], (False, True)
            w_out = ab_w_out[e]
        else:
            o = layer // 2
            qw, kw = SWA_H * SWA_D, SWA_KV * SWA_D
            w_in = c_w_in[o]
            wq = w_in[:, :qw] * (SWA_D ** -0.5 * math.log2(math.e))
            wqv_t = jnp.concatenate([wq, w_in[:, qw + kw:]], axis=1).T.astype(BF16)
            qvt = _mm_t(u, wqv_t, tm, (qw + kw) // 3)
            k = _mm(u, w_in[:, qw:qw + kw].astype(BF16), tm, kw)
            ys, y_t = [_swa_attention(qvt, k, c_sinks[o], S)], (True,)
            w_out = c_w_out[o]
        wr = jnp.pad(ec_w_router[layer], ((0, 0), (0, LANE - N_EXP)))
        wr_hi = wr.astype(BF16)
        wr = jnp.concatenate([wr_hi, (wr - wr_hi.astype(F32)).astype(BF16)], axis=1)
        h1, xe, aff_t = _outproj(h, ys, w_out.astype(BF16), norm_ffn[layer], wr, 256, y_t)
        last = layer == depth - 1
        g_next = norm_final if last else norm_mix[layer + 1]
        h, u = _moe(h1, xe, aff_t, ec_w_gate, ec_w_up, ec_w_down, layer, g_next, S,
                    S if last else Lp, F32 if last else BF16, want_h=not last)
    return u
```

```python
import functools
import math

import jax
import jax.numpy as jnp
from jax import lax
from jax.experimental import pallas as pl
from jax.experimental.pallas import tpu as pltpu

F32 = jnp.float32
BF16 = jnp.bfloat16
I32 = jnp.int32

N_META = 16
GRID_W = 64
QBLK = 128
EPS = 1e-6
NEG = -1e30

MLA_H = 8
MLA_NOPE = 128
MLA_ROPE = 64
MLA_V = 128
MLA_QL = 512
MLA_KVL = 512
MLA_HP = 256
ROPE_THETA = 10000.0

NA_H = 8
NA_D = 128
NA_KH = 8
NA_KW = 16

SWA_H = 32
SWA_KV = 4
SWA_G = SWA_H // SWA_KV
SWA_D = 64
SWA_WIN = 128

N_EXP = 16
CAP_F = 2

TAIL = 512
LANE = 128
ZCHUNK = 256
TOKB = 512


def _cp(sem, vmem_mb=48):
    return pltpu.CompilerParams(dimension_semantics=sem, vmem_limit_bytes=vmem_mb << 20)


def _dot(a, b):
    return jnp.dot(a, b, preferred_element_type=F32)


def _dot_nt(a, b):
    return lax.dot_general(a, b, (((1,), (1,)), ((), ())), preferred_element_type=F32)


def _rms(x, g):
    return x * lax.rsqrt(jnp.mean(x * x, axis=-1, keepdims=True) + EPS) * g


def _embed_body(x_ref, meta_ref, g_ref, h_ref, u_ref, *, n_blocks):
    i = pl.program_id(1)

    @pl.when(i < n_blocks)
    def _():
        h_ref[0] = x_ref[0]

    @pl.when(i >= n_blocks)
    def _():
        h_ref[0] = jnp.zeros(h_ref.shape[1:], h_ref.dtype)
        h_ref[0, 0:N_META, :] = meta_ref[...]

    u_ref[0] = _rms(h_ref[0], g_ref[...]).astype(u_ref.dtype)


def _embed(x, meta, g, tm):
    B, S, D = x.shape
    Lp = S + TAIL
    nb = S // tm
    row = lambda b, i: (b, i, 0)
    return pl.pallas_call(
        functools.partial(_embed_body, n_blocks=nb),
        out_shape=(jax.ShapeDtypeStruct((B, Lp, D), x.dtype), jax.ShapeDtypeStruct((B, Lp, D), BF16)),
        grid=(B, Lp // tm),
        in_specs=[pl.BlockSpec((1, tm, D), lambda b, i: (b, jnp.minimum(i, nb - 1), 0)),
                  pl.BlockSpec(meta.shape, lambda b, i: (0, 0)), pl.BlockSpec((1, D), lambda b, i: (0, 0))],
        out_specs=(pl.BlockSpec((1, tm, D), row), pl.BlockSpec((1, tm, D), row)),
        compiler_params=_cp(("parallel", "parallel")),
        name="embed_rmsnorm",
    )(x, meta, g.reshape(1, D))


def _mm_body(x_ref, w_ref, o_ref):
    o_ref[0] = _dot(x_ref[0], w_ref[...]).astype(o_ref.dtype)


def _mm(x, w, tm, tn):
    B, Lp, K = x.shape
    N = w.shape[1]
    return pl.pallas_call(
        _mm_body,
        out_shape=jax.ShapeDtypeStruct((B, Lp, N), BF16),
        grid=(B, Lp // tm, N // tn),
        in_specs=[pl.BlockSpec((1, tm, K), lambda b, i, j: (b, i, 0)), pl.BlockSpec((K, tn), lambda b, i, j: (0, j))],
        out_specs=pl.BlockSpec((1, tm, tn), lambda b, i, j: (b, i, j)),
        compiler_params=_cp(("parallel", "parallel", "parallel")),
        name="in_proj",
    )(x, w)


def _mm_t_body(x_ref, wt_ref, o_ref):
    o_ref[0] = _dot_nt(wt_ref[...], x_ref[0]).astype(o_ref.dtype)


def _mm_t(x, wt, tm, tn):
    B, Lp, K = x.shape
    N = wt.shape[0]
    return pl.pallas_call(
        _mm_t_body,
        out_shape=jax.ShapeDtypeStruct((B, N, Lp), BF16),
        grid=(B, Lp // tm, N // tn),
        in_specs=[pl.BlockSpec((1, tm, K), lambda b, i, j: (b, i, 0)), pl.BlockSpec((tn, K), lambda b, i, j: (j, 0))],
        out_specs=pl.BlockSpec((1, tn, tm), lambda b, i, j: (b, j, i)),
        compiler_params=_cp(("parallel", "parallel", "parallel")),
        name="in_proj_t",
    )(x, wt)


def _rope(pe, c, sn, sp):
    return pe * c + pltpu.roll(pe, 96, 1) * sn + pltpu.roll(pe, 32, 1) * sp


def _uq_body(lat_ref, g_ref, wt_ref, c_ref, s_ref, o_ref, *, scale):
    xn = _rms(lat_ref[0].astype(F32), g_ref[...]).astype(BF16)
    zt = _dot_nt(wt_ref[...], xn)
    c, s = c_ref[...], s_ref[...]
    half = MLA_ROPE // 2
    dt = o_ref.dtype
    for h in range(MLA_H):
        o = h * MLA_HP
        p1, p2, p3 = o + MLA_NOPE, o + MLA_NOPE + half, o + MLA_NOPE + MLA_ROPE
        x1, x2 = zt[p1:p2], zt[p2:p3]
        o_ref[0, o:p1, :] = (zt[o:p1] * scale).astype(dt)
        o_ref[0, p1:p2, :] = ((x1 * c - x2 * s) * scale).astype(dt)
        o_ref[0, p2:p3, :] = ((x2 * c + x1 * s) * scale).astype(dt)
        o_ref[0, p3:o + MLA_HP, :] = jnp.zeros((o + MLA_HP - p3, zt.shape[1]), dt)


def _ukv_body(lat_ref, g_ref, wk_ref, wvt_ref, kr_ref, c_ref, sn_ref, sp_ref, k_ref, vt_ref):
    xn = _rms(lat_ref[0].astype(F32), g_ref[...]).astype(BF16)
    zk = _dot(xn, wk_ref[...])
    kr = _rope(kr_ref[0].astype(F32), c_ref[...], sn_ref[...], sp_ref[...]).astype(k_ref.dtype)
    for h in range(MLA_H):
        k_ref[0, :, h * MLA_HP:h * MLA_HP + LANE] = zk[:, h * LANE:(h + 1) * LANE].astype(k_ref.dtype)
        k_ref[0, :, h * MLA_HP + LANE:(h + 1) * MLA_HP] = kr
    vt_ref[0] = _dot_nt(wvt_ref[...], xn).astype(vt_ref.dtype)


def _mla_up(lat, krope, qn, kvn, wqt, wk, wvt, tabs, tabs_t, tm):
    B, Lp, _ = lat.shape
    c, sn, sp = tabs
    ct, st = tabs_t
    tab_spec = pl.BlockSpec((tm, LANE), lambda b, i: (i, 0))
    tabt_spec = pl.BlockSpec((MLA_ROPE // 2, tm), lambda b, i: (0, i))
    scale = (MLA_NOPE + MLA_ROPE) ** -0.5 * math.log2(math.e)
    qt = pl.pallas_call(
        functools.partial(_uq_body, scale=scale),
        out_shape=jax.ShapeDtypeStruct((B, MLA_H * MLA_HP, Lp), BF16),
        grid=(B, Lp // tm),
        in_specs=[pl.BlockSpec((1, tm, MLA_QL), lambda b, i: (b, i, 0)),
                  pl.BlockSpec((1, MLA_QL), lambda b, i: (0, 0)),
                  pl.BlockSpec(wqt.shape, lambda b, i: (0, 0)),
                  tabt_spec, tabt_spec],
        out_specs=pl.BlockSpec((1, MLA_H * MLA_HP, tm), lambda b, i: (b, 0, i)),
        compiler_params=_cp(("parallel", "parallel")),
        name="mla_q_up",
    )(lat, qn.reshape(1, -1), wqt, ct, st)
    k, vt = pl.pallas_call(
        _ukv_body,
        out_shape=(jax.ShapeDtypeStruct((B, Lp, MLA_H * MLA_HP), BF16),
                   jax.ShapeDtypeStruct((B, MLA_H * MLA_V, Lp), BF16)),
        grid=(B, Lp // tm),
        in_specs=[pl.BlockSpec((1, tm, MLA_KVL), lambda b, i: (b, i, 1)),
                  pl.BlockSpec((1, MLA_KVL), lambda b, i: (0, 0)),
                  pl.BlockSpec(wk.shape, lambda b, i: (0, 0)),
                  pl.BlockSpec(wvt.shape, lambda b, i: (0, 0)),
                  pl.BlockSpec((1, tm, LANE), lambda b, i: (b, i, 0)),
                  tab_spec, tab_spec, tab_spec],
        out_specs=(pl.BlockSpec((1, tm, MLA_H * MLA_HP), lambda b, i: (b, i, 0)),
                   pl.BlockSpec((1, MLA_H * MLA_V, tm), lambda b, i: (b, 0, i))),
        compiler_params=_cp(("parallel", "parallel")),
        name="mla_kv_up",
    )(lat, kvn.reshape(1, -1), wk, wvt, krope, c, sn, sp)
    return qt, k, vt


def _mla_body(qt_ref, k_ref, vt_ref, o_ref, m_sc, acc_sc, s_sc, *, n_real, tkc, ahead):
    qt = qt_ref[0]
    tq = qt.shape[1]
    nch = n_real // tkc
    ring = s_sc.shape[0]
    m_sc[...] = jnp.full(m_sc.shape, NEG, F32)
    acc_sc[...] = jnp.zeros(acc_sc.shape, F32)

    def scores(c):
        return _dot(k_ref[0, pl.ds(pl.multiple_of(c * tkc, tkc), tkc), :], qt)

    def consume(st, vtc):
        m_prev = m_sc[...]
        m_new = jnp.maximum(m_prev, jnp.max(st, axis=0, keepdims=True))
        p = jnp.exp2((st - m_new).astype(BF16))
        alpha = jnp.exp2(m_prev - m_new)
        vte = jnp.concatenate([vtc, jnp.ones((16, vtc.shape[1]), BF16)], axis=0)
        acc_sc[...] = alpha * acc_sc[...] + _dot(vte, p)
        m_sc[...] = m_new

    def values(c):
        return vt_ref[0, :, pl.ds(pl.multiple_of(c * tkc, tkc), tkc)]

    for u in range(ahead):
        s_sc[u] = scores(u)

    def trip(c0, last):
        for u in range(ring):
            if not (last and u + ahead >= ring):
                s_sc[(u + ahead) % ring] = scores(c0 + u + ahead)
            consume(s_sc[u], values(c0 + u))

    def body(cb, carry):
        trip(cb * ring, False)
        return carry

    lax.fori_loop(0, nch // ring - 1, body, 0)
    trip(jnp.int32(nch - ring), True)
    consume(_dot(k_ref[0, n_real:n_real + N_META, :], qt), vt_ref[0, :, n_real:n_real + N_META])
    out = (acc_sc[0:MLA_V, :] / acc_sc[MLA_V:MLA_V + 1, :]).T
    i = pl.program_id(2)
    row = i * tq + lax.broadcasted_iota(I32, (tq, 1), 0)
    o_ref[0] = jnp.where(row < n_real + N_META, out, 0.0).astype(o_ref.dtype)


def _mla_attention(qt, k, vt, n_real, tq, tkc):
    B, Lp, _ = k.shape
    nch = n_real // tkc
    ring, ahead = (8, 3) if nch % 8 == 0 else ((4, 2) if nch % 4 == 0 else (2, 1))
    return pl.pallas_call(
        functools.partial(_mla_body, n_real=n_real, tkc=tkc, ahead=ahead),
        out_shape=jax.ShapeDtypeStruct((B, Lp, MLA_H * MLA_V), BF16),
        grid=(B, MLA_H, Lp // tq),
        in_specs=[pl.BlockSpec((1, MLA_HP, tq), lambda b, h, i: (b, h, i)),
                  pl.BlockSpec((1, Lp, MLA_HP), lambda b, h, i: (b, 0, h)),
                  pl.BlockSpec((1, MLA_V, Lp), lambda b, h, i: (b, h, 0))],
        out_specs=pl.BlockSpec((1, tq, MLA_V), lambda b, h, i: (b, i, h)),
        scratch_shapes=[pltpu.VMEM((1, tq), F32), pltpu.VMEM((MLA_V + 16, tq), F32),
                        pltpu.VMEM((ring, tkc, tq), F32)],
        compiler_params=_cp(("parallel", "parallel", "arbitrary")),
        name="mla_attention",
    )(qt, k, vt)


NA_GROUP = 4
NA_UNION = NA_KH + NA_GROUP


def _na_bias_table(rpb):
    col = jnp.arange(GRID_W)
    cs = jnp.clip(col - NA_KW // 2, 0, GRID_W - NA_KW)
    ok_c = (col[None, :] >= cs[:, None]) & (col[None, :] < cs[:, None] + NA_KW)
    dc = jnp.clip(col[None, :] - col[:, None], -(NA_KW - 1), NA_KW - 1) + (NA_KW - 1)
    i = jnp.arange(NA_GROUP)
    j = jnp.arange(NA_UNION)
    start = jnp.stack([0 * i, i, 0 * i + NA_UNION - NA_KH])
    d0 = jnp.stack([NA_KH - 1 - i, 0 * i + NA_KH // 2 - 1, NA_GROUP - 1 - i])
    rel = j[None, None, :] - start[:, :, None]
    ok_r = (rel >= 0) & (rel < NA_KH)
    dr = jnp.clip(d0[:, :, None] + rel, 0, 2 * NA_KH - 2)
    t = rpb[:, dr][:, :, :, :, dc]
    t = jnp.where(ok_r[None, :, :, :, None, None] & ok_c[None, None, None, None], t * math.log2(math.e), NEG)
    t = t.transpose(0, 1, 3, 5, 2, 4)
    return t.reshape(rpb.shape[0], 3, NA_UNION * GRID_W, NA_GROUP * GRID_W).astype(F32)


def _na_body(qt_ref, k_ref, vt_ref, t_ref, o_ref, s_sc, sm_sc, p_sc, pm_sc, l_sc, *, n_real, gps):
    i = pl.program_id(2)
    rows = n_real // GRID_W
    n_groups = rows // NA_GROUP
    gw = NA_GROUP * GRID_W
    km = k_ref[0, n_real:n_real + N_META, :]
    vmt = vt_ref[0, :, n_real:n_real + N_META]

    @pl.when(i < n_groups // gps)
    def _():
        def window(gg):
            g = i * gps + gg
            variant = jnp.where(g == 0, 0, jnp.where(g == n_groups - 1, 2, 1))
            u0 = jnp.where(g == 0, 0, jnp.where(g == n_groups - 1, rows - NA_UNION, g * NA_GROUP - NA_KH // 2))
            return variant, pl.multiple_of(u0 * GRID_W, NA_GROUP * GRID_W)

        for gg in range(gps):
            variant, k0 = window(gg)
            qt = qt_ref[0, :, gg * gw:(gg + 1) * gw]
            s_sc[gg] = _dot(k_ref[0, pl.ds(k0, NA_UNION * GRID_W), :], qt) + t_ref[0, variant]
            sm_sc[gg] = _dot(km, qt)
        for gg in range(gps):
            s, sm = s_sc[gg], sm_sc[gg]
            m = jnp.maximum(jnp.max(s, axis=0, keepdims=True), jnp.max(sm, axis=0, keepdims=True))
            p = jnp.exp2(s - m)
            pm = jnp.exp2(sm - m)
            l_sc[gg] = jnp.sum(p, axis=0, keepdims=True) + jnp.sum(pm, axis=0, keepdims=True)
            p_sc[gg] = p.astype(BF16)
            pm_sc[gg] = pm.astype(BF16)
        for gg in range(gps):
            _, k0 = window(gg)
            ot = (_dot(vt_ref[0, :, pl.ds(k0, NA_UNION * GRID_W)], p_sc[gg]) + _dot(vmt, pm_sc[gg])) / l_sc[gg]
            o_ref[0, :, gg * gw:(gg + 1) * gw] = ot.astype(o_ref.dtype)

    @pl.when(i >= n_groups // gps)
    def _():
        sm = _dot(km, qt_ref[0, :, 0:gw])
        pm = jnp.exp2(sm - jnp.max(sm, axis=0, keepdims=True))
        ot = _dot(vmt, pm.astype(BF16)) / jnp.sum(pm, axis=0, keepdims=True)
        lane = lax.broadcasted_iota(I32, ot.shape, 1)
        o_ref[0] = jnp.zeros(o_ref.shape[1:], o_ref.dtype)
        o_ref[0, :, 0:gw] = jnp.where(lane < N_META, ot, 0.0).astype(o_ref.dtype)


def _na_attention(qvt, k, table, n_real):
    B, Lp, _ = k.shape
    gps = TAIL // (NA_GROUP * GRID_W)
    gw, nk = NA_GROUP * GRID_W, NA_UNION * GRID_W
    tq = gps * gw
    assert n_real // GRID_W >= NA_UNION and (n_real // GRID_W) % (NA_GROUP * gps) == 0
    return pl.pallas_call(
        functools.partial(_na_body, n_real=n_real, gps=gps),
        out_shape=jax.ShapeDtypeStruct((B, NA_H * NA_D, Lp), BF16),
        grid=(B, NA_H, Lp // tq),
        in_specs=[pl.BlockSpec((1, NA_D, tq), lambda b, h, i: (b, h, i)),
                  pl.BlockSpec((1, Lp, NA_D), lambda b, h, i: (b, 0, h)),
                  pl.BlockSpec((1, NA_D, Lp), lambda b, h, i: (b, NA_H + h, 0)),
                  pl.BlockSpec((1,) + table.shape[1:], lambda b, h, i: (h, 0, 0, 0))],
        out_specs=pl.BlockSpec((1, NA_D, tq), lambda b, h, i: (b, h, i)),
        scratch_shapes=[pltpu.VMEM((gps, nk, gw), F32), pltpu.VMEM((gps, N_META, gw), F32),
                        pltpu.VMEM((gps, nk, gw), BF16), pltpu.VMEM((gps, N_META, gw), BF16),
                        pltpu.VMEM((gps, 1, gw), F32)],
        compiler_params=_cp(("parallel", "parallel", "arbitrary")),
        name="na_attention",
    )(qvt, k, qvt, table)


def _swa_bias_tables():
    a = jnp.arange(QBLK)[None, :]
    j = jnp.arange(3 * QBLK)[:, None]
    dist = jnp.abs(j - QBLK - a)
    slopes = 2.0 ** (-8.0 * jnp.arange(1, SWA_H + 1, dtype=F32) / SWA_H)
    real = jnp.where(dist <= SWA_WIN, -slopes[:, None, None] * dist.astype(F32) * math.log2(math.e), NEG)
    meta = jnp.where(N_META + jnp.arange(QBLK)[:, None] - a <= SWA_WIN, 0.0, NEG)
    return real.astype(F32), meta.astype(F32)


def _swa_body(sink_ref, qt_ref, kp_ref, kc_ref, kn_ref, km_ref, vp_ref, vc_ref, vn_ref, vm_ref, tr_ref, tm_ref, o_ref,
              st_sc, sm_sc, p_sc, pm_sc, l_sc, *, n_real):
    i = pl.program_id(1)
    nb = n_real // QBLK
    log2e = math.log2(math.e)
    gw = SWA_G * SWA_D
    kmeta = km_ref[0, 0:N_META, :]
    vmeta_t = vm_ref[0, :, 0:N_META]

    def attend(kall, vall_t, bias_of):
        nk = kall.shape[0]
        for kv in range(SWA_KV):
            tile, half = kv // 2, kv % 2
            q64 = jnp.concatenate([qt_ref[0, kv * gw + g * SWA_D:kv * gw + (g + 1) * SWA_D, :] for g in range(SWA_G)],
                                  axis=1)
            z64 = jnp.zeros_like(q64)
            qpad = jnp.concatenate([q64, z64] if half == 0 else [z64, q64], axis=0)
            st_sc[kv, 0:nk] = _dot(kall[:, tile * LANE:(tile + 1) * LANE], qpad)
            sm_sc[kv] = _dot(kmeta[:, tile * LANE:(tile + 1) * LANE], qpad)
        for kv in range(SWA_KV):
            for g in range(SWA_G):
                h = kv * SWA_G + g
                cols = slice(g * QBLK, (g + 1) * QBLK)
                s = st_sc[kv, 0:nk, cols] + bias_of(h)
                smg = sm_sc[kv, :, cols]
                sink = sink_ref[h] * log2e
                m = jnp.maximum(jnp.maximum(jnp.max(s, axis=0, keepdims=True), jnp.max(smg, axis=0, keepdims=True)),
                                sink)
                p = jnp.exp2(s - m)
                pm = jnp.exp2(smg - m)
                l_sc[kv, :, cols] = (jnp.sum(p, axis=0, keepdims=True) + jnp.sum(pm, axis=0, keepdims=True)
                                     + jnp.exp2(sink - m))
                p_sc[kv, 0:nk, cols] = p.astype(BF16)
                pm_sc[kv, :, cols] = pm.astype(BF16)
        for kv in range(SWA_KV):
            rows = slice(kv * SWA_D, (kv + 1) * SWA_D)
            ot = (_dot(vall_t[rows, :], p_sc[kv, 0:nk]) + _dot(vmeta_t[rows, :], pm_sc[kv])) / l_sc[kv]
            for g in range(SWA_G):
                o_ref[0, kv * gw + g * SWA_D:kv * gw + (g + 1) * SWA_D, :] = ot[:, g * QBLK:(g + 1) * QBLK].astype(
                    o_ref.dtype)

    @pl.when(i < nb)
    def _():
        kall = jnp.concatenate([kp_ref[0], kc_ref[0], kn_ref[0]], axis=0)
        vall_t = jnp.concatenate([vp_ref[0], vc_ref[0], vn_ref[0]], axis=1)
        row = lax.broadcasted_iota(I32, (3 * QBLK, QBLK), 0)
        pen = jnp.where(((i == 0) & (row < QBLK)) | ((i == nb - 1) & (row >= 2 * QBLK)), NEG, 0.0)
        attend(kall, vall_t, lambda h: tr_ref[h] + pen)

    @pl.when(i == nb)
    def _():
        attend(kn_ref[0], vn_ref[0], lambda h: tm_ref[...])
        lane = lax.broadcasted_iota(I32, o_ref.shape[1:], 1)
        o_ref[0] = jnp.where(lane < N_META, o_ref[0], jnp.zeros(o_ref.shape[1:], o_ref.dtype))

    @pl.when(i > nb)
    def _():
        o_ref[0] = jnp.zeros(o_ref.shape[1:], o_ref.dtype)


def _swa_attention(qvt, k, sinks, n_real):
    B, Lp, kw = k.shape
    nb = n_real // QBLK
    qw = SWA_H * SWA_D
    vrow = qw // kw
    prev = lambda i: jnp.clip(i - 1, 0, nb - 1)
    cur = lambda i: jnp.minimum(i, nb - 1)
    nxt = lambda i: jnp.where(i >= nb, 0, jnp.minimum(i + 1, nb - 1))
    met = lambda i: nb
    kspec = lambda f: pl.BlockSpec((1, QBLK, kw), lambda b, i: (b, f(i), 0))
    vspec = lambda f: pl.BlockSpec((1, kw, QBLK), lambda b, i: (b, vrow, f(i)))
    t_real, t_meta = _swa_bias_tables()
    return pl.pallas_call(
        functools.partial(_swa_body, n_real=n_real),
        out_shape=jax.ShapeDtypeStruct((B, qw, Lp), BF16),
        grid=(B, Lp // QBLK),
        in_specs=[pl.BlockSpec(memory_space=pltpu.SMEM),
                  pl.BlockSpec((1, qw, QBLK), lambda b, i: (b, 0, i)),
                  kspec(prev), kspec(cur), kspec(nxt), kspec(met),
                  vspec(prev), vspec(cur), vspec(nxt), vspec(met),
                  pl.BlockSpec(t_real.shape, lambda b, i: (0, 0, 0)),
                  pl.BlockSpec(t_meta.shape, lambda b, i: (0, 0))],
        out_specs=pl.BlockSpec((1, qw, QBLK), lambda b, i: (b, 0, i)),
        scratch_shapes=[pltpu.VMEM((SWA_KV, 3 * QBLK, SWA_G * QBLK), F32), pltpu.VMEM((SWA_KV, N_META, SWA_G * QBLK), F32),
                        pltpu.VMEM((SWA_KV, 3 * QBLK, SWA_G * QBLK), BF16),
                        pltpu.VMEM((SWA_KV, N_META, SWA_G * QBLK), BF16), pltpu.VMEM((SWA_KV, 1, SWA_G * QBLK), F32)],
        compiler_params=_cp(("parallel", "arbitrary")),
        name="swa_attention",
    )(sinks, qvt, k, k, k, k, qvt, qvt, qvt, qvt, t_real, t_meta)


def _outproj_body(*refs, n_y, y_t):
    h_ref = refs[0]
    y_refs = refs[1:1 + n_y]
    wo_ref, g_ref, wr_ref, h1_ref, xe_ref, aff_ref = refs[1 + n_y:]
    D = h_ref.shape[2]
    acc = h_ref[0]
    k0 = 0
    for y_ref, t in zip(y_refs, y_t):
        if t:
            kk = y_ref.shape[1]
            acc = acc + lax.dot_general(y_ref[0], wo_ref[k0:k0 + kk, :], (((0,), (0,)), ((), ())),
                                        preferred_element_type=F32)
        else:
            kk = y_ref.shape[2]
            acc = acc + _dot(y_ref[0], wo_ref[k0:k0 + kk, :])
        k0 += kk
    h1_ref[0] = acc
    xn = _rms(acc, g_ref[...])
    xh = xn.astype(BF16)
    xl = (xn - xh.astype(F32)).astype(BF16)
    hh = _dot(xh, wr_ref[...])
    logits = hh[:, 0:LANE] + hh[:, LANE:2 * LANE] + _dot(xl, wr_ref[:, 0:LANE])
    lane = lax.broadcasted_iota(I32, logits.shape, 1)
    logits = jnp.where(lane < N_EXP, logits, NEG)
    e = jnp.exp(logits - jnp.max(logits, axis=1, keepdims=True))
    aff = e / jnp.sum(e, axis=1, keepdims=True)
    tm = xn.shape[0]
    xe_ref[0, :, :, 0:D] = xn.reshape(tm, 1, D)
    xe_ref[0, :, :, D:D + LANE] = aff.reshape(tm, 1, LANE)
    aff_ref[0] = aff.T[0:N_EXP, :]


def _outproj(h, ys, wo, g, wr, tm, y_t):
    B, Lp, D = h.shape
    row = lambda b, i: (b, i, 0)
    full = lambda b, i: (0, 0)
    y_spec = lambda y, t: (pl.BlockSpec((1, y.shape[1], tm), lambda b, i: (b, 0, i)) if t
                           else pl.BlockSpec((1, tm, y.shape[2]), row))
    return pl.pallas_call(
        functools.partial(_outproj_body, n_y=len(ys), y_t=y_t),
        out_shape=(jax.ShapeDtypeStruct((B, Lp, D), F32),
                   jax.ShapeDtypeStruct((B, Lp, 1, D + LANE), F32),
                   jax.ShapeDtypeStruct((B, N_EXP, Lp), F32)),
        grid=(B, Lp // tm),
        in_specs=[pl.BlockSpec((1, tm, D), row)] + [y_spec(y, t) for y, t in zip(ys, y_t)]
        + [pl.BlockSpec(wo.shape, full), pl.BlockSpec((1, D), full), pl.BlockSpec(wr.shape, full)],
        out_specs=(pl.BlockSpec((1, tm, D), row), pl.BlockSpec((1, tm, 1, D + LANE), lambda b, i: (b, i, 0, 0)),
                   pl.BlockSpec((1, N_EXP, tm), lambda b, i: (b, 0, i))),
        compiler_params=_cp(("parallel", "parallel")),
        name="outproj_router",
    )(h, *ys, wo, g.reshape(1, D), wr)


def _topk_body(aff_ref, idx_ref, q_ref, lohi_ref, sel_sc, chunk_sc, *, n_real, cap):
    Lp = aff_ref.shape[2]
    nch = Lp // LANE
    cm = n_real // LANE
    aff = aff_ref[0]
    tok = lax.broadcasted_iota(I32, aff.shape, 1)
    keys = jnp.where(tok < n_real + N_META, pltpu.bitcast(aff, I32), -1)

    def search(it, prefix):
        cand = prefix | lax.shift_left(jnp.int32(1), 30 - it)
        cnt = jnp.sum(jnp.where(keys >= cand, 1.0, 0.0), axis=1, keepdims=True)
        return jnp.where(cnt >= cap, cand, prefix)

    thr = lax.fori_loop(0, 31, search, jnp.zeros((N_EXP, 1), I32))
    need = cap - jnp.sum(jnp.where(keys > thr, 1.0, 0.0), axis=1, keepdims=True)

    ri = lax.broadcasted_iota(I32, (LANE, LANE), 0)
    ci = lax.broadcasted_iota(I32, (LANE, LANE), 1)
    upper = (ri <= ci).astype(BF16)
    er = lax.broadcasted_iota(I32, (N_EXP, N_EXP), 0)
    ec = lax.broadcasted_iota(I32, (N_EXP, N_EXP), 1)
    lower = (ec < er).astype(BF16)

    sel_sc[...] = jnp.zeros(sel_sc.shape, F32)

    def select(c, carry):
        c0 = pl.multiple_of(c * LANE, LANE)
        a = pltpu.bitcast(aff_ref[0, :, pl.ds(c0, LANE)], I32)
        t = c0 + lax.broadcasted_iota(I32, a.shape, 1)
        kc = jnp.where(t < n_real + N_META, a, -1)
        eq = (kc == thr).astype(F32)
        rank = _dot(eq.astype(BF16), upper) - eq + carry
        sel = jnp.where((kc > thr) | ((eq > 0) & (rank < need)), 1.0, 0.0)
        sel_sc[:, pl.ds(c0, LANE)] = sel
        return carry + jnp.sum(eq, axis=1, keepdims=True)

    carry = select(jnp.int32(cm), jnp.zeros((N_EXP, 1), F32))
    lax.fori_loop(0, cm, select, carry)

    chunk_sc[...] = jnp.zeros(chunk_sc.shape, F32)

    def offsets(c, carry):
        off_c = carry
        c0 = pl.multiple_of(c * LANE, LANE)
        sel = sel_sc[:, pl.ds(c0, LANE)]
        selb = sel.astype(BF16)
        nt = jnp.sum(sel, axis=0, keepdims=True)
        off_incl = _dot(jnp.broadcast_to(nt, (8, LANE)).astype(BF16), upper)[0:1] + off_c
        r = _dot(lower, selb)
        q_ref[0, :, pl.ds(c0, LANE)] = (off_incl - nt + r).astype(I32)
        lohi_ref[0, 0:1, pl.ds(c0, LANE)] = (off_incl - nt).astype(I32)
        lohi_ref[0, 1:2, pl.ds(c0, LANE)] = off_incl.astype(I32)
        within = _dot(selb, upper)
        for e in range(N_EXP):
            chunk_sc[e, pl.ds(c, 1), :] = within[e:e + 1, :]
        return off_c + jnp.sum(nt, axis=1, keepdims=True)

    lax.fori_loop(0, nch, offsets, jnp.zeros((1, 1), F32))

    ncp = chunk_sc.shape[1]
    cpad = idx_ref.shape[2]
    cr_ = lax.broadcasted_iota(I32, (ncp, ncp), 0)
    cc_ = lax.broadcasted_iota(I32, (ncp, ncp), 1)
    lower_incl = (cc_ <= cr_).astype(BF16)
    slot = lax.broadcasted_iota(I32, (1, cpad), 1).astype(F32)
    chunk_id = lax.broadcasted_iota(I32, (ncp, cpad), 0).astype(F32)
    for e in range(N_EXP):
        within = chunk_sc[e]
        tot = jnp.broadcast_to(within[:, LANE - 1:LANE], (ncp, LANE))
        cend = _dot(lower_incl, tot.astype(BF16))
        cstart = cend - tot
        cend_t = jnp.concatenate([cend] * (cpad // LANE), axis=1)
        cstart_t = jnp.concatenate([cstart] * (cpad // LANE), axis=1)
        cstar = jnp.sum(jnp.where(cend_t <= slot, 1.0, 0.0), axis=0, keepdims=True)
        onehot = chunk_id == cstar
        srel = slot - jnp.sum(jnp.where(onehot, cstart_t, 0.0), axis=0, keepdims=True)
        g = lax.dot_general(within.astype(BF16), jnp.where(onehot, 1.0, 0.0).astype(BF16), (((0,), (0,)), ((), ())),
                            preferred_element_type=F32)
        lane_in = jnp.sum(jnp.where(g <= srel, 1.0, 0.0), axis=0, keepdims=True)
        idx = jnp.where(slot < cap, cstar * LANE + lane_in, 0.0)
        idx_ref[0, e:e + 1, :] = idx.astype(I32)


def _topk(aff_t, n_real, cap, cp):
    B, E, Lp = aff_t.shape
    spec = pl.BlockSpec((1, E, Lp), lambda b: (b, 0, 0))
    ncp = -(-(Lp // LANE) // 8) * 8
    return pl.pallas_call(
        functools.partial(_topk_body, n_real=n_real, cap=cap),
        out_shape=(jax.ShapeDtypeStruct((B, E, cp), I32), jax.ShapeDtypeStruct((B, E, Lp), I32),
                   jax.ShapeDtypeStruct((B, 2, Lp), I32)),
        grid=(B,),
        in_specs=[spec],
        out_specs=(pl.BlockSpec((1, E, cp), lambda b: (b, 0, 0)), spec, pl.BlockSpec((1, 2, Lp), lambda b: (b, 0, 0))),
        scratch_shapes=[pltpu.VMEM((E, Lp), F32), pltpu.VMEM((E, ncp, LANE), F32)],
        compiler_params=_cp(("parallel",)),
        name="expert_topk",
    )(aff_t)


def _ffn_body(idxg_ref, idxw_ref, q_ref, xe_ref, wg_ref, wu_ref, wd_ref, z_ref, xbuf, x2d_sc, xb_sc, yacc, ybuf,
              gsem, ssem, *, cap, zr, n_tiles, n_r, n_f):
    s = pl.program_id(0)
    f = pl.program_id(1)
    nf = pl.num_programs(1)
    tr, D = xb_sc.shape
    B = xe_ref.shape[0]
    total = N_EXP * cap
    per = tr // n_f

    def tile(t):
        t = jnp.clip(t, 0, n_tiles - 1)
        return t // (B * n_r), (t // n_r) % B, (t % n_r) * tr

    @pl.when(f == 0)
    def _():
        @pl.when(s == 0)
        def _():
            xb_sc[...] = jnp.zeros(xb_sc.shape, BF16)
            ybuf[...] = jnp.zeros(ybuf.shape, F32)
            for bb in range(B):
                for r0 in range(total, zr + tr, tr):
                    n = min(tr, zr + tr - r0)
                    cp = pltpu.make_async_copy(ybuf.at[0:n], z_ref.at[bb, r0:r0 + n], ssem)
                    cp.start()
                    cp.wait()

        @pl.when(s >= 1)
        def _():
            pltpu.make_async_copy(xe_ref.at[0, 0:tr], xbuf, gsem).wait()
            x2d_sc[...] = xbuf[...].reshape(x2d_sc.shape)
            xb_sc[...] = x2d_sc[:, 0:D].astype(BF16)

    x = xb_sc[...]
    g = _dot(x, wg_ref[0, 0].astype(BF16))
    u = _dot(x, wu_ref[0, 0].astype(BF16))
    hmid = (g * jax.nn.sigmoid(g) * u).astype(BF16)
    contrib = _dot(hmid, wd_ref[0, 0].astype(BF16))

    _, gb, g0 = tile(s)
    _, wb, w0 = tile(s - 2)
    for i in range(per):
        j = f * per + i
        t = idxg_ref[0, 0, 0, g0 + j]
        pltpu.make_async_copy(xe_ref.at[gb, t], xbuf.at[j], gsem).start()
        live = (s >= 2) & (w0 + j < cap)
        dst = jnp.where(live, q_ref[0, 0, 0, idxw_ref[0, 0, 0, w0 + j]], zr + j)
        pltpu.make_async_copy(ybuf.at[j], z_ref.at[wb, dst], ssem).start()

    @pl.when(f == 0)
    def _():
        yacc[...] = contrib

    @pl.when(f > 0)
    def _():
        yacc[...] += contrib

    @pl.when(f == nf - 1)
    def _():
        pltpu.make_async_copy(ybuf, z_ref.at[0, 0:tr], ssem).wait()

        @pl.when(s >= 1)
        def _():
            e, _, _ = tile(s - 1)
            aff = x2d_sc[:, D:D + LANE]
            lane = lax.broadcasted_iota(I32, aff.shape, 1)
            gate = jnp.sum(jnp.where(lane == e, aff, 0.0), axis=1, keepdims=True)
            ybuf[...] = (yacc[...] * gate).reshape(tr, 1, D)

        @pl.when(s == pl.num_programs(0) - 1)
        def _():
            pltpu.make_async_copy(xe_ref.at[0, 0:tr], xbuf, gsem).wait()


def _ffn(idx, q, xe, wg, wu, wd, layer, cap, cr, zr, tf):
    B, E, _, cp = idx.shape
    Lp = q.shape[2]
    De = xe.shape[3]
    D, F = wg.shape[2], wg.shape[3]
    n_r = 3
    tr = cr // n_r
    n_f = F // tf
    n_tiles = E * B * n_r

    def at_tile(off):
        def index_map(s, f):
            t = jnp.clip(s + off, 0, n_tiles - 1)
            return ((t // n_r) % B, t // (B * n_r), 0, 0)
        return index_map

    expert = lambda s: jnp.clip(s - 1, 0, n_tiles - 1) // (B * n_r)
    smem = lambda n, off: pl.BlockSpec((1, 1, 1, n), at_tile(off), memory_space=pltpu.SMEM)
    return pl.pallas_call(
        functools.partial(_ffn_body, cap=cap, zr=zr, n_tiles=n_tiles, n_r=n_r, n_f=n_f),
        out_shape=jax.ShapeDtypeStruct((B, zr + tr, 1, D), F32),
        grid=(n_tiles + 2, n_f),
        in_specs=[smem(cp, 0), smem(cp, -2), smem(Lp, -2), pl.BlockSpec(memory_space=pl.ANY),
                  pl.BlockSpec((1, 1, D, tf), lambda s, f: (layer, expert(s), 0, f)),
                  pl.BlockSpec((1, 1, D, tf), lambda s, f: (layer, expert(s), 0, f)),
                  pl.BlockSpec((1, 1, tf, D), lambda s, f: (layer, expert(s), f, 0))],
        out_specs=pl.BlockSpec(memory_space=pl.ANY),
        scratch_shapes=[pltpu.VMEM((tr, 1, De), F32), pltpu.VMEM((tr, De), F32), pltpu.VMEM((tr, D), BF16),
                        pltpu.VMEM((tr, D), F32),
                        pltpu.VMEM((tr, 1, D), F32), pltpu.SemaphoreType.DMA(()), pltpu.SemaphoreType.DMA(())],
        compiler_params=_cp(("arbitrary", "arbitrary"), 56),
        name="expert_ffn",
    )(idx, idx, q.reshape(B, E, 1, Lp), xe, wg, wu, wd)


def _combine_body(k0_ref, k1_ref, h_ref, lohi_ref, g_ref, z_ref, *rest, nblk, want_h):
    if want_h:
        h2_ref, u_ref, buf, zb_sc, acc_sc, sem = rest
    else:
        h2_ref, (u_ref, buf, zb_sc, acc_sc, sem) = None, rest
    b = pl.program_id(0)
    n = b * nblk + pl.program_id(1)
    n_total = pl.num_programs(0) * nblk
    k0 = k0_ref[n]
    k1 = k1_ref[n]
    lo = lohi_ref[0, 0:1, :]
    hi = lohi_ref[0, 1:2, :]
    acc_sc[...] = jnp.zeros(acc_sc.shape, F32)

    def copy(bb, k):
        r0 = pl.multiple_of(k * ZCHUNK, ZCHUNK)
        slot = lax.rem(k, 2)
        return pltpu.make_async_copy(z_ref.at[bb, pl.ds(r0, ZCHUNK)], buf.at[slot], sem.at[slot])

    def resident(n_prev, n_next):
        return ((n_prev // nblk == n_next // nblk) & (k1_ref[n_prev] > k0_ref[n_prev])
                & (k1_ref[n_prev] - 1 == k0_ref[n_next]))

    @pl.when(k1 > k0)
    def _():
        @pl.when(n == 0)
        def _():
            copy(b, k0).start()

        @pl.when((n == 0) | jnp.logical_not(resident(jnp.maximum(n - 1, 0), n)))
        def _():
            copy(b, k0).wait()

    def chunk(k, c):
        @pl.when(k + 1 < k1)
        def _():
            copy(b, k + 1).start()

        w = k * ZCHUNK + lax.broadcasted_iota(I32, (ZCHUNK, lo.shape[1]), 0)
        band_t = jnp.where((w >= lo) & (w < hi), 1.0, 0.0)
        zb_sc[...] = buf[lax.rem(k, 2)].reshape(zb_sc.shape)
        acc_sc[...] += _dot(band_t.T.astype(BF16), zb_sc[...].astype(BF16))

        @pl.when(k + 1 < k1)
        def _():
            copy(b, k + 1).wait()
        return c

    lax.fori_loop(k0, k1, chunk, 0)

    n_next = jnp.minimum(n + 1, n_total - 1)

    @pl.when((n + 1 < n_total) & (k1_ref[n_next] > k0_ref[n_next]) & jnp.logical_not(resident(n, n_next)))
    def _():
        copy(n_next // nblk, k0_ref[n_next]).start()

    h2 = h_ref[0] + acc_sc[...]
    if want_h:
        h2_ref[0] = h2
    u_ref[0] = _rms(h2, g_ref[...]).astype(u_ref.dtype)


def _combine(h1, lohi, z, g, n_rows, u_dtype, want_h):
    B, Lp, D = h1.shape
    nblk = n_rows // TOKB
    lo = lohi[:, 0, :n_rows].reshape(B, nblk, TOKB)
    hi = lohi[:, 1, :n_rows].reshape(B, nblk, TOKB)
    k0 = (lo[:, :, 0] // ZCHUNK).reshape(-1).astype(I32)
    k1 = ((hi[:, :, -1] + ZCHUNK - 1) // ZCHUNK).reshape(-1).astype(I32)
    k1 = jnp.where(hi[:, :, -1].reshape(-1) > lo[:, :, 0].reshape(-1), k1, k0)
    row = lambda b, j, *_: (b, j, 0)
    grid_spec = pltpu.PrefetchScalarGridSpec(
        num_scalar_prefetch=2,
        grid=(B, nblk),
        in_specs=[pl.BlockSpec((1, TOKB, D), row),
                  pl.BlockSpec((1, 2, TOKB), lambda b, j, *_: (b, 0, j)),
                  pl.BlockSpec((1, D), lambda b, j, *_: (0, 0)),
                  pl.BlockSpec(memory_space=pl.ANY)],
        out_specs=(pl.BlockSpec((1, TOKB, D), row),) * (2 if want_h else 1),
        scratch_shapes=[pltpu.VMEM((2, ZCHUNK, 1, D), z.dtype), pltpu.VMEM((ZCHUNK, D), F32),
                        pltpu.VMEM((TOKB, D), F32), pltpu.SemaphoreType.DMA((2,))],
    )
    out_shape = (jax.ShapeDtypeStruct((B, n_rows, D), F32),) * want_h + (jax.ShapeDtypeStruct((B, n_rows, D), u_dtype),)
    outs = pl.pallas_call(
        functools.partial(_combine_body, nblk=nblk, want_h=want_h),
        out_shape=out_shape,
        grid_spec=grid_spec,
        compiler_params=_cp(("arbitrary", "arbitrary")),
        name="moe_combine",
    )(k0, k1, h1, lohi, g.reshape(1, D), z)
    return outs if want_h else (None, outs[0])


def _moe(h1, xe, aff_t, wg, wu, wd, layer, g_next, n_real, n_rows_out, u_dtype, want_h):
    B, Lp, D = h1.shape
    cap = (CAP_F * (n_real + N_META)) // N_EXP
    cr = -(-cap // 48) * 48
    cp = -(-(cap + 1) // LANE) * LANE
    zr = -(-(N_EXP * cap) // ZCHUNK) * ZCHUNK
    if zr == N_EXP * cap:
        zr += ZCHUNK
    idx, q, lohi = _topk(aff_t, n_real, cap, cp)
    idx = idx.reshape(B, N_EXP, 1, cp)
    z = _ffn(idx, q, xe, wg, wu, wd, layer, cap, cr, zr, min(wg.shape[3], 256))
    return _combine(h1, lohi, z, g_next, n_rows_out, u_dtype, want_h)


def _rope_tables(n_real, Lp):
    pos = jnp.concatenate([jnp.arange(n_real) + N_META, jnp.arange(N_META), jnp.zeros((Lp - n_real - N_META,), I32)])
    inv_freq = 1.0 / (ROPE_THETA ** (jnp.arange(0, MLA_ROPE, 2, dtype=F32) / MLA_ROPE))
    ang = pos.astype(F32)[:, None] * inv_freq[None, :]
    cos, sin = jnp.cos(ang), jnp.sin(ang)
    half = MLA_ROPE // 2
    z = lambda n: jnp.zeros((Lp, n), F32)
    c = jnp.concatenate([cos, cos, z(LANE - 2 * half)], axis=1)
    sn = jnp.concatenate([-sin, z(LANE - half)], axis=1)
    sp = jnp.concatenate([z(half), sin, z(LANE - 2 * half)], axis=1)
    return (c, sn, sp), (cos.T, sin.T)


def kernel(x, meta_tokens, norm_mix, norm_ffn, norm_final, ab_w_in, ab_q_norm, ab_kv_norm, ab_w_uq, ab_w_ukv, ab_rpb, ab_w_out, c_w_in, c_sinks, c_w_out, ec_w_router, ec_w_gate, ec_w_up, ec_w_down):
    B, S, D = x.shape
    depth = norm_mix.shape[0]
    Lp = S + TAIL
    tm = 512
    h, u = _embed(x, meta_tokens.astype(x.dtype), norm_mix[0], tm)
    tabs, tabs_t = _rope_tables(S, Lp)

    for layer in range(depth):
        if layer % 2 == 0:
            e = layer // 2
            w_in = ab_w_in[e].astype(BF16)
            o2 = MLA_QL + MLA_KVL
            o3 = o2 + MLA_ROPE
            lat = _mm(u, w_in[:, :o2], tm, o2)
            krope = _mm(u, jnp.pad(w_in[:, o2:o3], ((0, 0), (0, LANE - MLA_ROPE))), tm, LANE)
            nw = NA_H * NA_D
            na_wq = ab_w_in[e][:, o3:o3 + nw] * (NA_D ** -0.5 * math.log2(math.e))
            na_wqv_t = jnp.concatenate([na_wq, ab_w_in[e][:, o3 + 2 * nw:]], axis=1).T.astype(BF16)
            na_qvt = _mm_t(u, na_wqv_t, tm, nw)
            na_k = _mm(u, w_in[:, o3 + nw:o3 + 2 * nw], tm, nw)
            wq = jnp.pad(ab_w_uq[e].reshape(MLA_QL, MLA_H, MLA_NOPE + MLA_ROPE),
                         ((0, 0), (0, 0), (0, MLA_HP - MLA_NOPE - MLA_ROPE))).reshape(MLA_QL, MLA_H * MLA_HP)
            wkv = ab_w_ukv[e].reshape(MLA_KVL, MLA_H, MLA_NOPE + MLA_V)
            wk = wkv[:, :, :MLA_NOPE].reshape(MLA_KVL, -1)
            wv = wkv[:, :, MLA_NOPE:].reshape(MLA_KVL, -1)
            qt, k, vt = _mla_up(lat, krope, ab_q_norm[e], ab_kv_norm[e], wq.T.astype(BF16), wk.astype(BF16),
                                wv.T.astype(BF16), tabs, tabs_t, tm)
            y_mla = _mla_attention(qt, k, vt, S, TAIL, min(S, 512))
            y_na = _na_attention(na_qvt, na_k, _na_bias_table(ab_rpb[e]), S)
            ys, y_t = [y_mla, y_na], (False, True)
            w_out = ab_w_out[e]
        else:
            o = layer // 2
            qw, kw = SWA_H * SWA_D, SWA_KV * SWA_D
            w_in = c_w_in[o]
            wq = w_in[:, :qw] * (SWA_D ** -0.5 * math.log2(math.e))
            wqv_t = jnp.concatenate([wq, w_in[:, qw + kw:]], axis=1).T.astype(BF16)
            qvt = _mm_t(u, wqv_t, tm, (qw + kw) // 3)
            k = _mm(u, w_in[:, qw:qw + kw].astype(BF16), tm, kw)
            ys, y_t = [_swa_attention(qvt, k, c_sinks[o], S)], (True,)
            w_out = c_w_out[o]
        wr = jnp.pad(ec_w_router[layer], ((0, 0), (0, LANE - N_EXP)))
        wr_hi = wr.astype(BF16)
        wr = jnp.concatenate([wr_hi, (wr - wr_hi.astype(F32)).astype(BF16)], axis=1)
        h1, xe, aff_t = _outproj(h, ys, w_out.astype(BF16), norm_ffn[layer], wr, 256, y_t)
        last = layer == depth - 1
        g_next = norm_final if last else norm_mix[layer + 1]
        h, u = _moe(h1, xe, aff_t, ec_w_gate, ec_w_up, ec_w_down, layer, g_next, S,
                    S if last else Lp, F32 if last else BF16, want_h=not last)
    return u
```

```python
import functools
import math

import jax
import jax.numpy as jnp
from jax import lax
from jax.experimental import pallas as pl
from jax.experimental.pallas import tpu as pltpu

F32 = jnp.float32
BF16 = jnp.bfloat16
I32 = jnp.int32

N_META = 16
GRID_W = 64
QBLK = 128
EPS = 1e-6
NEG = -1e30

MLA_H = 8
MLA_NOPE = 128
MLA_ROPE = 64
MLA_V = 128
MLA_QL = 512
MLA_KVL = 512
MLA_HP = 256
ROPE_THETA = 10000.0

NA_H = 8
NA_D = 128
NA_KH = 8
NA_KW = 16

SWA_H = 32
SWA_KV = 4
SWA_G = SWA_H // SWA_KV
SWA_D = 64
SWA_WIN = 128

N_EXP = 16
CAP_F = 2

TAIL = 512
LANE = 128
ZCHUNK = 256
TOKB = 512


def _cp(sem, vmem_mb=48):
    return pltpu.CompilerParams(dimension_semantics=sem, vmem_limit_bytes=vmem_mb << 20)


def _dot(a, b):
    return jnp.dot(a, b, preferred_element_type=F32)


def _dot_nt(a, b):
    return lax.dot_general(a, b, (((1,), (1,)), ((), ())), preferred_element_type=F32)


def _rms(x, g):
    return x * lax.rsqrt(jnp.mean(x * x, axis=-1, keepdims=True) + EPS) * g


def _embed_body(x_ref, meta_ref, g_ref, h_ref, u_ref, *, n_blocks):
    i = pl.program_id(1)

    @pl.when(i < n_blocks)
    def _():
        h_ref[0] = x_ref[0]

    @pl.when(i >= n_blocks)
    def _():
        h_ref[0] = jnp.zeros(h_ref.shape[1:], h_ref.dtype)
        h_ref[0, 0:N_META, :] = meta_ref[...]

    u_ref[0] = _rms(h_ref[0], g_ref[...]).astype(u_ref.dtype)


def _embed(x, meta, g, tm):
    B, S, D = x.shape
    Lp = S + TAIL
    nb = S // tm
    row = lambda b, i: (b, i, 0)
    return pl.pallas_call(
        functools.partial(_embed_body, n_blocks=nb),
        out_shape=(jax.ShapeDtypeStruct((B, Lp, D), x.dtype), jax.ShapeDtypeStruct((B, Lp, D), BF16)),
        grid=(B, Lp // tm),
        in_specs=[pl.BlockSpec((1, tm, D), lambda b, i: (b, jnp.minimum(i, nb - 1), 0)),
                  pl.BlockSpec(meta.shape, lambda b, i: (0, 0)), pl.BlockSpec((1, D), lambda b, i: (0, 0))],
        out_specs=(pl.BlockSpec((1, tm, D), row), pl.BlockSpec((1, tm, D), row)),
        compiler_params=_cp(("parallel", "parallel")),
        name="embed_rmsnorm",
    )(x, meta, g.reshape(1, D))


def _mm_body(x_ref, w_ref, o_ref):
    o_ref[0] = _dot(x_ref[0], w_ref[...]).astype(o_ref.dtype)


def _mm(x, w, tm, tn):
    B, Lp, K = x.shape
    N = w.shape[1]
    return pl.pallas_call(
        _mm_body,
        out_shape=jax.ShapeDtypeStruct((B, Lp, N), BF16),
        grid=(B, Lp // tm, N // tn),
        in_specs=[pl.BlockSpec((1, tm, K), lambda b, i, j: (b, i, 0)), pl.BlockSpec((K, tn), lambda b, i, j: (0, j))],
        out_specs=pl.BlockSpec((1, tm, tn), lambda b, i, j: (b, i, j)),
        compiler_params=_cp(("parallel", "parallel", "parallel")),
        name="in_proj",
    )(x, w)


def _mm_t_body(x_ref, wt_ref, o_ref):
    o_ref[0] = _dot_nt(wt_ref[...], x_ref[0]).astype(o_ref.dtype)


def _mm_t(x, wt, tm, tn):
    B, Lp, K = x.shape
    N = wt.shape[0]
    return pl.pallas_call(
        _mm_t_body,
        out_shape=jax.ShapeDtypeStruct((B, N, Lp), BF16),
        grid=(B, Lp // tm, N // tn),
        in_specs=[pl.BlockSpec((1, tm, K), lambda b, i, j: (b, i, 0)), pl.BlockSpec((tn, K), lambda b, i, j: (j, 0))],
        out_specs=pl.BlockSpec((1, tn, tm), lambda b, i, j: (b, j, i)),
        compiler_params=_cp(("parallel", "parallel", "parallel")),
        name="in_proj_t",
    )(x, wt)


def _rope(pe, c, sn, sp):
    return pe * c + pltpu.roll(pe, 96, 1) * sn + pltpu.roll(pe, 32, 1) * sp


def _uq_body(lat_ref, g_ref, wt_ref, c_ref, s_ref, o_ref, *, scale):
    xn = _rms(lat_ref[0].astype(F32), g_ref[...]).astype(BF16)
    zt = _dot_nt(wt_ref[...], xn)
    c, s = c_ref[...], s_ref[...]
    half = MLA_ROPE // 2
    dt = o_ref.dtype
    for h in range(MLA_H):
        o = h * MLA_HP
        p1, p2, p3 = o + MLA_NOPE, o + MLA_NOPE + half, o + MLA_NOPE + MLA_ROPE
        x1, x2 = zt[p1:p2], zt[p2:p3]
        o_ref[0, o:p1, :] = (zt[o:p1] * scale).astype(dt)
        o_ref[0, p1:p2, :] = ((x1 * c - x2 * s) * scale).astype(dt)
        o_ref[0, p2:p3, :] = ((x2 * c + x1 * s) * scale).astype(dt)
        o_ref[0, p3:o + MLA_HP, :] = jnp.zeros((o + MLA_HP - p3, zt.shape[1]), dt)


def _ukv_body(lat_ref, g_ref, wk_ref, wvt_ref, kr_ref, c_ref, sn_ref, sp_ref, k_ref, vt_ref):
    xn = _rms(lat_ref[0].astype(F32), g_ref[...]).astype(BF16)
    zk = _dot(xn, wk_ref[...])
    kr = _rope(kr_ref[0].astype(F32), c_ref[...], sn_ref[...], sp_ref[...]).astype(k_ref.dtype)
    for h in range(MLA_H):
        k_ref[0, :, h * MLA_HP:h * MLA_HP + LANE] = zk[:, h * LANE:(h + 1) * LANE].astype(k_ref.dtype)
        k_ref[0, :, h * MLA_HP + LANE:(h + 1) * MLA_HP] = kr
    vt_ref[0] = _dot_nt(wvt_ref[...], xn).astype(vt_ref.dtype)


def _mla_up(lat, krope, qn, kvn, wqt, wk, wvt, tabs, tabs_t, tm):
    B, Lp, _ = lat.shape
    c, sn, sp = tabs
    ct, st = tabs_t
    tab_spec = pl.BlockSpec((tm, LANE), lambda b, i: (i, 0))
    tabt_spec = pl.BlockSpec((MLA_ROPE // 2, tm), lambda b, i: (0, i))
    scale = (MLA_NOPE + MLA_ROPE) ** -0.5 * math.log2(math.e)
    qt = pl.pallas_call(
        functools.partial(_uq_body, scale=scale),
        out_shape=jax.ShapeDtypeStruct((B, MLA_H * MLA_HP, Lp), BF16),
        grid=(B, Lp // tm),
        in_specs=[pl.BlockSpec((1, tm, MLA_QL), lambda b, i: (b, i, 0)),
                  pl.BlockSpec((1, MLA_QL), lambda b, i: (0, 0)),
                  pl.BlockSpec(wqt.shape, lambda b, i: (0, 0)),
                  tabt_spec, tabt_spec],
        out_specs=pl.BlockSpec((1, MLA_H * MLA_HP, tm), lambda b, i: (b, 0, i)),
        compiler_params=_cp(("parallel", "parallel")),
        name="mla_q_up",
    )(lat, qn.reshape(1, -1), wqt, ct, st)
    k, vt = pl.pallas_call(
        _ukv_body,
        out_shape=(jax.ShapeDtypeStruct((B, Lp, MLA_H * MLA_HP), BF16),
                   jax.ShapeDtypeStruct((B, MLA_H * MLA_V, Lp), BF16)),
        grid=(B, Lp // tm),
        in_specs=[pl.BlockSpec((1, tm, MLA_KVL), lambda b, i: (b, i, 1)),
                  pl.BlockSpec((1, MLA_KVL), lambda b, i: (0, 0)),
                  pl.BlockSpec(wk.shape, lambda b, i: (0, 0)),
                  pl.BlockSpec(wvt.shape, lambda b, i: (0, 0)),
                  pl.BlockSpec((1, tm, LANE), lambda b, i: (b, i, 0)),
                  tab_spec, tab_spec, tab_spec],
        out_specs=(pl.BlockSpec((1, tm, MLA_H * MLA_HP), lambda b, i: (b, i, 0)),
                   pl.BlockSpec((1, MLA_H * MLA_V, tm), lambda b, i: (b, 0, i))),
        compiler_params=_cp(("parallel", "parallel")),
        name="mla_kv_up",
    )(lat, kvn.reshape(1, -1), wk, wvt, krope, c, sn, sp)
    return qt, k, vt


def _mla_body(qt_ref, k_ref, vt_ref, o_ref, m_sc, acc_sc, s_sc, *, n_real, tkc, ahead):
    qt = qt_ref[0]
    tq = qt.shape[1]
    nch = n_real // tkc
    ring = s_sc.shape[0]
    m_sc[...] = jnp.full(m_sc.shape, NEG, F32)
    acc_sc[...] = jnp.zeros(acc_sc.shape, F32)

    def scores(c):
        return _dot(k_ref[0, pl.ds(pl.multiple_of(c * tkc, tkc), tkc), :], qt)

    def consume(st, vtc):
        m_prev = m_sc[...]
        m_new = jnp.maximum(m_prev, jnp.max(st, axis=0, keepdims=True))
        p = jnp.exp2((st - m_new).astype(BF16))
        alpha = jnp.exp2(m_prev - m_new)
        vte = jnp.concatenate([vtc, jnp.ones((16, vtc.shape[1]), BF16)], axis=0)
        acc_sc[...] = alpha * acc_sc[...] + _dot(vte, p)
        m_sc[...] = m_new

    def values(c):
        return vt_ref[0, :, pl.ds(pl.multiple_of(c * tkc, tkc), tkc)]

    for u in range(ahead):
        s_sc[u] = scores(u)

    def trip(c0, last):
        for u in range(ring):
            if not (last and u + ahead >= ring):
                s_sc[(u + ahead) % ring] = scores(c0 + u + ahead)
            consume(s_sc[u], values(c0 + u))

    def body(cb, carry):
        trip(cb * ring, False)
        return carry

    lax.fori_loop(0, nch // ring - 1, body, 0)
    trip(jnp.int32(nch - ring), True)
    consume(_dot(k_ref[0, n_real:n_real + N_META, :], qt), vt_ref[0, :, n_real:n_real + N_META])
    out = (acc_sc[0:MLA_V, :] / acc_sc[MLA_V:MLA_V + 1, :]).T
    i = pl.program_id(2)
    row = i * tq + lax.broadcasted_iota(I32, (tq, 1), 0)
    o_ref[0] = jnp.where(row < n_real + N_META, out, 0.0).astype(o_ref.dtype)


def _mla_attention(qt, k, vt, n_real, tq, tkc):
    B, Lp, _ = k.shape
    nch = n_real // tkc
    ring, ahead = (8, 3) if nch % 8 == 0 else ((4, 2) if nch % 4 == 0 else (2, 1))
    return pl.pallas_call(
        functools.partial(_mla_body, n_real=n_real, tkc=tkc, ahead=ahead),
        out_shape=jax.ShapeDtypeStruct((B, Lp, MLA_H * MLA_V), BF16),
        grid=(B, MLA_H, Lp // tq),
        in_specs=[pl.BlockSpec((1, MLA_HP, tq), lambda b, h, i: (b, h, i)),
                  pl.BlockSpec((1, Lp, MLA_HP), lambda b, h, i: (b, 0, h)),
                  pl.BlockSpec((1, MLA_V, Lp), lambda b, h, i: (b, h, 0))],
        out_specs=pl.BlockSpec((1, tq, MLA_V), lambda b, h, i: (b, i, h)),
        scratch_shapes=[pltpu.VMEM((1, tq), F32), pltpu.VMEM((MLA_V + 16, tq), F32),
                        pltpu.VMEM((ring, tkc, tq), F32)],
        compiler_params=_cp(("parallel", "parallel", "arbitrary")),
        name="mla_attention",
    )(qt, k, vt)


NA_GROUP = 4
NA_UNION = NA_KH + NA_GROUP


def _na_bias_table(rpb):
    col = jnp.arange(GRID_W)
    cs = jnp.clip(col - NA_KW // 2, 0, GRID_W - NA_KW)
    ok_c = (col[None, :] >= cs[:, None]) & (col[None, :] < cs[:, None] + NA_KW)
    dc = jnp.clip(col[None, :] - col[:, None], -(NA_KW - 1), NA_KW - 1) + (NA_KW - 1)
    i = jnp.arange(NA_GROUP)
    j = jnp.arange(NA_UNION)
    start = jnp.stack([0 * i, i, 0 * i + NA_UNION - NA_KH])
    d0 = jnp.stack([NA_KH - 1 - i, 0 * i + NA_KH // 2 - 1, NA_GROUP - 1 - i])
    rel = j[None, None, :] - start[:, :, None]
    ok_r = (rel >= 0) & (rel < NA_KH)
    dr = jnp.clip(d0[:, :, None] + rel, 0, 2 * NA_KH - 2)
    t = rpb[:, dr][:, :, :, :, dc]
    t = jnp.where(ok_r[None, :, :, :, None, None] & ok_c[None, None, None, None], t * math.log2(math.e), NEG)
    t = t.transpose(0, 1, 3, 5, 2, 4)
    return t.reshape(rpb.shape[0], 3, NA_UNION * GRID_W, NA_GROUP * GRID_W).astype(F32)


def _na_body(qt_ref, k_ref, vt_ref, t_ref, o_ref, s_sc, sm_sc, p_sc, pm_sc, l_sc, *, n_real, gps):
    i = pl.program_id(2)
    rows = n_real // GRID_W
    n_groups = rows // NA_GROUP
    gw = NA_GROUP * GRID_W
    km = k_ref[0, n_real:n_real + N_META, :]
    vmt = vt_ref[0, :, n_real:n_real + N_META]

    @pl.when(i < n_groups // gps)
    def _():
        def window(gg):
            g = i * gps + gg
            variant = jnp.where(g == 0, 0, jnp.where(g == n_groups - 1, 2, 1))
            u0 = jnp.where(g == 0, 0, jnp.where(g == n_groups - 1, rows - NA_UNION, g * NA_GROUP - NA_KH // 2))
            return variant, pl.multiple_of(u0 * GRID_W, NA_GROUP * GRID_W)

        for gg in range(gps):
            variant, k0 = window(gg)
            qt = qt_ref[0, :, gg * gw:(gg + 1) * gw]
            s_sc[gg] = _dot(k_ref[0, pl.ds(k0, NA_UNION * GRID_W), :], qt) + t_ref[0, variant]
            sm_sc[gg] = _dot(km, qt)
        for gg in range(gps):
            s, sm = s_sc[gg], sm_sc[gg]
            m = jnp.maximum(jnp.max(s, axis=0, keepdims=True), jnp.max(sm, axis=0, keepdims=True))
            p = jnp.exp2(s - m)
            pm = jnp.exp2(sm - m)
            l_sc[gg] = jnp.sum(p, axis=0, keepdims=True) + jnp.sum(pm, axis=0, keepdims=True)
            p_sc[gg] = p.astype(BF16)
            pm_sc[gg] = pm.astype(BF16)
        for gg in range(gps):
            _, k0 = window(gg)
            ot = (_dot(vt_ref[0, :, pl.ds(k0, NA_UNION * GRID_W)], p_sc[gg]) + _dot(vmt, pm_sc[gg])) / l_sc[gg]
            o_ref[0, :, gg * gw:(gg + 1) * gw] = ot.astype(o_ref.dtype)

    @pl.when(i >= n_groups // gps)
    def _():
        sm = _dot(km, qt_ref[0, :, 0:gw])
        pm = jnp.exp2(sm - jnp.max(sm, axis=0, keepdims=True))
        ot = _dot(vmt, pm.astype(BF16)) / jnp.sum(pm, axis=0, keepdims=True)
        lane = lax.broadcasted_iota(I32, ot.shape, 1)
        o_ref[0] = jnp.zeros(o_ref.shape[1:], o_ref.dtype)
        o_ref[0, :, 0:gw] = jnp.where(lane < N_META, ot, 0.0).astype(o_ref.dtype)


def _na_attention(qvt, k, table, n_real):
    B, Lp, _ = k.shape
    gps = TAIL // (NA_GROUP * GRID_W)
    gw, nk = NA_GROUP * GRID_W, NA_UNION * GRID_W
    tq = gps * gw
    assert n_real // GRID_W >= NA_UNION and (n_real // GRID_W) % (NA_GROUP * gps) == 0
    return pl.pallas_call(
        functools.partial(_na_body, n_real=n_real, gps=gps),
        out_shape=jax.ShapeDtypeStruct((B, NA_H * NA_D, Lp), BF16),
        grid=(B, NA_H, Lp // tq),
        in_specs=[pl.BlockSpec((1, NA_D, tq), lambda b, h, i: (b, h, i)),
                  pl.BlockSpec((1, Lp, NA_D), lambda b, h, i: (b, 0, h)),
                  pl.BlockSpec((1, NA_D, Lp), lambda b, h, i: (b, NA_H + h, 0)),
                  pl.BlockSpec((1,) + table.shape[1:], lambda b, h, i: (h, 0, 0, 0))],
        out_specs=pl.BlockSpec((1, NA_D, tq), lambda b, h, i: (b, h, i)),
        scratch_shapes=[pltpu.VMEM((gps, nk, gw), F32), pltpu.VMEM((gps, N_META, gw), F32),
                        pltpu.VMEM((gps, nk, gw), BF16), pltpu.VMEM((gps, N_META, gw), BF16),
                        pltpu.VMEM((gps, 1, gw), F32)],
        compiler_params=_cp(("parallel", "parallel", "arbitrary")),
        name="na_attention",
    )(qvt, k, qvt, table)


def _swa_bias_tables():
    a = jnp.arange(QBLK)[None, :]
    j = jnp.arange(3 * QBLK)[:, None]
    dist = jnp.abs(j - QBLK - a)
    slopes = 2.0 ** (-8.0 * jnp.arange(1, SWA_H + 1, dtype=F32) / SWA_H)
    real = jnp.where(dist <= SWA_WIN, -slopes[:, None, None] * dist.astype(F32) * math.log2(math.e), NEG)
    meta = jnp.where(N_META + jnp.arange(QBLK)[:, None] - a <= SWA_WIN, 0.0, NEG)
    return real.astype(F32), meta.astype(F32)


def _swa_body(sink_ref, qt_ref, kp_ref, kc_ref, kn_ref, km_ref, vp_ref, vc_ref, vn_ref, vm_ref, tr_ref, tm_ref, o_ref,
              st_sc, sm_sc, p_sc, pm_sc, l_sc, *, n_real):
    i = pl.program_id(1)
    nb = n_real // QBLK
    log2e = math.log2(math.e)
    gw = SWA_G * SWA_D
    kmeta = km_ref[0, 0:N_META, :]
    vmeta_t = vm_ref[0, :, 0:N_META]

    def attend(kall, vall_t, bias_of):
        nk = kall.shape[0]
        for kv in range(SWA_KV):
            tile, half = kv // 2, kv % 2
            q64 = jnp.concatenate([qt_ref[0, kv * gw + g * SWA_D:kv * gw + (g + 1) * SWA_D, :] for g in range(SWA_G)],
                                  axis=1)
            z64 = jnp.zeros_like(q64)
            qpad = jnp.concatenate([q64, z64] if half == 0 else [z64, q64], axis=0)
            st_sc[kv, 0:nk] = _dot(kall[:, tile * LANE:(tile + 1) * LANE], qpad)
            sm_sc[kv] = _dot(kmeta[:, tile * LANE:(tile + 1) * LANE], qpad)
        for kv in range(SWA_KV):
            for g in range(SWA_G):
                h = kv * SWA_G + g
                cols = slice(g * QBLK, (g + 1) * QBLK)
                s = st_sc[kv, 0:nk, cols] + bias_of(h)
                smg = sm_sc[kv, :, cols]
                sink = sink_ref[h] * log2e
                m = jnp.maximum(jnp.maximum(jnp.max(s, axis=0, keepdims=True), jnp.max(smg, axis=0, keepdims=True)),
                                sink)
                p = jnp.exp2(s - m)
                pm = jnp.exp2(smg - m)
                l_sc[kv, :, cols] = (jnp.sum(p, axis=0, keepdims=True) + jnp.sum(pm, axis=0, keepdims=True)
                                     + jnp.exp2(sink - m))
                p_sc[kv, 0:nk, cols] = p.astype(BF16)
                pm_sc[kv, :, cols] = pm.astype(BF16)
        for kv in range(SWA_KV):
            rows = slice(kv * SWA_D, (kv + 1) * SWA_D)
            ot = (_dot(vall_t[rows, :], p_sc[kv, 0:nk]) + _dot(vmeta_t[rows, :], pm_sc[kv])) / l_sc[kv]
            for g in range(SWA_G):
                o_ref[0, kv * gw + g * SWA_D:kv * gw + (g + 1) * SWA_D, :] = ot[:, g * QBLK:(g + 1) * QBLK].astype(
                    o_ref.dtype)

    @pl.when(i < nb)
    def _():
        kall = jnp.concatenate([kp_ref[0], kc_ref[0], kn_ref[0]], axis=0)
        vall_t = jnp.concatenate([vp_ref[0], vc_ref[0], vn_ref[0]], axis=1)
        row = lax.broadcasted_iota(I32, (3 * QBLK, QBLK), 0)
        pen = jnp.where(((i == 0) & (row < QBLK)) | ((i == nb - 1) & (row >= 2 * QBLK)), NEG, 0.0)
        attend(kall, vall_t, lambda h: tr_ref[h] + pen)

    @pl.when(i == nb)
    def _():
        attend(kn_ref[0], vn_ref[0], lambda h: tm_ref[...])
        lane = lax.broadcasted_iota(I32, o_ref.shape[1:], 1)
        o_ref[0] = jnp.where(lane < N_META, o_ref[0], jnp.zeros(o_ref.shape[1:], o_ref.dtype))

    @pl.when(i > nb)
    def _():
        o_ref[0] = jnp.zeros(o_ref.shape[1:], o_ref.dtype)


def _swa_attention(qvt, k, sinks, n_real):
    B, Lp, kw = k.shape
    nb = n_real // QBLK
    qw = SWA_H * SWA_D
    vrow = qw // kw
    prev = lambda i: jnp.clip(i - 1, 0, nb - 1)
    cur = lambda i: jnp.minimum(i, nb - 1)
    nxt = lambda i: jnp.where(i >= nb, 0, jnp.minimum(i + 1, nb - 1))
    met = lambda i: nb
    kspec = lambda f: pl.BlockSpec((1, QBLK, kw), lambda b, i: (b, f(i), 0))
    vspec = lambda f: pl.BlockSpec((1, kw, QBLK), lambda b, i: (b, vrow, f(i)))
    t_real, t_meta = _swa_bias_tables()
    return pl.pallas_call(
        functools.partial(_swa_body, n_real=n_real),
        out_shape=jax.ShapeDtypeStruct((B, qw, Lp), BF16),
        grid=(B, Lp // QBLK),
        in_specs=[pl.BlockSpec(memory_space=pltpu.SMEM),
                  pl.BlockSpec((1, qw, QBLK), lambda b, i: (b, 0, i)),
                  kspec(prev), kspec(cur), kspec(nxt), kspec(met),
                  vspec(prev), vspec(cur), vspec(nxt), vspec(met),
                  pl.BlockSpec(t_real.shape, lambda b, i: (0, 0, 0)),
                  pl.BlockSpec(t_meta.shape, lambda b, i: (0, 0))],
        out_specs=pl.BlockSpec((1, qw, QBLK), lambda b, i: (b, 0, i)),
        scratch_shapes=[pltpu.VMEM((SWA_KV, 3 * QBLK, SWA_G * QBLK), F32), pltpu.VMEM((SWA_KV, N_META, SWA_G * QBLK), F32),
                        pltpu.VMEM((SWA_KV, 3 * QBLK, SWA_G * QBLK), BF16),
                        pltpu.VMEM((SWA_KV, N_META, SWA_G * QBLK), BF16), pltpu.VMEM((SWA_KV, 1, SWA_G * QBLK), F32)],
        compiler_params=_cp(("parallel", "arbitrary")),
        name="swa_attention",
    )(sinks, qvt, k, k, k, k, qvt, qvt, qvt, qvt, t_real, t_meta)


def _outproj_body(*refs, n_y, y_t):
    h_ref = refs[0]
    y_refs = refs[1:1 + n_y]
    wo_ref, g_ref, wr_ref, h1_ref, xe_ref, aff_ref = refs[1 + n_y:]
    D = h_ref.shape[2]
    acc = h_ref[0]
    k0 = 0
    for y_ref, t in zip(y_refs, y_t):
        if t:
            kk = y_ref.shape[1]
            acc = acc + lax.dot_general(y_ref[0], wo_ref[k0:k0 + kk, :], (((0,), (0,)), ((), ())),
                                        preferred_element_type=F32)
        else:
            kk = y_ref.shape[2]
            acc = acc + _dot(y_ref[0], wo_ref[k0:k0 + kk, :])
        k0 += kk
    h1_ref[0] = acc
    xn = _rms(acc, g_ref[...])
    xh = xn.astype(BF16)
    xl = (xn - xh.astype(F32)).astype(BF16)
    hh = _dot(xh, wr_ref[...])
    logits = hh[:, 0:LANE] + hh[:, LANE:2 * LANE] + _dot(xl, wr_ref[:, 0:LANE])
    lane = lax.broadcasted_iota(I32, logits.shape, 1)
    logits = jnp.where(lane < N_EXP, logits, NEG)
    e = jnp.exp(logits - jnp.max(logits, axis=1, keepdims=True))
    aff = e / jnp.sum(e, axis=1, keepdims=True)
    tm = xn.shape[0]
    xe_ref[0, :, :, 0:D] = xn.reshape(tm, 1, D)
    xe_ref[0, :, :, D:D + LANE] = aff.reshape(tm, 1, LANE)
    aff_ref[0] = aff.T[0:N_EXP, :]


def _outproj(h, ys, wo, g, wr, tm, y_t):
    B, Lp, D = h.shape
    row = lambda b, i: (b, i, 0)
    full = lambda b, i: (0, 0)
    y_spec = lambda y, t: (pl.BlockSpec((1, y.shape[1], tm), lambda b, i: (b, 0, i)) if t
                           else pl.BlockSpec((1, tm, y.shape[2]), row))
    return pl.pallas_call(
        functools.partial(_outproj_body, n_y=len(ys), y_t=y_t),
        out_shape=(jax.ShapeDtypeStruct((B, Lp, D), F32),
                   jax.ShapeDtypeStruct((B, Lp, 1, D + LANE), F32),
                   jax.ShapeDtypeStruct((B, N_EXP, Lp), F32)),
        grid=(B, Lp // tm),
        in_specs=[pl.BlockSpec((1, tm, D), row)] + [y_spec(y, t) for y, t in zip(ys, y_t)]
        + [pl.BlockSpec(wo.shape, full), pl.BlockSpec((1, D), full), pl.BlockSpec(wr.shape, full)],
        out_specs=(pl.BlockSpec((1, tm, D), row), pl.BlockSpec((1, tm, 1, D + LANE), lambda b, i: (b, i, 0, 0)),
                   pl.BlockSpec((1, N_EXP, tm), lambda b, i: (b, 0, i))),
        compiler_params=_cp(("parallel", "parallel")),
        name="outproj_router",
    )(h, *ys, wo, g.reshape(1, D), wr)


def _topk_body(aff_ref, idx_ref, q_ref, lohi_ref, sel_sc, chunk_sc, *, n_real, cap):
    Lp = aff_ref.shape[2]
    nch = Lp // LANE
    cm = n_real // LANE
    aff = aff_ref[0]
    tok = lax.broadcasted_iota(I32, aff.shape, 1)
    keys = jnp.where(tok < n_real + N_META, pltpu.bitcast(aff, I32), -1)

    def search(it, prefix):
        cand = prefix | lax.shift_left(jnp.int32(1), 30 - it)
        cnt = jnp.sum(jnp.where(keys >= cand, 1.0, 0.0), axis=1, keepdims=True)
        return jnp.where(cnt >= cap, cand, prefix)

    thr = lax.fori_loop(0, 31, search, jnp.zeros((N_EXP, 1), I32))
    need = cap - jnp.sum(jnp.where(keys > thr, 1.0, 0.0), axis=1, keepdims=True)

    ri = lax.broadcasted_iota(I32, (LANE, LANE), 0)
    ci = lax.broadcasted_iota(I32, (LANE, LANE), 1)
    upper = (ri <= ci).astype(BF16)
    er = lax.broadcasted_iota(I32, (N_EXP, N_EXP), 0)
    ec = lax.broadcasted_iota(I32, (N_EXP, N_EXP), 1)
    lower = (ec < er).astype(BF16)

    sel_sc[...] = jnp.zeros(sel_sc.shape, F32)

    def select(c, carry):
        c0 = pl.multiple_of(c * LANE, LANE)
        a = pltpu.bitcast(aff_ref[0, :, pl.ds(c0, LANE)], I32)
        t = c0 + lax.broadcasted_iota(I32, a.shape, 1)
        kc = jnp.where(t < n_real + N_META, a, -1)
        eq = (kc == thr).astype(F32)
        rank = _dot(eq.astype(BF16), upper) - eq + carry
        sel = jnp.where((kc > thr) | ((eq > 0) & (rank < need)), 1.0, 0.0)
        sel_sc[:, pl.ds(c0, LANE)] = sel
        return carry + jnp.sum(eq, axis=1, keepdims=True)

    carry = select(jnp.int32(cm), jnp.zeros((N_EXP, 1), F32))
    lax.fori_loop(0, cm, select, carry)

    chunk_sc[...] = jnp.zeros(chunk_sc.shape, F32)

    def offsets(c, carry):
        off_c = carry
        c0 = pl.multiple_of(c * LANE, LANE)
        sel = sel_sc[:, pl.ds(c0, LANE)]
        selb = sel.astype(BF16)
        nt = jnp.sum(sel, axis=0, keepdims=True)
        off_incl = _dot(jnp.broadcast_to(nt, (8, LANE)).astype(BF16), upper)[0:1] + off_c
        r = _dot(lower, selb)
        q_ref[0, :, pl.ds(c0, LANE)] = (off_incl - nt + r).astype(I32)
        lohi_ref[0, 0:1, pl.ds(c0, LANE)] = (off_incl - nt).astype(I32)
        lohi_ref[0, 1:2, pl.ds(c0, LANE)] = off_incl.astype(I32)
        within = _dot(selb, upper)
        for e in range(N_EXP):
            chunk_sc[e, pl.ds(c, 1), :] = within[e:e + 1, :]
        return off_c + jnp.sum(nt, axis=1, keepdims=True)

    lax.fori_loop(0, nch, offsets, jnp.zeros((1, 1), F32))

    ncp = chunk_sc.shape[1]
    cpad = idx_ref.shape[2]
    cr_ = lax.broadcasted_iota(I32, (ncp, ncp), 0)
    cc_ = lax.broadcasted_iota(I32, (ncp, ncp), 1)
    lower_incl = (cc_ <= cr_).astype(BF16)
    slot = lax.broadcasted_iota(I32, (1, cpad), 1).astype(F32)
    chunk_id = lax.broadcasted_iota(I32, (ncp, cpad), 0).astype(F32)
    for e in range(N_EXP):
        within = chunk_sc[e]
        tot = jnp.broadcast_to(within[:, LANE - 1:LANE], (ncp, LANE))
        cend = _dot(lower_incl, tot.astype(BF16))
        cstart = cend - tot
        cend_t = jnp.concatenate([cend] * (cpad // LANE), axis=1)
        cstart_t = jnp.concatenate([cstart] * (cpad // LANE), axis=1)
        cstar = jnp.sum(jnp.where(cend_t <= slot, 1.0, 0.0), axis=0, keepdims=True)
        onehot = chunk_id == cstar
        srel = slot - jnp.sum(jnp.where(onehot, cstart_t, 0.0), axis=0, keepdims=True)
        g = lax.dot_general(within.astype(BF16), jnp.where(onehot, 1.0, 0.0).astype(BF16), (((0,), (0,)), ((), ())),
                            preferred_element_type=F32)
        lane_in = jnp.sum(jnp.where(g <= srel, 1.0, 0.0), axis=0, keepdims=True)
        idx = jnp.where(slot < cap, cstar * LANE + lane_in, 0.0)
        idx_ref[0, e:e + 1, :] = idx.astype(I32)


def _topk(aff_t, n_real, cap, cp):
    B, E, Lp = aff_t.shape
    spec = pl.BlockSpec((1, E, Lp), lambda b: (b, 0, 0))
    ncp = -(-(Lp // LANE) // 8) * 8
    return pl.pallas_call(
        functools.partial(_topk_body, n_real=n_real, cap=cap),
        out_shape=(jax.ShapeDtypeStruct((B, E, cp), I32), jax.ShapeDtypeStruct((B, E, Lp), I32),
                   jax.ShapeDtypeStruct((B, 2, Lp), I32)),
        grid=(B,),
        in_specs=[spec],
        out_specs=(pl.BlockSpec((1, E, cp), lambda b: (b, 0, 0)), spec, pl.BlockSpec((1, 2, Lp), lambda b: (b, 0, 0))),
        scratch_shapes=[pltpu.VMEM((E, Lp), F32), pltpu.VMEM((E, ncp, LANE), F32)],
        compiler_params=_cp(("parallel",)),
        name="expert_topk",
    )(aff_t)


def _ffn_body(idxg_ref, idxw_ref, q_ref, xe_ref, wg_ref, wu_ref, wd_ref, z_ref, xbuf, x2d_sc, xb_sc, yacc, ybuf,
              gsem, ssem, *, cap, zr, n_tiles, n_r, n_f):
    s = pl.program_id(0)
    f = pl.program_id(1)
    nf = pl.num_programs(1)
    tr, D = xb_sc.shape
    B = xe_ref.shape[0]
    total = N_EXP * cap
    per = tr // n_f

    def tile(t):
        t = jnp.clip(t, 0, n_tiles - 1)
        return t // (B * n_r), (t // n_r) % B, (t % n_r) * tr

    @pl.when(f == 0)
    def _():
        @pl.when(s == 0)
        def _():
            xb_sc[...] = jnp.zeros(xb_sc.shape, BF16)
            ybuf[...] = jnp.zeros(ybuf.shape, F32)
            for bb in range(B):
                for r0 in range(total, zr + tr, tr):
                    n = min(tr, zr + tr - r0)
                    cp = pltpu.make_async_copy(ybuf.at[0:n], z_ref.at[bb, r0:r0 + n], ssem)
                    cp.start()
                    cp.wait()

        @pl.when(s >= 1)
        def _():
            pltpu.make_async_copy(xe_ref.at[0, 0:tr], xbuf, gsem).wait()
            x2d_sc[...] = xbuf[...].reshape(x2d_sc.shape)
            xb_sc[...] = x2d_sc[:, 0:D].astype(BF16)

    x = xb_sc[...]
    g = _dot(x, wg_ref[0, 0].astype(BF16))
    u = _dot(x, wu_ref[0, 0].astype(BF16))
    hmid = (g * jax.nn.sigmoid(g) * u).astype(BF16)
    contrib = _dot(hmid, wd_ref[0, 0].astype(BF16))

    _, gb, g0 = tile(s)
    _, wb, w0 = tile(s - 2)
    for i in range(per):
        j = f * per + i
        t = idxg_ref[0, 0, 0, g0 + j]
        pltpu.make_async_copy(xe_ref.at[gb, t], xbuf.at[j], gsem).start()
        live = (s >= 2) & (w0 + j < cap)
        dst = jnp.where(live, q_ref[0, 0, 0, idxw_ref[0, 0, 0, w0 + j]], zr + j)
        pltpu.make_async_copy(ybuf.at[j], z_ref.at[wb, dst], ssem).start(priority=i % 2)

    @pl.when(f == 0)
    def _():
        yacc[...] = contrib

    @pl.when(f > 0)
    def _():
        yacc[...] += contrib

    @pl.when(f == nf - 1)
    def _():
        pltpu.make_async_copy(ybuf, z_ref.at[0, 0:tr], ssem).wait()

        @pl.when(s >= 1)
        def _():
            e, _, _ = tile(s - 1)
            aff = x2d_sc[:, D:D + LANE]
            lane = lax.broadcasted_iota(I32, aff.shape, 1)
            gate = jnp.sum(jnp.where(lane == e, aff, 0.0), axis=1, keepdims=True)
            ybuf[...] = (yacc[...] * gate).reshape(tr, 1, D)

        @pl.when(s == pl.num_programs(0) - 1)
        def _():
            pltpu.make_async_copy(xe_ref.at[0, 0:tr], xbuf, gsem).wait()


def _ffn(idx, q, xe, wg, wu, wd, layer, cap, cr, zr, tf):
    B, E, _, cp = idx.shape
    Lp = q.shape[2]
    De = xe.shape[3]
    D, F = wg.shape[2], wg.shape[3]
    n_r = 3
    tr = cr // n_r
    n_f = F // tf
    n_tiles = E * B * n_r

    def at_tile(off):
        def index_map(s, f):
            t = jnp.clip(s + off, 0, n_tiles - 1)
            return ((t // n_r) % B, t // (B * n_r), 0, 0)
        return index_map

    expert = lambda s: jnp.clip(s - 1, 0, n_tiles - 1) // (B * n_r)
    smem = lambda n, off: pl.BlockSpec((1, 1, 1, n), at_tile(off), memory_space=pltpu.SMEM)
    return pl.pallas_call(
        functools.partial(_ffn_body, cap=cap, zr=zr, n_tiles=n_tiles, n_r=n_r, n_f=n_f),
        out_shape=jax.ShapeDtypeStruct((B, zr + tr, 1, D), F32),
        grid=(n_tiles + 2, n_f),
        in_specs=[smem(cp, 0), smem(cp, -2), smem(Lp, -2), pl.BlockSpec(memory_space=pl.ANY),
                  pl.BlockSpec((1, 1, D, tf), lambda s, f: (layer, expert(s), 0, f)),
                  pl.BlockSpec((1, 1, D, tf), lambda s, f: (layer, expert(s), 0, f)),
                  pl.BlockSpec((1, 1, tf, D), lambda s, f: (layer, expert(s), f, 0))],
        out_specs=pl.BlockSpec(memory_space=pl.ANY),
        scratch_shapes=[pltpu.VMEM((tr, 1, De), F32), pltpu.VMEM((tr, De), F32), pltpu.VMEM((tr, D), BF16),
                        pltpu.VMEM((tr, D), F32),
                        pltpu.VMEM((tr, 1, D), F32), pltpu.SemaphoreType.DMA(()), pltpu.SemaphoreType.DMA(())],
        compiler_params=_cp(("arbitrary", "arbitrary"), 56),
        name="expert_ffn",
    )(idx, idx, q.reshape(B, E, 1, Lp), xe, wg, wu, wd)


def _combine_body(k0_ref, k1_ref, h_ref, lohi_ref, g_ref, z_ref, *rest, nblk, want_h):
    if want_h:
        h2_ref, u_ref, buf, zb_sc, acc_sc, sem = rest
    else:
        h2_ref, (u_ref, buf, zb_sc, acc_sc, sem) = None, rest
    b = pl.program_id(0)
    n = b * nblk + pl.program_id(1)
    n_total = pl.num_programs(0) * nblk
    k0 = k0_ref[n]
    k1 = k1_ref[n]
    lo = lohi_ref[0, 0:1, :]
    hi = lohi_ref[0, 1:2, :]
    acc_sc[...] = jnp.zeros(acc_sc.shape, F32)

    def copy(bb, k):
        r0 = pl.multiple_of(k * ZCHUNK, ZCHUNK)
        slot = lax.rem(k, 2)
        return pltpu.make_async_copy(z_ref.at[bb, pl.ds(r0, ZCHUNK)], buf.at[slot], sem.at[slot])

    def resident(n_prev, n_next):
        return ((n_prev // nblk == n_next // nblk) & (k1_ref[n_prev] > k0_ref[n_prev])
                & (k1_ref[n_prev] - 1 == k0_ref[n_next]))

    @pl.when(k1 > k0)
    def _():
        @pl.when(n == 0)
        def _():
            copy(b, k0).start()

        @pl.when((n == 0) | jnp.logical_not(resident(jnp.maximum(n - 1, 0), n)))
        def _():
            copy(b, k0).wait()

    def chunk(k, c):
        @pl.when(k + 1 < k1)
        def _():
            copy(b, k + 1).start()

        w = k * ZCHUNK + lax.broadcasted_iota(I32, (ZCHUNK, lo.shape[1]), 0)
        band_t = jnp.where((w >= lo) & (w < hi), 1.0, 0.0)
        zb_sc[...] = buf[lax.rem(k, 2)].reshape(zb_sc.shape)
        acc_sc[...] += _dot(band_t.T.astype(BF16), zb_sc[...].astype(BF16))

        @pl.when(k + 1 < k1)
        def _():
            copy(b, k + 1).wait()
        return c

    lax.fori_loop(k0, k1, chunk, 0)

    n_next = jnp.minimum(n + 1, n_total - 1)

    @pl.when((n + 1 < n_total) & (k1_ref[n_next] > k0_ref[n_next]) & jnp.logical_not(resident(n, n_next)))
    def _():
        copy(n_next // nblk, k0_ref[n_next]).start()

    h2 = h_ref[0] + acc_sc[...]
    if want_h:
        h2_ref[0] = h2
    u_ref[0] = _rms(h2, g_ref[...]).astype(u_ref.dtype)


def _combine(h1, lohi, z, g, n_rows, u_dtype, want_h):
    B, Lp, D = h1.shape
    nblk = n_rows // TOKB
    lo = lohi[:, 0, :n_rows].reshape(B, nblk, TOKB)
    hi = lohi[:, 1, :n_rows].reshape(B, nblk, TOKB)
    k0 = (lo[:, :, 0] // ZCHUNK).reshape(-1).astype(I32)
    k1 = ((hi[:, :, -1] + ZCHUNK - 1) // ZCHUNK).reshape(-1).astype(I32)
    k1 = jnp.where(hi[:, :, -1].reshape(-1) > lo[:, :, 0].reshape(-1), k1, k0)
    row = lambda b, j, *_: (b, j, 0)
    grid_spec = pltpu.PrefetchScalarGridSpec(
        num_scalar_prefetch=2,
        grid=(B, nblk),
        in_specs=[pl.BlockSpec((1, TOKB, D), row),
                  pl.BlockSpec((1, 2, TOKB), lambda b, j, *_: (b, 0, j)),
                  pl.BlockSpec((1, D), lambda b, j, *_: (0, 0)),
                  pl.BlockSpec(memory_space=pl.ANY)],
        out_specs=(pl.BlockSpec((1, TOKB, D), row),) * (2 if want_h else 1),
        scratch_shapes=[pltpu.VMEM((2, ZCHUNK, 1, D), z.dtype), pltpu.VMEM((ZCHUNK, D), F32),
                        pltpu.VMEM((TOKB, D), F32), pltpu.SemaphoreType.DMA((2,))],
    )
    out_shape = (jax.ShapeDtypeStruct((B, n_rows, D), F32),) * want_h + (jax.ShapeDtypeStruct((B, n_rows, D), u_dtype),)
    outs = pl.pallas_call(
        functools.partial(_combine_body, nblk=nblk, want_h=want_h),
        out_shape=out_shape,
        grid_spec=grid_spec,
        compiler_params=_cp(("arbitrary", "arbitrary")),
        name="moe_combine",
    )(k0, k1, h1, lohi, g.reshape(1, D), z)
    return outs if want_h else (None, outs[0])


def _moe(h1, xe, aff_t, wg, wu, wd, layer, g_next, n_real, n_rows_out, u_dtype, want_h):
    B, Lp, D = h1.shape
    cap = (CAP_F * (n_real + N_META)) // N_EXP
    cr = -(-cap // 48) * 48
    cp = -(-(cap + 1) // LANE) * LANE
    zr = -(-(N_EXP * cap) // ZCHUNK) * ZCHUNK
    if zr == N_EXP * cap:
        zr += ZCHUNK
    idx, q, lohi = _topk(aff_t, n_real, cap, cp)
    idx = idx.reshape(B, N_EXP, 1, cp)
    z = _ffn(idx, q, xe, wg, wu, wd, layer, cap, cr, zr, min(wg.shape[3], 256))
    return _combine(h1, lohi, z, g_next, n_rows_out, u_dtype, want_h)


def _rope_tables(n_real, Lp):
    pos = jnp.concatenate([jnp.arange(n_real) + N_META, jnp.arange(N_META), jnp.zeros((Lp - n_real - N_META,), I32)])
    inv_freq = 1.0 / (ROPE_THETA ** (jnp.arange(0, MLA_ROPE, 2, dtype=F32) / MLA_ROPE))
    ang = pos.astype(F32)[:, None] * inv_freq[None, :]
    cos, sin = jnp.cos(ang), jnp.sin(ang)
    half = MLA_ROPE // 2
    z = lambda n: jnp.zeros((Lp, n), F32)
    c = jnp.concatenate([cos, cos, z(LANE - 2 * half)], axis=1)
    sn = jnp.concatenate([-sin, z(LANE - half)], axis=1)
    sp = jnp.concatenate([z(half), sin, z(LANE - 2 * half)], axis=1)
    return (c, sn, sp), (cos.T, sin.T)


def kernel(x, meta_tokens, norm_mix, norm_ffn, norm_final, ab_w_in, ab_q_norm, ab_kv_norm, ab_w_uq, ab_w_ukv, ab_rpb, ab_w_out, c_w_in, c_sinks, c_w_out, ec_w_router, ec_w_gate, ec_w_up, ec_w_down):
    B, S, D = x.shape
    depth = norm_mix.shape[0]
    Lp = S + TAIL
    tm = 512
    h, u = _embed(x, meta_tokens.astype(x.dtype), norm_mix[0], tm)
    tabs, tabs_t = _rope_tables(S, Lp)

    for layer in range(depth):
        if layer % 2 == 0:
            e = layer // 2
            w_in = ab_w_in[e].astype(BF16)
            o2 = MLA_QL + MLA_KVL
            o3 = o2 + MLA_ROPE
            lat = _mm(u, w_in[:, :o2], tm, o2)
            krope = _mm(u, jnp.pad(w_in[:, o2:o3], ((0, 0), (0, LANE - MLA_ROPE))), tm, LANE)
            nw = NA_H * NA_D
            na_wq = ab_w_in[e][:, o3:o3 + nw] * (NA_D ** -0.5 * math.log2(math.e))
            na_wqv_t = jnp.concatenate([na_wq, ab_w_in[e][:, o3 + 2 * nw:]], axis=1).T.astype(BF16)
            na_qvt = _mm_t(u, na_wqv_t, tm, nw)
            na_k = _mm(u, w_in[:, o3 + nw:o3 + 2 * nw], tm, nw)
            wq = jnp.pad(ab_w_uq[e].reshape(MLA_QL, MLA_H, MLA_NOPE + MLA_ROPE),
                         ((0, 0), (0, 0), (0, MLA_HP - MLA_NOPE - MLA_ROPE))).reshape(MLA_QL, MLA_H * MLA_HP)
            wkv = ab_w_ukv[e].reshape(MLA_KVL, MLA_H, MLA_NOPE + MLA_V)
            wk = wkv[:, :, :MLA_NOPE].reshape(MLA_KVL, -1)
            wv = wkv[:, :, MLA_NOPE:].reshape(MLA_KVL, -1)
            qt, k, vt = _mla_up(lat, krope, ab_q_norm[e], ab_kv_norm[e], wq.T.astype(BF16), wk.astype(BF16),
                                wv.T.astype(BF16), tabs, tabs_t, tm)
            y_mla = _mla_attention(qt, k, vt, S, TAIL, min(S, 512))
            y_na = _na_attention(na_qvt, na_k, _na_bias_table(ab_rpb[e]), S)
            ys, y_t = [y_mla, y_na], (False, True)
            w_out = ab_w_out[e]
        else:
            o = layer // 2
            qw, kw = SWA_H * SWA_D, SWA_KV * SWA_D
            w_in = c_w_in[o]
            wq = w_in[:, :qw] * (SWA_D ** -0.5 * math.log2(math.e))
            wqv_t = jnp.concatenate([wq, w_in[:, qw + kw:]], axis=1).T.astype(BF16)
            qvt = _mm_t(u, wqv_t, tm, (qw + kw) // 3)
            k = _mm(u, w_in[:, qw:qw + kw].astype(BF16), tm, kw)
            ys, y_t = [_swa_attention(qvt, k, c_sinks[o], S)], (True,)
            w_out = c_w_out[o]
        wr = jnp.pad(ec_w_router[layer], ((0, 0), (0, LANE - N_EXP)))
        wr_hi = wr.astype(BF16)
        wr = jnp.concatenate([wr_hi, (wr - wr_hi.astype(F32)).astype(BF16)], axis=1)
        h1, xe, aff_t = _outproj(h, ys, w_out.astype(BF16), norm_ffn[layer], wr, 256, y_t)
        last = layer == depth - 1
        g_next = norm_final if last else norm_mix[layer + 1]
        h, u = _moe(h1, xe, aff_t, ec_w_gate, ec_w_up, ec_w_down, layer, g_next, S,
                    S if last else Lp, F32 if last else BF16, want_h=not last)
    return u
```
